```python
import math
import jax
import jax.numpy as jnp
from jax import lax
import numpy as np

D_MODEL = 1024
BATCH = 8
SEQ = 2048
DEPTH = 4
DEC_BATCH = 32
DEC_SEQ = 4
PAST_LEN = 16384
PAGE_SIZE = 128

QB = 128
NORM_EPS = 1e-6
NEG = -1e30
RES_HALF = 0.5
N_MOD = 9
D_FF = 2816

H_A = 4
DK_A = 128
DV_A = 128
HGRN_CHUNK = 64

H_B = 8
Q_LORA = 256
KV_LORA = 128
NOPE = 64
ROPE_D = 32
VD_B = 64
ROPE_BASE = 10000.0

DH = 64
H_C = 8
G_C = 2
HPG_C = H_C // G_C
NSA_BLOCK = 64
TOP_N = 8
WINDOW = 512
FORCE_BONUS = 100.0

H_D = 8
KV_D = 4
HPK_D = H_D // KV_D

NUM_BUCKETS = 32
MAX_DISTANCE = 128

N_EVEN = (DEPTH + 1) // 2
N_ODD = DEPTH // 2
EVEN_SPLITS = [H_A * DK_A, H_A * DK_A, H_A * DV_A, H_A * DV_A, Q_LORA, KV_LORA, ROPE_D]
IN_EVEN = sum(EVEN_SPLITS)
MIX_EVEN = H_A * DV_A + H_B * VD_B
NSA_KV = G_C * DH
NSA_ROW = 4 * NSA_KV
WIN_ROW = 2 * NSA_KV
SB_KV = KV_D * DH
SB_ROW = 2 * SB_KV
ODD_SPLITS = [H_C * DH, NSA_ROW + WIN_ROW, 3 * H_C, H_D * DH, SB_KV, SB_KV]
IN_ODD = sum(ODD_SPLITS)
MIX_ODD = H_C * DH + H_D * DH
MLA_ROW = KV_LORA + ROPE_D

kernel_name = 'hybrid_hgrn2_mla_nsa_stickbreak_step'


def rmsnorm(x, g):
    xf = x.astype(jnp.float32)
    y = xf * lax.rsqrt(jnp.mean(xf * xf, axis=-1, keepdims=True) + NORM_EPS)
    return (y * g.astype(jnp.float32)).astype(x.dtype)


def split_cols(z, sizes):
    return jnp.split(z, np.cumsum(sizes)[:-1].tolist(), axis=-1)


def swiglu(h, wg, wu, wd):
    return (jax.nn.silu(h @ wg) * (h @ wu)) @ wd


def rope(x, pos):
    half = ROPE_D // 2
    inv = ROPE_BASE ** (-jnp.arange(half, dtype=jnp.float32) / half)
    ang = pos.astype(jnp.float32)[:, None] * inv[None, :]
    shape = (ang.shape[0],) + (1,) * (x.ndim - 3) + (half,)
    cos, sin = jnp.cos(ang).reshape(shape), jnp.sin(ang).reshape(shape)
    xf = x.astype(jnp.float32)
    x1, x2 = xf[..., :half], xf[..., half:]
    return jnp.concatenate([x1 * cos - x2 * sin, x1 * sin + x2 * cos], axis=-1).astype(x.dtype)


def masked_softmax(s, mask):
    return jax.nn.softmax(jnp.where(mask, s, NEG), axis=-1) * mask


def t5_bucket(dist):
    exact = NUM_BUCKETS // 2
    d = jnp.maximum(dist, 0)
    large = exact + (jnp.log(jnp.maximum(d, 1).astype(jnp.float32) / exact)
                     / math.log(MAX_DISTANCE / exact) * (NUM_BUCKETS - exact)).astype(jnp.int32)
    return jnp.where(d < exact, d, jnp.minimum(large, NUM_BUCKETS - 1))


def t5_bias_grouped(rel_bias, dist):
    b = rel_bias[t5_bucket(dist)]
    return jnp.transpose(b, (2, 0, 1)).reshape(G_C, HPG_C, dist.shape[0], dist.shape[1]).astype(jnp.float32)


def t5_bias_gathered(rel_bias, dist):
    tbl = rel_bias.reshape(NUM_BUCKETS, G_C, HPG_C).transpose(1, 0, 2)
    b = tbl[jnp.arange(G_C)[None, :, None, None], t5_bucket(dist)]
    return jnp.moveaxis(b, -1, 2).astype(jnp.float32)


def gather_pages(cache, page_table):
    rows = cache[page_table]
    return rows.reshape(rows.shape[0], rows.shape[1] * rows.shape[2], rows.shape[3])


def sweep_queries(fn, qpos, *qs):
    T = qpos.shape[0]
    if T <= QB or T % QB:
        return fn(qpos, *qs)
    n = T // QB

    def blk(a):
        return jnp.swapaxes(a.reshape(a.shape[0], n, QB, *a.shape[2:]), 0, 1)

    out = lax.map(lambda args: fn(*args), (qpos.reshape(n, QB),) + tuple(blk(a) for a in qs))
    out = jnp.swapaxes(out, 0, 1)
    return out.reshape(out.shape[0], T, *out.shape[3:])


def last_rows(rows, n):
    T = rows.shape[1]
    if T < n:
        rows = jnp.pad(rows, ((0, 0), (n - T, 0), (0, 0)))
    return rows[:, rows.shape[1] - n:]


def hgrn_chunked(q, k, v, logf, s0):
    B, T, H, _ = q.shape
    C = HGRN_CHUNK if T % HGRN_CHUNK == 0 else T
    n = T // C

    def chunks(a):
        return a.astype(jnp.float32).reshape(B, n, C, H, a.shape[-1]).transpose(1, 0, 3, 2, 4)

    causal = jnp.tril(jnp.ones((C, C), bool))[:, :, None]

    def step(S, inp):
        qc, kc, vc, gc = inp
        G = jnp.cumsum(gc, axis=2)
        diff = G[:, :, :, None, :] - G[:, :, None, :, :]
        decay = jnp.where(causal, jnp.exp(jnp.where(causal, diff, 0.0)), 0.0)
        a = jnp.einsum('bhtk,bhsk,bhtsk->bhts', qc, kc, decay)
        o = (jnp.einsum('bhtk,bhkv->bhtv', qc * jnp.exp(G), S)
             + jnp.einsum('bhts,bhsv->bhtv', a, vc))
        g_last = G[:, :, -1:, :]
        S = (jnp.exp(g_last[:, :, 0, :, None]) * S
             + jnp.einsum('bhsk,bhsv->bhkv', kc * jnp.exp(g_last - G), vc))
        return S, o

    S, o = lax.scan(step, s0, tuple(chunks(a) for a in (q, k, v, logf)))
    return o.transpose(1, 0, 3, 2, 4).reshape(B, T, H, v.shape[-1]), S


def mla_attend(q_lat, q_rope, qpos, keys, kpos):
    c_kv, k_rope = keys[..., :KV_LORA], keys[..., KV_LORA:]
    s = (jnp.einsum('bqhc,blc->bhql', q_lat, c_kv, preferred_element_type=jnp.float32)
         + jnp.einsum('bqhr,blr->bhql', q_rope, k_rope, preferred_element_type=jnp.float32))
    s = s * (NOPE + ROPE_D) ** -0.5
    p = masked_softmax(s, kpos[None, :] <= qpos[:, None])
    return jnp.einsum('bhql,blc->bqhc', p.astype(keys.dtype), c_kv)


def nsa_branches(q, qpos, gates, cmp_k, cmp_v, slc_k, slc_v, kw, vw, kwpos, rel_bias):
    B, Q = q.shape[0], q.shape[1]
    nblk = cmp_k.shape[1]
    scale = DH ** -0.5
    qg = q.reshape(B, Q, G_C, HPG_C, DH)
    blk = jnp.arange(nblk, dtype=jnp.int32)
    blk_end = blk * NSA_BLOCK + NSA_BLOCK - 1
    vis = blk_end[None, :] <= qpos[:, None]
    s = jnp.einsum('bqgjd,bngd->bgjqn', qg, cmp_k, preferred_element_type=jnp.float32) * scale
    s = s + t5_bias_grouped(rel_bias, qpos[:, None] - blk_end[None, :])
    p_cmp = masked_softmax(s, vis)
    o_cmp = jnp.einsum('bgjqn,bngd->bqgjd', p_cmp.astype(q.dtype), cmp_v)
    imp = p_cmp.sum(axis=2)
    cur = qpos // NSA_BLOCK
    allowed = blk[None, :] <= cur[:, None]
    forced = (blk[None, :] == 0) | (blk[None, :] == cur[:, None]) | (blk[None, :] == cur[:, None] - 1)
    sel_score = jnp.where(allowed, imp + FORCE_BONUS * forced, NEG)
    n_sel = min(TOP_N, nblk)
    top_val, top_idx = lax.top_k(sel_score, n_sel)
    slot_ok = top_val > 0.5 * NEG
    take_blocks = jax.vmap(jax.vmap(lambda kb, ix: kb[ix], in_axes=(2, 0)), in_axes=(0, 0))
    kg = take_blocks(slc_k, top_idx).reshape(B, G_C, Q, n_sel * NSA_BLOCK, DH)
    vg = take_blocks(slc_v, top_idx).reshape(B, G_C, Q, n_sel * NSA_BLOCK, DH)
    kpos5 = top_idx[..., None] * NSA_BLOCK + jnp.arange(NSA_BLOCK, dtype=jnp.int32)
    smask = (slot_ok[..., None] & (kpos5 <= qpos[:, None, None])).reshape(B, G_C, Q, -1)
    kpos = kpos5.reshape(B, G_C, Q, -1)
    s = jnp.einsum('bqgjd,bgqkd->bgjqk', qg, kg, preferred_element_type=jnp.float32) * scale
    s = s + t5_bias_gathered(rel_bias, qpos[:, None] - kpos)
    p = masked_softmax(s, smask[:, :, None])
    o_slc = jnp.einsum('bgjqk,bgqkd->bqgjd', p.astype(q.dtype), vg)
    wdist = qpos[:, None] - kwpos[None, :]
    wmask = (wdist >= 0) & (wdist <= WINDOW) & (kwpos[None, :] >= 0)
    s = jnp.einsum('bqgjd,blgd->bgjql', qg, kw, preferred_element_type=jnp.float32) * scale
    s = s + t5_bias_grouped(rel_bias, wdist)
    p = masked_softmax(s, wmask)
    o_win = jnp.einsum('bgjql,blgd->bqgjd', p.astype(q.dtype), vw)
    gt = gates.reshape(B, Q, G_C, HPG_C, 3)
    o = gt[..., 0:1] * o_cmp + gt[..., 1:2] * o_slc + gt[..., 2:3] * o_win
    return o.reshape(B, Q, H_C * DH).astype(q.dtype)


def stick_breaking(q, qpos, k, v, kpos):
    B, Q = q.shape[0], q.shape[1]
    qg = q.reshape(B, Q, KV_D, HPK_D, DH)
    z = jnp.einsum('bqgjd,blgd->bgjql', qg, k, preferred_element_type=jnp.float32) * DH ** -0.5
    mask = kpos[None, :] < qpos[:, None]
    log_fail = jnp.where(mask, jax.nn.log_sigmoid(-z), 0.0)
    between = lax.cumsum(log_fail, axis=4, reverse=True) - log_fail
    a = jnp.where(mask, jnp.exp(jax.nn.log_sigmoid(z) + between), 0.0)
    o = jnp.einsum('bgjql,blgd->bqgjd', a.astype(v.dtype), v)
    return o.reshape(B, Q, H_D * DH)


def even_mixer(h, pos, lj, w, past):
    B, T = h.shape[0], h.shape[1]
    zq, zf, zi, zg, zqa, zkv, zkr = split_cols(h @ w['w_in_even'][lj], EVEN_SPLITS)

    def heads(a, d):
        return a.reshape(B, T, H_A, d)

    lb_all = jax.nn.softmax(w['hgrn_lb'].astype(jnp.float32), axis=0)
    lb = (jnp.cumsum(lb_all, axis=0) - lb_all[0])[lj]
    zf32 = zf.astype(jnp.float32)
    ls_f = jax.nn.log_sigmoid(zf32)
    lb_pos = lb > 0
    log_lb = jnp.log(jnp.where(lb_pos, lb, 1.0))
    logf = jnp.where(lb_pos, jnp.logaddexp(log_lb, jnp.log1p(-lb) + ls_f), ls_f)
    k_in = (1.0 - lb) * jax.nn.sigmoid(-zf32)
    if past is None:
        s0 = jnp.zeros((B, H_A, DK_A, DV_A), jnp.float32)
    else:
        s0 = past[1].astype(jnp.float32)
    o_a, s_new = hgrn_chunked(heads(jax.nn.silu(zq), DK_A), heads(k_in, DK_A),
                              heads(zi, DV_A), heads(logf, DK_A), s0)
    o_a = rmsnorm(o_a.astype(h.dtype), w['hgrn_norm_g'][lj].reshape(H_A, DV_A)) * jax.nn.silu(heads(zg, DV_A))
    qa = rmsnorm(zqa, w['mla_q_norm_g'][lj])
    qb = (qa @ w['mla_w_qb'][lj]).reshape(B, T, H_B, NOPE + ROPE_D)
    q_lat = jnp.einsum('bthn,hcn->bthc', qb[..., :NOPE], w['mla_w_kb'][lj])
    q_rope = rope(qb[..., NOPE:], pos)
    rows = jnp.concatenate([rmsnorm(zkv, w['mla_kv_norm_g'][lj]), rope(zkr, pos)], axis=-1)
    if past is None:
        keys = rows
    else:
        keys = jnp.concatenate([gather_pages(past[0], past[2]).astype(rows.dtype), rows], axis=1)
    kpos = jnp.arange(keys.shape[1], dtype=jnp.int32)
    o_lat = sweep_queries(lambda qp, ql, qr: mla_attend(ql, qr, qp, keys, kpos), pos, q_lat, q_rope)
    o_b = jnp.einsum('bthc,hcd->bthd', o_lat, w['mla_w_vb'][lj])
    o = jnp.concatenate([o_a.reshape(B, T, -1), o_b.reshape(B, T, -1).astype(h.dtype)], axis=-1)
    return o @ w['w_out_even'][lj], (rows, s_new)


def odd_mixer(h, pos, lj, w, past, win_len):
    B, T = h.shape[0], h.shape[1]
    zq, zkv, zg, sq, sk, sv = split_cols(h @ w['w_in_odd'][lj], ODD_SPLITS)
    q = zq.reshape(B, T, H_C, DH)
    gates = jax.nn.sigmoid(zg.astype(jnp.float32)).reshape(B, T, H_C, 3)
    nsa_rows, win_rows = zkv[..., :NSA_ROW], zkv[..., NSA_ROW:]
    sb_rows = jnp.concatenate([sk, sv], axis=-1)
    if past is None:
        full = nsa_rows
        kv_win = jnp.pad(win_rows, ((0, 0), (WINDOW, 0), (0, 0)))
        win_pos = None
        sb_keys = sb_rows
        new_win = last_rows(win_rows, win_len)
    else:
        cache_nsa, state_win, cache_sb, page_table = past
        p_len = page_table.shape[1] * PAGE_SIZE
        wb = state_win.shape[1]
        full = jnp.concatenate([gather_pages(cache_nsa, page_table).astype(nsa_rows.dtype), nsa_rows], axis=1)
        kv_win = jnp.concatenate([state_win.astype(win_rows.dtype), win_rows], axis=1)
        win_pos = jnp.arange(p_len - wb, p_len + T, dtype=jnp.int32)
        sb_keys = jnp.concatenate([gather_pages(cache_sb, page_table).astype(sb_rows.dtype), sb_rows], axis=1)
        new_win = kv_win[:, kv_win.shape[1] - wb:]
    L = full.shape[1]
    lp = -(-L // NSA_BLOCK) * NSA_BLOCK
    full = jnp.pad(full, ((0, 0), (0, lp - L), (0, 0)))
    blocks = full.reshape(B, lp // NSA_BLOCK, NSA_BLOCK, 4, G_C, DH)
    cmp_k = blocks[:, :, :, 0].astype(jnp.float32).mean(axis=2).astype(h.dtype)
    cmp_v = blocks[:, :, :, 1].astype(jnp.float32).mean(axis=2).astype(h.dtype)
    slc_k, slc_v = blocks[:, :, :, 2], blocks[:, :, :, 3]
    kw_all = kv_win[..., :NSA_KV].reshape(B, -1, G_C, DH)
    vw_all = kv_win[..., NSA_KV:].reshape(B, -1, G_C, DH)

    def nsa_fn(qpos_b, q_b, g_b):
        if win_pos is None:
            start = qpos_b[0]
            n = qpos_b.shape[0] + WINDOW
            kw = lax.dynamic_slice_in_dim(kw_all, start, n, axis=1)
            vw = lax.dynamic_slice_in_dim(vw_all, start, n, axis=1)
            kwpos = start - WINDOW + jnp.arange(n, dtype=jnp.int32)
        else:
            kw, vw, kwpos = kw_all, vw_all, win_pos
        return nsa_branches(q_b, qpos_b, g_b, cmp_k, cmp_v, slc_k, slc_v, kw, vw, kwpos, w['rel_bias'])

    o_c = sweep_queries(nsa_fn, pos, q, gates)
    ls = sb_keys.shape[1]
    k_sb = sb_keys[..., :SB_KV].reshape(B, ls, KV_D, DH)
    v_sb = sb_keys[..., SB_KV:].reshape(B, ls, KV_D, DH)
    sb_pos = jnp.arange(ls, dtype=jnp.int32)
    o_d = sweep_queries(lambda qp, qb: stick_breaking(qb, qp, k_sb, v_sb, sb_pos), pos, sq.reshape(B, T, H_D, DH))
    o = jnp.concatenate([o_c.astype(h.dtype), o_d.astype(h.dtype)], axis=-1)
    return o @ w['w_out_odd'][lj], (nsa_rows, new_win, sb_rows)


def block_layer(x, c, l, w, mixer):
    mod = (jax.nn.silu(c) @ w['w_ada'][l] + w['b_ada'][l]).reshape(c.shape[0], N_MOD, 1, D_MODEL)

    def pre(v, i):
        return rmsnorm(v, w['norm_g'][l, i]) * (1.0 + mod[:, 3 * i + 1]) + mod[:, 3 * i]

    def ffn(hh, j):
        return swiglu(hh, w['ffn_w_gate'][l, j], w['ffn_w_up'][l, j], w['ffn_w_down'][l, j])

    x = x + RES_HALF * mod[:, 2] * ffn(pre(x, 0), 0)
    o, st = mixer(pre(x, 1))
    x = x + mod[:, 5] * o
    x = x + RES_HALF * mod[:, 8] * ffn(pre(x, 2), 1)
    return x, st


def run_group(x, c, pos, w, pasts, win_len):
    states = []
    for l in range(DEPTH):
        if l % 2 == 0:
            mixer = lambda hh, l=l: even_mixer(hh, pos, l // 2, w, pasts[l])
        else:
            mixer = lambda hh, l=l: odd_mixer(hh, pos, l // 2, w, pasts[l], win_len)
        x, st = block_layer(x, c, l, w, mixer)
        states.append(st)
    return rmsnorm(x, w['final_norm_g']), states


def setup_inputs(seed: int = 0) -> dict:
    key = jax.random.key(seed)
    keys = iter(jax.random.split(key, 48))

    def nrm(shape, scale=1.0):
        return jax.random.normal(next(keys), shape, jnp.float32) * scale

    def gain(shape):
        return 1.0 + nrm(shape, 0.05)

    n_pages = PAST_LEN // PAGE_SIZE
    n_used = DEC_BATCH * n_pages
    n_pool = n_used + (n_used + 3) // 4
    win_len = min(WINDOW, PAST_LEN)
    page_table = jax.random.permutation(next(keys), n_pool)[:n_used].reshape(DEC_BATCH, n_pages).astype(jnp.int32)
    d = {}
    d['x_prompt'] = nrm((BATCH, SEQ, D_MODEL))
    d['x_sample'] = nrm((DEC_BATCH, DEC_SEQ, D_MODEL))
    d['cache_mla_l0'] = nrm((n_pool, PAGE_SIZE, MLA_ROW))
    d['state_hgrn_l0'] = nrm((DEC_BATCH, H_A, DK_A, DV_A), 0.5)
    d['cache_nsa_l1'] = nrm((n_pool, PAGE_SIZE, NSA_ROW))
    d['state_win_l1'] = nrm((DEC_BATCH, win_len, WIN_ROW))
    d['cache_sb_l1'] = nrm((n_pool, PAGE_SIZE, SB_ROW))
    d['cache_mla_l2'] = nrm((n_pool, PAGE_SIZE, MLA_ROW))
    d['state_hgrn_l2'] = nrm((DEC_BATCH, H_A, DK_A, DV_A), 0.5)
    d['cache_nsa_l3'] = nrm((n_pool, PAGE_SIZE, NSA_ROW))
    d['state_win_l3'] = nrm((DEC_BATCH, win_len, WIN_ROW))
    d['cache_sb_l3'] = nrm((n_pool, PAGE_SIZE, SB_ROW))
    d['page_table'] = page_table
    d['c_prompt'] = nrm((BATCH, D_MODEL))
    d['c_sample'] = nrm((DEC_BATCH, D_MODEL))
    d['w_ada'] = nrm((DEPTH, D_MODEL, N_MOD * D_MODEL), D_MODEL ** -0.5)
    d['b_ada'] = nrm((DEPTH, N_MOD * D_MODEL), 0.02)
    d['norm_g'] = gain((DEPTH, 3, D_MODEL))
    d['ffn_w_gate'] = nrm((DEPTH, 2, D_MODEL, D_FF), D_MODEL ** -0.5)
    d['ffn_w_up'] = nrm((DEPTH, 2, D_MODEL, D_FF), D_MODEL ** -0.5)
    d['ffn_w_down'] = nrm((DEPTH, 2, D_FF, D_MODEL), D_FF ** -0.5)
    d['w_in_even'] = nrm((N_EVEN, D_MODEL, IN_EVEN), D_MODEL ** -0.5)
    d['w_out_even'] = nrm((N_EVEN, MIX_EVEN, D_MODEL), MIX_EVEN ** -0.5)
    d['hgrn_lb'] = 1.0 + nrm((N_EVEN, H_A * DK_A), 0.1)
    d['hgrn_norm_g'] = gain((N_EVEN, H_A * DV_A))
    d['mla_q_norm_g'] = gain((N_EVEN, Q_LORA))
    d['mla_kv_norm_g'] = gain((N_EVEN, KV_LORA))
    d['mla_w_qb'] = nrm((N_EVEN, Q_LORA, H_B * (NOPE + ROPE_D)), Q_LORA ** -0.5)
    d['mla_w_kb'] = nrm((N_EVEN, H_B, KV_LORA, NOPE), KV_LORA ** -0.5)
    d['mla_w_vb'] = nrm((N_EVEN, H_B, KV_LORA, VD_B), KV_LORA ** -0.5)
    d['w_in_odd'] = nrm((N_ODD, D_MODEL, IN_ODD), D_MODEL ** -0.5)
    d['w_out_odd'] = nrm((N_ODD, MIX_ODD, D_MODEL), MIX_ODD ** -0.5)
    d['rel_bias'] = nrm((NUM_BUCKETS, H_C), 0.5)
    d['final_norm_g'] = gain((D_MODEL,))
    return d


def reference(x_prompt, x_sample,
              cache_mla_l0, state_hgrn_l0, cache_nsa_l1, state_win_l1, cache_sb_l1,
              cache_mla_l2, state_hgrn_l2, cache_nsa_l3, state_win_l3, cache_sb_l3,
              page_table, c_prompt, c_sample,
              w_ada, b_ada, norm_g, ffn_w_gate, ffn_w_up, ffn_w_down,
              w_in_even, w_out_even, hgrn_lb, hgrn_norm_g, mla_q_norm_g, mla_kv_norm_g,
              mla_w_qb, mla_w_kb, mla_w_vb, w_in_odd, w_out_odd, rel_bias, final_norm_g):
    w = dict(w_ada=w_ada, b_ada=b_ada, norm_g=norm_g, ffn_w_gate=ffn_w_gate, ffn_w_up=ffn_w_up,
             ffn_w_down=ffn_w_down, w_in_even=w_in_even, w_out_even=w_out_even, hgrn_lb=hgrn_lb,
             hgrn_norm_g=hgrn_norm_g, mla_q_norm_g=mla_q_norm_g, mla_kv_norm_g=mla_kv_norm_g,
             mla_w_qb=mla_w_qb, mla_w_kb=mla_w_kb, mla_w_vb=mla_w_vb, w_in_odd=w_in_odd,
             w_out_odd=w_out_odd, rel_bias=rel_bias, final_norm_g=final_norm_g)
    win_len = state_win_l1.shape[1]
    past_len = page_table.shape[1] * PAGE_SIZE
    pos_p = jnp.arange(x_prompt.shape[1], dtype=jnp.int32)
    pos_s = past_len + jnp.arange(x_sample.shape[1], dtype=jnp.int32)
    pasts = [(cache_mla_l0, state_hgrn_l0, page_table),
             (cache_nsa_l1, state_win_l1, cache_sb_l1, page_table),
             (cache_mla_l2, state_hgrn_l2, page_table),
             (cache_nsa_l3, state_win_l3, cache_sb_l3, page_table)]
    y_prompt, st_p = run_group(x_prompt, c_prompt, pos_p, w, [None] * DEPTH, win_len)
    y_sample, st_s = run_group(x_sample, c_sample, pos_s, w, pasts, win_len)
    (mla0_p, hgrn0_p), (nsa1_p, win1_p, sb1_p), (mla2_p, hgrn2_p), (nsa3_p, win3_p, sb3_p) = st_p
    (mla0_s, hgrn0_s), (nsa1_s, win1_s, sb1_s), (mla2_s, hgrn2_s), (nsa3_s, win3_s, sb3_s) = st_s
    return (y_prompt, y_sample,
            mla0_p, mla0_s, hgrn0_p, hgrn0_s,
            nsa1_p, nsa1_s, win1_p, win1_s, sb1_p, sb1_s,
            mla2_p, mla2_s, hgrn2_p, hgrn2_s,
            nsa3_p, nsa3_s, win3_p, win3_s, sb3_p, sb3_s)
```

```python
import functools
import math

import numpy as np
import jax
import jax.numpy as jnp
from jax import lax
from jax.experimental import pallas as pl
from jax.experimental.pallas import tpu as pltpu

F32 = jnp.float32
BF16 = jnp.bfloat16

D_MODEL = 1024
DEPTH = 4
PAGE = 128
NORM_EPS = 1e-6
NEG = -1e30
PICKED = -3e38
N_MOD = 9
D_FF = 2816

H_A = 4
DK_A = 128
HGRN_CHUNK = 64
HGRN_SUB = 16

H_B = 8
Q_LORA = 256
KV_LORA = 128
NOPE = 64
ROPE_D = 32
VD_B = 64
ROPE_BASE = 10000.0
MLA_ROW = KV_LORA + ROPE_D
MLA_SCALE = (NOPE + ROPE_D) ** -0.5

DH = 64
H_C = 8
G_C = 2
HPG_C = 4
NSA_BLOCK = 64
TOP_N = 8
WINDOW = 512
FORCE_BONUS = 100.0
NSA_ROW = 512
WIN_ROW = 256
H_D = 8
KV_D = 4
HPK_D = 2
SB_ROW = 512
ATT_SCALE = DH ** -0.5
NUM_BUCKETS = 32
MAX_DISTANCE = 128

V7X_VMEM_BYTES = 64 * 1024 * 1024
VMEM_LIMIT = V7X_VMEM_BYTES - 8 * 1024 * 1024

TM_DENSE = 512
FF_CHUNK = 1408
TQ = 256
SAMPLE_ROWS = 8
PAGES_PER_STEP = 16


def _cparams(*sem):
    return pltpu.CompilerParams(dimension_semantics=sem, vmem_limit_bytes=VMEM_LIMIT)


def _iota(shape, dim):
    return lax.broadcasted_iota(jnp.int32, shape, dim)


def _rms(x):
    return x * lax.rsqrt(jnp.mean(x * x, axis=-1, keepdims=True) + NORM_EPS)


def _silu(x):
    return x * jax.nn.sigmoid(x)


def _log_sigmoid(x):
    return jnp.minimum(x, 0.0) - jnp.log1p(jnp.exp(-jnp.abs(x)))


def _dot(a, b):
    return jnp.dot(a, b, preferred_element_type=F32)


def _dot_nt(a, b):
    return lax.dot_general(a, b, (((1,), (1,)), ((), ())), preferred_element_type=F32)


def _dot_tn(a, b):
    return lax.dot_general(a, b, (((0,), (0,)), ((), ())), preferred_element_type=F32)


def _split3(x):
    hi = x.astype(BF16)
    r = x - hi.astype(F32)
    mid = r.astype(BF16)
    lo = (r - mid.astype(F32)).astype(BF16)
    return hi, mid, lo


def _ada_body(c_ref, w_ref, b_ref, o_ref):
    h = _silu(c_ref[...]).astype(BF16)
    o_ref[0] = _dot(h, w_ref[0].astype(BF16)) + b_ref[0]


def _ada_mod(c_all, w_ada, b_ada):
    nc = c_all.shape[0]
    depth, _, ncol = w_ada.shape
    tn = 1024
    return pl.pallas_call(
        _ada_body,
        grid=(depth, ncol // tn),
        in_specs=[pl.BlockSpec((nc, D_MODEL), lambda l, j: (0, 0)),
                  pl.BlockSpec((1, D_MODEL, tn), lambda l, j: (l, 0, j)),
                  pl.BlockSpec((1, 1, tn), lambda l, j: (l, 0, j))],
        out_specs=pl.BlockSpec((1, nc, tn), lambda l, j: (l, 0, j)),
        out_shape=jax.ShapeDtypeStruct((depth, nc, ncol), F32),
        compiler_params=_cparams("arbitrary", "arbitrary"),
    )(c_all, w_ada, b_ada.reshape(depth, 1, ncol))


def _tile_rows(n):
    return min(TM_DENSE, n)


def _mod_spec(mod, tm, seq_len):
    s, _, r, _ = mod.shape
    if r == 1:
        tiles_per_seq = seq_len // tm
        return pl.BlockSpec((1, N_MOD, 1, D_MODEL), lambda i: (i // tiles_per_seq, 0, 0, 0))
    return pl.BlockSpec((1, N_MOD, tm, D_MODEL), lambda i: (0, 0, i, 0))


def _prenorm(x, mod_ref, g_ref, sub):
    shift = mod_ref[0, 3 * sub]
    scale = mod_ref[0, 3 * sub + 1]
    return _rms(x) * g_ref[sub:sub + 1, :] * (1.0 + scale) + shift


def _const_spec(shape):
    nd = len(shape)
    return pl.BlockSpec(shape, lambda i: (0,) * nd, pipeline_mode=pl.Buffered(1))


def _ffn_body(x_ref, mod_ref, g_ref, wg_ref, wu_ref, wd_ref, *rest, sub, final):
    o_ref = rest[-1]
    x = x_ref[...]
    h = _prenorm(x, mod_ref, g_ref, sub).astype(BF16)
    acc = None
    for c0 in range(0, D_FF, FF_CHUNK):
        a = _dot(h, wg_ref[:, c0:c0 + FF_CHUNK])
        u = _dot(h, wu_ref[:, c0:c0 + FF_CHUNK])
        t = (_silu(a) * u).astype(BF16)
        part = _dot(t, wd_ref[c0:c0 + FF_CHUNK, :])
        acc = part if acc is None else acc + part
    y = x + 0.5 * mod_ref[0, 3 * sub + 2] * acc
    if final:
        y = _rms(y) * rest[0][...]
    o_ref[...] = y


def _ffn(x, mod, norm_g, wg, wu, wd, sub, seq_len, final_g=None):
    n = x.shape[0]
    tm = _tile_rows(n)
    final = final_g is not None
    in_specs = [pl.BlockSpec((tm, D_MODEL), lambda i: (i, 0)),
                _mod_spec(mod, tm, seq_len),
                _const_spec((3, D_MODEL)),
                _const_spec((D_MODEL, D_FF)), _const_spec((D_MODEL, D_FF)), _const_spec((D_FF, D_MODEL))]
    args = [x, mod, norm_g, wg, wu, wd]
    if final:
        in_specs.append(_const_spec((1, D_MODEL)))
        args.append(final_g.reshape(1, D_MODEL))
    return pl.pallas_call(
        functools.partial(_ffn_body, sub=sub, final=final),
        grid=(n // tm,),
        in_specs=in_specs,
        out_specs=pl.BlockSpec((tm, D_MODEL), lambda i: (i, 0)),
        out_shape=jax.ShapeDtypeStruct((n, D_MODEL), F32),
        compiler_params=_cparams("arbitrary"),
    )(*args)


EVEN_COLS = 2688
ODD_COLS = 2432


def _inproj_even_body(x_ref, mod_ref, g_ref, w_ref, cos_ref, sin_ref, gq_ref, gkv_ref, wq_ref, wkb_ref,
                      zh_ref, kfull_ref, rows_ref, qlat_ref, qrope_ref):
    h = _prenorm(x_ref[...], mod_ref, g_ref, 1).astype(BF16)
    zh_ref[...] = _dot(h, w_ref[:, 0:2048])
    z = _dot(h, w_ref[:, 2048:EVEN_COLS])
    cos = cos_ref[...]
    sin = sin_ref[...]
    ckv = _rms(z[:, 256:384]) * gkv_ref[...]
    krope = z[:, 384:512] * cos + z[:, 512:640] * sin
    kfull_ref[:, 0:128] = ckv
    kfull_ref[:, 128:256] = krope
    rows_ref[:, 0:128] = ckv
    rows_ref[:, 128:MLA_ROW] = krope[:, 0:ROPE_D]
    qa = (_rms(z[:, 0:256]) * gq_ref[...]).astype(BF16)
    qz = _dot(qa, wq_ref[...])
    qlat_ref[...] = _dot(qz[:, 0:512].astype(BF16), wkb_ref[...]) * MLA_SCALE
    cos2 = jnp.concatenate([cos, cos], axis=1)
    sin2 = jnp.concatenate([sin, sin], axis=1)
    qrope_ref[...] = (qz[:, 512:768] * cos2 + qz[:, 768:1024] * sin2) * MLA_SCALE


def _inproj_even(x, mod, norm_g, w, cos, sin, gq, gkv, wq, wkb, seq_len):
    n = x.shape[0]
    tm = _tile_rows(n)
    tab_tiles = cos.shape[0] // tm
    tok = lambda c: pl.BlockSpec((tm, c), lambda i: (i, 0))
    tab = pl.BlockSpec((tm, 128), lambda i: (i % tab_tiles, 0))
    return pl.pallas_call(
        _inproj_even_body,
        grid=(n // tm,),
        in_specs=[tok(D_MODEL), _mod_spec(mod, tm, seq_len), _const_spec((3, D_MODEL)),
                  _const_spec((D_MODEL, EVEN_COLS)), tab, tab,
                  _const_spec((1, Q_LORA)), _const_spec((1, KV_LORA)),
                  _const_spec((Q_LORA, 1024)), _const_spec((512, 1024))],
        out_specs=[tok(2048), tok(256), tok(MLA_ROW), tok(1024), tok(256)],
        out_shape=[jax.ShapeDtypeStruct((n, c), F32) for c in (2048, 256, MLA_ROW, 1024, 256)],
        compiler_params=_cparams("arbitrary"),
    )(x, mod, norm_g, w, cos, sin, gq, gkv, wq, wkb)


def _inproj_odd_body(x_ref, mod_ref, g_ref, w_ref, qn_ref, nsa_ref, win_ref, qs_ref, sb_ref, zg_ref):
    h = _prenorm(x_ref[...], mod_ref, g_ref, 1).astype(BF16)
    qn_ref[...] = _dot(h, w_ref[:, 0:512])
    nsa_ref[...] = _dot(h, w_ref[:, 512:1024])
    win_ref[...] = _dot(h, w_ref[:, 1024:1280])
    qs_ref[...] = _dot(h, w_ref[:, 1280:1792])
    sb_ref[...] = _dot(h, w_ref[:, 1792:2304])
    zg_ref[...] = _dot(h, w_ref[:, 2304:ODD_COLS])


def _inproj_odd(x, mod, norm_g, w, seq_len):
    n = x.shape[0]
    tm = _tile_rows(n)
    tok = lambda c: pl.BlockSpec((tm, c), lambda i: (i, 0))
    cols = (512, NSA_ROW, WIN_ROW, 512, SB_ROW, 128)
    return pl.pallas_call(
        _inproj_odd_body,
        grid=(n // tm,),
        in_specs=[tok(D_MODEL), _mod_spec(mod, tm, seq_len), _const_spec((3, D_MODEL)),
                  _const_spec((D_MODEL, ODD_COLS))],
        out_specs=[tok(c) for c in cols],
        out_shape=[jax.ShapeDtypeStruct((n, c), F32) for c in cols],
        compiler_params=_cparams("arbitrary"),
    )(x, mod, norm_g, w)


def _outproj_even_body(x_ref, mod_ref, oa_ref, ob_ref, w_ref, o_ref):
    mix = (_dot(oa_ref[...].astype(BF16), w_ref[0:512, :])
           + _dot(ob_ref[...].astype(BF16), w_ref[512:1024, :]))
    o_ref[...] = x_ref[...] + mod_ref[0, 5] * mix


def _outproj_odd_body(x_ref, mod_ref, ocmp_ref, oslc_ref, owin_ref, zg_ref, osb_ref, e_ref, w_ref, o_ref):
    hi, mid, lo = _split3(jax.nn.sigmoid(zg_ref[...]))
    e = e_ref[...]
    gexp = _dot(hi, e) + _dot(mid, e) + _dot(lo, e)
    nsa = (gexp[:, 0:512] * ocmp_ref[...] + gexp[:, 512:1024] * oslc_ref[...]
           + gexp[:, 1024:1536] * owin_ref[...])
    mix = (_dot(nsa.astype(BF16), w_ref[0:512, :])
           + _dot(osb_ref[...].astype(BF16), w_ref[512:1024, :]))
    o_ref[...] = x_ref[...] + mod_ref[0, 5] * mix


def _outproj(x, mod, parts, consts, w, seq_len, odd):
    n = x.shape[0]
    tm = _tile_rows(n)
    tok = lambda c: pl.BlockSpec((tm, c), lambda i: (i, 0))
    in_specs = ([tok(D_MODEL), _mod_spec(mod, tm, seq_len)] + [tok(p.shape[1]) for p in parts]
                + [_const_spec(c.shape) for c in consts] + [_const_spec((1024, D_MODEL))])
    return pl.pallas_call(
        _outproj_odd_body if odd else _outproj_even_body,
        grid=(n // tm,),
        in_specs=in_specs,
        out_specs=tok(D_MODEL),
        out_shape=jax.ShapeDtypeStruct((n, D_MODEL), F32),
        compiler_params=_cparams("arbitrary"),
    )(x, mod, *parts, *consts, w)


def _softmax_init(m_ref, l_ref, acc_ref):
    m_ref[...] = jnp.full(m_ref.shape, NEG, F32)
    l_ref[...] = jnp.zeros(l_ref.shape, F32)
    acc_ref[...] = jnp.zeros(acc_ref.shape, F32)


def _softmax_step(s, mask, v, m_ref, l_ref, acc_ref):
    if mask is not None:
        s = jnp.where(mask, s, NEG)
    m_prev = m_ref[...]
    m_new = jnp.maximum(m_prev, jnp.max(s, axis=1, keepdims=True))
    p = jnp.exp(s - m_new)
    if mask is not None:
        p = jnp.where(mask, p, 0.0)
    alpha = jnp.exp(m_prev - m_new)
    l_ref[...] = alpha * l_ref[...] + jnp.sum(p, axis=1, keepdims=True)
    acc_ref[...] = alpha * acc_ref[...] + _dot(p.astype(BF16), v)
    m_ref[...] = m_new


def _softmax_out(l_ref, acc_ref):
    l = l_ref[...]
    return jnp.where(l > 0.0, acc_ref[...] / jnp.where(l > 0.0, l, 1.0), 0.0)


def _mla_stack_q(qlat, qrope):
    lane = _iota((1, 128), 1)
    parts = []
    for h in range(H_B):
        ql = qlat[:, 128 * h:128 * (h + 1)]
        qr = qrope[:, 128 * (h // 4):128 * (h // 4 + 1)]
        qr = jnp.where((lane // ROPE_D) == (h % 4), qr, 0.0)
        parts.append(jnp.concatenate([ql, qr], axis=1))
    return jnp.concatenate(parts, axis=0).astype(BF16)


def _mla_prompt_body(qlat_ref, qrope_ref, kfull_ref, wvb_ref, o_ref, kbf_ref, m_ref, l_ref, acc_ref, *, tq):
    qi = pl.program_id(1)

    @pl.when(qi == 0)
    def _():
        kbf_ref[...] = kfull_ref[0].astype(BF16)

    qs = _mla_stack_q(qlat_ref[...], qrope_ref[...])
    _softmax_init(m_ref, l_ref, acc_ref)

    def step(kb, diag):
        k = kbf_ref[pl.ds(pl.multiple_of(kb * tq, tq), tq), :]
        s = _dot_nt(qs, k)
        mask = None
        if diag:
            shape = (H_B * tq, tq)
            mask = _iota(shape, 1) <= (_iota(shape, 0) & (tq - 1))
        _softmax_step(s, mask, k[:, 0:KV_LORA], m_ref, l_ref, acc_ref)

    def far(kb, c):
        step(kb, False)
        return c

    lax.fori_loop(0, qi, far, 0)
    step(qi, True)
    o = _softmax_out(l_ref, acc_ref)
    olat = jnp.concatenate([o[h * tq:(h + 1) * tq] for h in range(H_B)], axis=1)
    o_ref[...] = _dot(olat.astype(BF16), wvb_ref[...])


def _mla_prompt(qlat, qrope, kfull, wvb, batch, seq):
    tq = min(TQ, seq)
    assert tq & (tq - 1) == 0 and seq % tq == 0
    nq = seq // tq
    tok = lambda c: pl.BlockSpec((tq, c), lambda b, i: (b * nq + i, 0))
    return pl.pallas_call(
        functools.partial(_mla_prompt_body, tq=tq),
        grid=(batch, nq),
        in_specs=[tok(1024), tok(256), pl.BlockSpec((1, seq, 256), lambda b, i: (b, 0, 0)),
                  pl.BlockSpec((1024, 512), lambda b, i: (0, 0))],
        out_specs=tok(512),
        out_shape=jax.ShapeDtypeStruct((batch * seq, 512), F32),
        scratch_shapes=[pltpu.VMEM((seq, 256), BF16), pltpu.VMEM((H_B * tq, 1), F32),
                        pltpu.VMEM((H_B * tq, 1), F32), pltpu.VMEM((H_B * tq, KV_LORA), F32)],
        compiler_params=_cparams("arbitrary", "arbitrary"),
    )(qlat, qrope, kfull.reshape(batch, seq, 256), wvb)


def _hgrn_chunk(zq, zf, zi, zg, par, st, chunk, sub, valid):
    log_lb, log1m_lb, lb_pos, one_m_lb, gn = par[0:1], par[1:2], par[2:3], par[3:4], par[4:5]
    q = _silu(zq)
    ls = _log_sigmoid(zf)
    b = log1m_lb + ls
    lae = jnp.maximum(log_lb, b) + jnp.log1p(jnp.exp(-jnp.abs(log_lb - b)))
    logf = jnp.where(lb_pos > 0.5, lae, ls)
    kin = one_m_lb * jax.nn.sigmoid(-zf)
    v = zi
    row = _iota((chunk, 128), 0)
    g = logf
    sh = 1
    while sh < chunk:
        g = g + jnp.where(row >= sh, pltpu.roll(g, sh, axis=0), 0.0)
        sh *= 2
    o = _dot_nt((q * jnp.exp(g)).astype(BF16), st.astype(BF16))
    nsub = chunk // sub
    v_bf = v.astype(BF16)
    if nsub > 1:
        ends = [g[sub * j + sub - 1:sub * j + sub] for j in range(nsub)]
        esub = jnp.concatenate([jnp.broadcast_to(e, (sub, 128)) for e in ends], axis=0)
        kt = (kin * jnp.exp(esub - g)).astype(BF16)
        col_a = _iota((chunk, chunk), 1)
        row_a = _iota((chunk, chunk), 0)
        a = jnp.zeros((chunk, chunk), F32)
        for j in range(nsub - 1):
            qj = (q * jnp.exp(jnp.minimum(g - ends[j], 0.0))).astype(BF16)
            aj = _dot_nt(qj, kt)
            a = jnp.where((col_a >= sub * j) & (col_a < sub * (j + 1)) & (row_a >= sub * (j + 1)), aj, a)
        o = o + _dot(a.astype(BF16), v_bf)
    row_s = _iota((sub, 128), 0)
    diag = []
    for i in range(nsub):
        g_i = g[sub * i:sub * (i + 1)]
        q_i = q[sub * i:sub * (i + 1)]
        o_i = jnp.zeros((sub, 128), F32)
        for s in range(min(sub, valid - sub * i)):
            r = sub * i + s
            e = jnp.exp(jnp.minimum(g_i - g[r:r + 1], 0.0))
            x = jnp.where(row_s >= s, q_i * (kin[r:r + 1] * e), 0.0)
            o_i = o_i + jnp.sum(x, axis=1, keepdims=True) * v[r:r + 1]
        diag.append(o_i)
    o = o + (diag[0] if nsub == 1 else jnp.concatenate(diag, axis=0))
    g_last = g[valid - 1:valid]
    khat = kin * jnp.exp(jnp.minimum(g_last - g, 0.0))
    if valid < chunk:
        khat = jnp.where(row < valid, khat, 0.0)
    st_new = st * jnp.exp(g_last) + _dot_tn(v_bf, khat.astype(BF16))
    return _rms(o) * gn * _silu(zg), st_new


def _hgrn_body(zq_ref, zf_ref, zi_ref, zg_ref, par_ref, *rest, chunk, sub, valid, n_chunks, has_s0):
    if has_s0:
        s0_ref, o_ref, sout_ref, st_ref = rest
    else:
        o_ref, sout_ref, st_ref = rest
    tb = pl.program_id(1)

    @pl.when(tb == 0)
    def _():
        for h in range(H_A):
            st_ref[h] = s0_ref[0, h].T if has_s0 else jnp.zeros((128, 128), F32)

    def chunk_body(c, carry):
        r0 = pl.multiple_of(c * chunk, chunk)
        for h in range(H_A):
            cs = slice(128 * h, 128 * (h + 1))
            rs = pl.ds(r0, chunk)
            o, st_new = _hgrn_chunk(zq_ref[rs, cs], zf_ref[rs, cs], zi_ref[rs, cs], zg_ref[rs, cs],
                                    par_ref[:, cs], st_ref[h], chunk, sub, valid)
            o_ref[rs, cs] = o
            st_ref[h] = st_new
        return carry

    lax.fori_loop(0, n_chunks, chunk_body, 0)

    @pl.when(tb == pl.num_programs(1) - 1)
    def _():
        for h in range(H_A):
            sout_ref[0, h] = st_ref[h].T


def _hgrn(zh, par, batch, rows_per_seq, tb, chunk, sub, valid, s0=None):
    nt = rows_per_seq // tb
    col = lambda j: pl.BlockSpec((tb, 512), lambda b, t: (b * nt + t, j))
    in_specs = [col(0), col(1), col(2), col(3), pl.BlockSpec((8, 512), lambda b, t: (0, 0))]
    args = [zh, zh, zh, zh, par]
    st_spec = pl.BlockSpec((1, H_A, 128, 128), lambda b, t: (b, 0, 0, 0))
    if s0 is not None:
        in_specs.append(st_spec)
        args.append(s0)
    return pl.pallas_call(
        functools.partial(_hgrn_body, chunk=chunk, sub=sub, valid=valid, n_chunks=tb // chunk,
                          has_s0=s0 is not None),
        grid=(batch, nt),
        in_specs=in_specs,
        out_specs=[pl.BlockSpec((tb, 512), lambda b, t: (b * nt + t, 0)), st_spec],
        out_shape=[jax.ShapeDtypeStruct((batch * rows_per_seq, 512), F32),
                   jax.ShapeDtypeStruct((batch, H_A, 128, 128), F32)],
        scratch_shapes=[pltpu.VMEM((H_A, 128, 128), F32)],
        compiler_params=_cparams("arbitrary", "arbitrary"),
    )(*args)


def _block_means(x):
    nb = x.shape[0] // NSA_BLOCK
    return jnp.sum(x.reshape(nb, NSA_BLOCK, x.shape[1]), axis=1) * (1.0 / NSA_BLOCK)


def _means_prompt_body(x_ref, o_ref):
    o_ref[...] = _block_means(x_ref[...])


def _means_prompt(nsa_rows):
    n = nsa_rows.shape[0]
    tm = _tile_rows(n)
    return pl.pallas_call(
        _means_prompt_body,
        grid=(n // tm,),
        in_specs=[pl.BlockSpec((tm, 256), lambda i: (i, 0))],
        out_specs=pl.BlockSpec((tm // NSA_BLOCK, 256), lambda i: (i, 0)),
        out_shape=jax.ShapeDtypeStruct((n // NSA_BLOCK, 256), F32),
        compiler_params=_cparams("arbitrary"),
    )(nsa_rows)


def _cmp_head(qh, ck, cv, bias, vis):
    s = _dot_nt(qh.astype(BF16), ck) * ATT_SCALE + bias
    if vis is not None:
        s = jnp.where(vis, s, NEG)
    e = jnp.exp(s - jnp.max(s, axis=1, keepdims=True))
    p = e / jnp.sum(e, axis=1, keepdims=True)
    if vis is not None:
        p = jnp.where(vis, p, 0.0)
    return _dot(p.astype(BF16), cv), p


def _top_blocks(score, blk, n_sel):
    nb = score.shape[1]
    for _ in range(n_sel):
        m = jnp.max(score, axis=1, keepdims=True)
        idx = jnp.min(jnp.where(score == m, blk, nb), axis=1, keepdims=True)
        yield idx, m > 0.5 * NEG
        score = jnp.where(blk == idx, PICKED, score)


def _cmpsel_prompt_body(q_ref, cmp_ref, cb_ref, ocmp_ref, sel_ref, *, tq, nbp, n_sel):
    qi = pl.program_id(1)
    qpos = qi * tq + _iota((tq, 1), 0)
    blk = _iota((1, nbp), 1)
    vis = (blk * NSA_BLOCK + NSA_BLOCK - 1) <= qpos
    cur = jnp.right_shift(qpos, 6)
    allowed = blk <= cur
    forced = (blk == 0) | (blk == cur) | (blk == cur - 1)
    for g in range(G_C):
        ck = cmp_ref[0, :, DH * g:DH * (g + 1)].astype(BF16)
        cv = cmp_ref[0, :, 128 + DH * g:128 + DH * (g + 1)].astype(BF16)
        imp = jnp.zeros((tq, nbp), F32)
        for j in range(HPG_C):
            h = HPG_C * g + j
            o, p = _cmp_head(q_ref[:, DH * h:DH * (h + 1)], ck, cv, cb_ref[h], vis)
            ocmp_ref[:, DH * h:DH * (h + 1)] = o
            imp = imp + p
        score = jnp.where(allowed, imp + jnp.where(forced, FORCE_BONUS, 0.0), NEG)
        sel = jnp.zeros((tq, nbp), F32)
        for idx, ok in _top_blocks(score, blk, n_sel):
            sel = jnp.where((blk == idx) & ok, 1.0, sel)
        sel_ref[0, g] = sel


def _cmpsel_prompt(q, cmp, cb, batch, seq):
    tq = min(TQ, seq)
    nq = seq // tq
    nbp = cmp.shape[1]
    n_sel = min(TOP_N, -(-seq // NSA_BLOCK))
    return pl.pallas_call(
        functools.partial(_cmpsel_prompt_body, tq=tq, nbp=nbp, n_sel=n_sel),
        grid=(batch, nq),
        in_specs=[pl.BlockSpec((tq, 512), lambda b, i: (b * nq + i, 0)),
                  pl.BlockSpec((1, nbp, 256), lambda b, i: (b, 0, 0)),
                  pl.BlockSpec((H_C, tq, nbp), lambda b, i: (0, i, 0))],
        out_specs=[pl.BlockSpec((tq, 512), lambda b, i: (b * nq + i, 0)),
                   pl.BlockSpec((1, G_C, tq, nbp), lambda b, i: (b, 0, i, 0))],
        out_shape=[jax.ShapeDtypeStruct((batch * seq, 512), F32),
                   jax.ShapeDtypeStruct((batch, G_C, seq, nbp), F32)],
        compiler_params=_cparams("arbitrary", "arbitrary"),
    )(q, cmp, cb)


def _stack_heads(q_ref, h0, nh, scale):
    return jnp.concatenate([q_ref[:, DH * h:DH * (h + 1)] for h in range(h0, h0 + nh)], axis=0) * scale


def _tile_bias(tab_ref, cfar_ref, h0, nh, tq, mode):
    if mode == "far":
        return jnp.concatenate([jnp.broadcast_to(cfar_ref[h:h + 1, 0:1], (tq, 1)) for h in range(h0, h0 + nh)],
                               axis=0)
    return jnp.concatenate([tab_ref[h] for h in range(h0, h0 + nh)], axis=0)


def _slc_prompt_body(q_ref, rows_ref, sel_ref, bd_ref, bs_ref, cfar_ref, o_ref,
                     kv_ref, m_ref, l_ref, acc_ref, *, tq, nbp):
    qi = pl.program_id(1)

    @pl.when(qi == 0)
    def _():
        kv_ref[...] = rows_ref[0, :, 256:512].astype(BF16)

    per_tile = tq // NSA_BLOCK
    for g in range(G_C):
        q4 = _stack_heads(q_ref, HPG_C * g, HPG_C, ATT_SCALE).astype(BF16)
        sel = sel_ref[0, g].astype(BF16)
        _softmax_init(m_ref, l_ref, acc_ref)

        def step(kb, mode, g=g, q4=q4, sel=sel):
            rs = pl.ds(pl.multiple_of(kb * tq, tq), tq)
            k = kv_ref[rs, DH * g:DH * (g + 1)]
            v = kv_ref[rs, 128 + DH * g:128 + DH * (g + 1)]
            expand = (_iota((nbp, tq), 0) == kb * per_tile + jnp.right_shift(_iota((nbp, tq), 1), 6))
            mask = _dot(sel, jnp.where(expand, 1.0, 0.0).astype(BF16)) > 0.5
            if mode == "diag":
                mask = mask & (_iota((tq, tq), 1) <= _iota((tq, tq), 0))
            mask4 = jnp.concatenate([mask] * HPG_C, axis=0)
            tab = bd_ref if mode == "diag" else bs_ref
            s = _dot_nt(q4, k) + _tile_bias(tab, cfar_ref, HPG_C * g, HPG_C, tq, mode)
            _softmax_step(s, mask4, v, m_ref, l_ref, acc_ref)

        def far(kb, c, step=step):
            step(kb, "far")
            return c

        lax.fori_loop(0, jnp.maximum(qi - 1, 0), far, 0)

        @pl.when(qi >= 1)
        def _(step=step):
            step(qi - 1, "sub")

        step(qi, "diag")
        o = _softmax_out(l_ref, acc_ref)
        for j in range(HPG_C):
            h = HPG_C * g + j
            o_ref[:, DH * h:DH * (h + 1)] = o[j * tq:(j + 1) * tq]


def _win_prompt_body(q_ref, rows_ref, bd_ref, bs_ref, cfar_ref, o_ref, kv_ref, m_ref, l_ref, acc_ref, *, tq):
    qi = pl.program_id(1)

    @pl.when(qi == 0)
    def _():
        kv_ref[...] = rows_ref[0].astype(BF16)

    n_back = WINDOW // tq
    for g in range(G_C):
        q4 = _stack_heads(q_ref, HPG_C * g, HPG_C, ATT_SCALE).astype(BF16)
        _softmax_init(m_ref, l_ref, acc_ref)

        def step(d, g=g, q4=q4):
            rs = pl.ds(pl.multiple_of((qi - d) * tq, tq), tq)
            k = kv_ref[rs, DH * g:DH * (g + 1)]
            v = kv_ref[rs, 128 + DH * g:128 + DH * (g + 1)]
            col, row = _iota((HPG_C * tq, tq), 1), _iota((HPG_C * tq, tq), 0) & (tq - 1)
            mask = None
            if d == 0:
                mask = col <= row
            elif d == n_back:
                mask = col >= row
            mode = "diag" if d == 0 else ("sub" if d == 1 else "far")
            tab = bd_ref if d == 0 else bs_ref
            s = _dot_nt(q4, k) + _tile_bias(tab, cfar_ref, HPG_C * g, HPG_C, tq, mode)
            _softmax_step(s, mask, v, m_ref, l_ref, acc_ref)

        for d in range(n_back, 0, -1):
            @pl.when(qi >= d)
            def _(d=d, step=step):
                step(d)

        step(0)
        o = _softmax_out(l_ref, acc_ref)
        for j in range(HPG_C):
            h = HPG_C * g + j
            o_ref[:, DH * h:DH * (h + 1)] = o[j * tq:(j + 1) * tq]


def _nsa_prompt_attn(q, rows, sel, bd, bs, cfar, batch, seq):
    tq = min(TQ, seq)
    assert tq & (tq - 1) == 0 and seq % tq == 0 and WINDOW % tq == 0 and tq >= MAX_DISTANCE
    nq = seq // tq
    width = rows.shape[1]
    tok = pl.BlockSpec((tq, 512), lambda b, i: (b * nq + i, 0))
    in_specs = [tok, pl.BlockSpec((1, seq, width), lambda b, i: (b, 0, 0))]
    args = [q, rows.reshape(batch, seq, width)]
    if sel is not None:
        nbp = sel.shape[-1]
        in_specs.append(pl.BlockSpec((1, G_C, tq, nbp), lambda b, i: (b, 0, i, 0)))
        args.append(sel)
        body = functools.partial(_slc_prompt_body, tq=tq, nbp=nbp)
    else:
        body = functools.partial(_win_prompt_body, tq=tq)
    full = lambda a: pl.BlockSpec(a.shape, lambda b, i: (0,) * a.ndim)
    return pl.pallas_call(
        body,
        grid=(batch, nq),
        in_specs=in_specs + [full(bd), full(bs), full(cfar)],
        out_specs=tok,
        out_shape=jax.ShapeDtypeStruct((batch * seq, 512), F32),
        scratch_shapes=[pltpu.VMEM((seq, 256), BF16), pltpu.VMEM((HPG_C * tq, 1), F32),
                        pltpu.VMEM((HPG_C * tq, 1), F32), pltpu.VMEM((HPG_C * tq, DH), F32)],
        compiler_params=_cparams("arbitrary", "arbitrary"),
    )(*args, bd, bs, cfar)


def _sb_prompt_body(q_ref, rows_ref, u_ref, o_ref, kv_ref, carry_ref, acc_ref, *, tq):
    qi = pl.program_id(1)

    @pl.when(qi == 0)
    def _():
        kv_ref[...] = rows_ref[0].astype(BF16)

    shape = (HPK_D * tq, tq)
    for g in range(KV_D):
        q2 = _stack_heads(q_ref, HPK_D * g, HPK_D, ATT_SCALE).astype(BF16)
        carry_ref[...] = jnp.zeros(carry_ref.shape, F32)
        acc_ref[...] = jnp.zeros(acc_ref.shape, F32)

        def step(kb, diag, g=g, q2=q2):
            rs = pl.ds(pl.multiple_of(kb * tq, tq), tq)
            k = kv_ref[rs, DH * g:DH * (g + 1)]
            v = kv_ref[rs, 256 + DH * g:256 + DH * (g + 1)]
            z = _dot_nt(q2, k)
            lsz = _log_sigmoid(z)
            lf = lsz - z
            if diag:
                mask = _iota(shape, 1) < (_iota(shape, 0) & (tq - 1))
                lf = jnp.where(mask, lf, 0.0)
            u = u_ref[...]
            hi, mid, lo = _split3(lf)
            between = _dot(hi, u) + _dot(mid, u) + _dot(lo, u)
            a = jnp.exp(lsz + between + carry_ref[...])
            if diag:
                a = jnp.where(mask, a, 0.0)
            acc_ref[...] += _dot(a.astype(BF16), v)
            carry_ref[...] += jnp.sum(lf, axis=1, keepdims=True)

        step(qi, True)

        def back(it, c, step=step):
            step(qi - 1 - it, False)
            return c

        lax.fori_loop(0, qi, back, 0)
        o = acc_ref[...]
        for j in range(HPK_D):
            h = HPK_D * g + j
            o_ref[:, DH * h:DH * (h + 1)] = o[j * tq:(j + 1) * tq]


def _sb_prompt(q, rows, batch, seq):
    tq = min(TQ, seq)
    assert tq & (tq - 1) == 0 and seq % tq == 0
    nq = seq // tq
    later = np.arange(tq)[:, None] > np.arange(tq)[None, :]
    u = jnp.asarray(later, BF16)
    tok = pl.BlockSpec((tq, 512), lambda b, i: (b * nq + i, 0))
    return pl.pallas_call(
        functools.partial(_sb_prompt_body, tq=tq),
        grid=(batch, nq),
        in_specs=[tok, pl.BlockSpec((1, seq, SB_ROW), lambda b, i: (b, 0, 0)),
                  pl.BlockSpec((tq, tq), lambda b, i: (0, 0))],
        out_specs=tok,
        out_shape=jax.ShapeDtypeStruct((batch * seq, 512), F32),
        scratch_shapes=[pltpu.VMEM((seq, SB_ROW), BF16), pltpu.VMEM((HPK_D * tq, 1), F32),
                        pltpu.VMEM((HPK_D * tq, DH), F32)],
        compiler_params=_cparams("arbitrary", "arbitrary"),
    )(q, rows.reshape(batch, seq, SB_ROW), u)


def _page_specs(block, pp, col_block, page_of):
    return [pl.BlockSpec(block, lambda b, s, pt, j=j: (page_of(b, s, j, pt), 0, col_block)) for j in range(pp)]


def _mla_decode_body(pt_ref, q_ref, knew_ref, wvb_ref, *rest, pp, valid):
    pages = rest[:pp]
    o_ref, m_ref, l_ref, acc_ref = rest[pp:]
    st = pl.program_id(1)
    r = SAMPLE_ROWS
    q = q_ref[0].astype(BF16)

    @pl.when(st == 0)
    def _():
        _softmax_init(m_ref, l_ref, acc_ref)

    keys = [pages[j][0].astype(BF16) for j in range(pp)]
    s = jnp.concatenate([_dot_nt(q, k) for k in keys], axis=1)
    v = jnp.concatenate([k[:, 0:KV_LORA] for k in keys], axis=0)
    _softmax_step(s, None, v, m_ref, l_ref, acc_ref)

    @pl.when(st == pl.num_programs(1) - 1)
    def _():
        kn = knew_ref[0].astype(BF16)
        shape = (H_B * r, r)
        col = _iota(shape, 1)
        mask = (col <= (_iota(shape, 0) & (r - 1))) & (col < valid)
        _softmax_step(_dot_nt(q, kn), mask, kn[:, 0:KV_LORA], m_ref, l_ref, acc_ref)
        o = _softmax_out(l_ref, acc_ref)
        olat = jnp.concatenate([o[r * h:r * (h + 1)] for h in range(H_B)], axis=1)
        o_ref[0] = _dot(olat.astype(BF16), wvb_ref[...])


def _mla_decode(page_table, q, knew, wvb, cache, valid):
    batch, n_pages = page_table.shape
    pp = min(PAGES_PER_STEP, n_pages)
    assert n_pages % pp == 0
    r = SAMPLE_ROWS
    grid_spec = pltpu.PrefetchScalarGridSpec(
        num_scalar_prefetch=1,
        grid=(batch, n_pages // pp),
        in_specs=[pl.BlockSpec((1, H_B * r, MLA_ROW), lambda b, s, pt: (b, 0, 0)),
                  pl.BlockSpec((1, r, MLA_ROW), lambda b, s, pt: (b, 0, 0)),
                  pl.BlockSpec((1024, 512), lambda b, s, pt: (0, 0))]
        + _page_specs((1, PAGE, MLA_ROW), pp, 0, lambda b, s, j, pt: pt[b, s * pp + j]),
        out_specs=pl.BlockSpec((1, r, 512), lambda b, s, pt: (b, 0, 0)),
        scratch_shapes=[pltpu.VMEM((H_B * r, 1), F32), pltpu.VMEM((H_B * r, 1), F32),
                        pltpu.VMEM((H_B * r, KV_LORA), F32)])
    return pl.pallas_call(
        functools.partial(_mla_decode_body, pp=pp, valid=valid),
        grid_spec=grid_spec,
        out_shape=jax.ShapeDtypeStruct((batch, r, 512), F32),
        compiler_params=_cparams("arbitrary", "arbitrary"),
    )(page_table, q, knew, wvb, *([cache] * pp))


def _means_decode_body(pt_ref, *rest, pp):
    pages = rest[:pp]
    o_ref = rest[pp]
    o_ref[0] = _block_means(jnp.concatenate([pages[j][0] for j in range(pp)], axis=0))


def _means_decode(page_table, cache):
    batch, n_pages = page_table.shape
    pp = min(PAGES_PER_STEP, n_pages)
    per_page = PAGE // NSA_BLOCK
    grid_spec = pltpu.PrefetchScalarGridSpec(
        num_scalar_prefetch=1,
        grid=(batch, n_pages // pp),
        in_specs=_page_specs((1, PAGE, 256), pp, 0, lambda b, s, j, pt: pt[b, s * pp + j]),
        out_specs=pl.BlockSpec((1, pp * per_page, 256), lambda b, s, pt: (b, s, 0)))
    return pl.pallas_call(
        functools.partial(_means_decode_body, pp=pp),
        grid_spec=grid_spec,
        out_shape=jax.ShapeDtypeStruct((batch, n_pages * per_page, 256), F32),
        compiler_params=_cparams("arbitrary", "arbitrary"),
    )(page_table, *([cache] * pp))


def _cmpsel_decode_body(q_ref, cmp_ref, cb_ref, ocmp_ref, idx_ref, *, nb, n_past_sel):
    r = SAMPLE_ROWS
    blk = _iota((1, nb), 1)
    forced = (blk == 0) | (blk == nb - 1)
    lane = _iota((r, 128), 1)
    for g in range(G_C):
        ck = cmp_ref[0, :, DH * g:DH * (g + 1)].astype(BF16)
        cv = cmp_ref[0, :, 128 + DH * g:128 + DH * (g + 1)].astype(BF16)
        imp = jnp.zeros((r, nb), F32)
        for j in range(HPG_C):
            h = HPG_C * g + j
            o, p = _cmp_head(q_ref[0, :, DH * h:DH * (h + 1)], ck, cv, cb_ref[h], None)
            ocmp_ref[0, :, DH * h:DH * (h + 1)] = o
            imp = imp + p
        score = imp + jnp.where(forced, FORCE_BONUS, 0.0)
        picked = jnp.zeros((r, 128), jnp.int32)
        for slot, (idx, _) in enumerate(_top_blocks(score, blk, n_past_sel)):
            picked = jnp.where(lane == slot, idx, picked)
        idx_ref[0, g] = picked


def _cmpsel_decode(q, cmp, cb, n_past_sel):
    batch, nb, _ = cmp.shape
    r = SAMPLE_ROWS
    return pl.pallas_call(
        functools.partial(_cmpsel_decode_body, nb=nb, n_past_sel=n_past_sel),
        grid=(batch,),
        in_specs=[pl.BlockSpec((1, r, 512), lambda b: (b, 0, 0)),
                  pl.BlockSpec((1, nb, 256), lambda b: (b, 0, 0)),
                  pl.BlockSpec((H_C, r, nb), lambda b: (0, 0, 0))],
        out_specs=[pl.BlockSpec((1, r, 512), lambda b: (b, 0, 0)),
                   pl.BlockSpec((1, G_C, r, 128), lambda b: (b, 0, 0, 0))],
        out_shape=[jax.ShapeDtypeStruct((batch, r, 512), F32),
                   jax.ShapeDtypeStruct((batch, G_C, r, 128), jnp.int32)],
        compiler_params=_cparams("arbitrary"),
    )(q, cmp, cb)


def _dist_bias(tab3_ref, d0, width):
    nd = tab3_ref.shape[1]
    dist = jnp.clip(d0 - _iota((nd, width), 1), 0, nd - 1)
    onehot = jnp.where(_iota((nd, width), 0) == dist, 1.0, 0.0).astype(BF16)
    b = _dot(tab3_ref[...], onehot)
    return b[0:8] + b[8:16] + b[16:24]


def _slc_decode_body(idx_ref, pt_ref, q_ref, new_ref, tab3_ref, *rest, n_slots, p_len, valid):
    blocks = rest[:G_C * n_slots]
    o_ref = rest[G_C * n_slots]
    b, t = pl.program_id(0), pl.program_id(1)
    r = SAMPLE_ROWS
    q = (q_ref[0, 0] * ATT_SCALE).astype(BF16)
    qpos = p_len + t
    head_row = _iota((H_C, 1), 0)
    out = jnp.zeros((H_C, DH), F32)
    for g in range(G_C):
        pieces = []
        for s in range(n_slots):
            n = idx_ref[((b * G_C + g) * r + t) * n_slots + s]
            kv = blocks[g * n_slots + s][0]
            k = kv[:, DH * g:DH * (g + 1)].astype(BF16)
            v = kv[:, 128 + DH * g:128 + DH * (g + 1)].astype(BF16)
            sc = _dot_nt(q, k) + _dist_bias(tab3_ref, qpos - n * NSA_BLOCK, NSA_BLOCK)
            pieces.append((sc, v))
        kn = new_ref[0, :, 256 + DH * g:256 + DH * (g + 1)].astype(BF16)
        vn = new_ref[0, :, 384 + DH * g:384 + DH * (g + 1)].astype(BF16)
        col = _iota((H_C, r), 1)
        cur_ok = (col <= t) & (col < valid)
        sc_new = jnp.where(cur_ok, _dot_nt(q, kn) + _dist_bias(tab3_ref, t, r), NEG)
        m = jnp.max(sc_new, axis=1, keepdims=True)
        for sc, _ in pieces:
            m = jnp.maximum(m, jnp.max(sc, axis=1, keepdims=True))
        p_new = jnp.where(cur_ok, jnp.exp(sc_new - m), 0.0)
        l = jnp.sum(p_new, axis=1, keepdims=True)
        acc = _dot(p_new.astype(BF16), vn)
        for sc, v in pieces:
            p = jnp.exp(sc - m)
            l = l + jnp.sum(p, axis=1, keepdims=True)
            acc = acc + _dot(p.astype(BF16), v)
        in_group = (head_row >= HPG_C * g) & (head_row < HPG_C * (g + 1))
        out = jnp.where(in_group, acc / l, out)
    o_ref[0, 0] = out


def _slc_decode(idx, page_table, q, new_rows, tab3, cache, n_tok, p_len):
    batch, n_pages = page_table.shape
    r = SAMPLE_ROWS
    n_slots = idx.shape[0] // (batch * G_C * r)
    per_page = PAGE // NSA_BLOCK

    def slot_spec(g, s):
        def index(b, t, idx_ref, pt):
            n = idx_ref[((b * G_C + g) * r + t) * n_slots + s]
            return (pt[b, n // per_page], n % per_page, 1)
        return pl.BlockSpec((1, NSA_BLOCK, 256), index)

    grid_spec = pltpu.PrefetchScalarGridSpec(
        num_scalar_prefetch=2,
        grid=(batch, n_tok),
        in_specs=[pl.BlockSpec((1, 1, H_C, DH), lambda b, t, i, pt: (b, t, 0, 0)),
                  pl.BlockSpec((1, r, NSA_ROW), lambda b, t, i, pt: (b, 0, 0)),
                  pl.BlockSpec(tab3.shape, lambda b, t, i, pt: (0, 0))]
        + [slot_spec(g, s) for g in range(G_C) for s in range(n_slots)],
        out_specs=pl.BlockSpec((1, 1, H_C, DH), lambda b, t, i, pt: (b, t, 0, 0)))
    return pl.pallas_call(
        functools.partial(_slc_decode_body, n_slots=n_slots, p_len=p_len, valid=n_tok),
        grid_spec=grid_spec,
        out_shape=jax.ShapeDtypeStruct((batch, n_tok, H_C, DH), F32),
        compiler_params=_cparams("arbitrary", "arbitrary"),
    )(idx, page_table, q, new_rows, tab3, *([cache] * (G_C * n_slots)))


def _win_decode_body(q_ref, state_ref, new_ref, wbs_ref, wbn_ref, o_ref, *, valid):
    r = SAMPLE_ROWS
    wb = state_ref.shape[1]
    for g in range(G_C):
        q4 = (jnp.concatenate([q_ref[0, :, DH * h:DH * (h + 1)] for h in range(HPG_C * g, HPG_C * (g + 1))],
                              axis=0) * ATT_SCALE).astype(BF16)
        ks = state_ref[0, :, DH * g:DH * (g + 1)].astype(BF16)
        vs = state_ref[0, :, 128 + DH * g:128 + DH * (g + 1)].astype(BF16)
        kn = new_ref[0, :, DH * g:DH * (g + 1)].astype(BF16)
        vn = new_ref[0, :, 128 + DH * g:128 + DH * (g + 1)].astype(BF16)
        tok = _iota((HPG_C * r, 1), 0) & (r - 1)
        ok_s = _iota((HPG_C * r, wb), 1) >= tok + (wb - WINDOW)
        col_n = _iota((HPG_C * r, r), 1)
        ok_n = (col_n <= tok) & (col_n < valid)
        bias_s = jnp.concatenate([wbs_ref[h] for h in range(HPG_C * g, HPG_C * (g + 1))], axis=0)
        bias_n = jnp.concatenate([wbn_ref[h] for h in range(HPG_C * g, HPG_C * (g + 1))], axis=0)
        s_s = jnp.where(ok_s, _dot_nt(q4, ks) + bias_s, NEG)
        s_n = jnp.where(ok_n, _dot_nt(q4, kn) + bias_n, NEG)
        m = jnp.maximum(jnp.max(s_s, axis=1, keepdims=True), jnp.max(s_n, axis=1, keepdims=True))
        p_s = jnp.where(ok_s, jnp.exp(s_s - m), 0.0)
        p_n = jnp.where(ok_n, jnp.exp(s_n - m), 0.0)
        l = jnp.sum(p_s, axis=1, keepdims=True) + jnp.sum(p_n, axis=1, keepdims=True)
        o = (_dot(p_s.astype(BF16), vs) + _dot(p_n.astype(BF16), vn)) / l
        for j in range(HPG_C):
            h = HPG_C * g + j
            o_ref[0, :, DH * h:DH * (h + 1)] = o[r * j:r * (j + 1)]


def _win_decode(q, state, new_rows, wbs, wbn, valid):
    batch, wb, _ = state.shape
    r = SAMPLE_ROWS
    return pl.pallas_call(
        functools.partial(_win_decode_body, valid=valid),
        grid=(batch,),
        in_specs=[pl.BlockSpec((1, r, 512), lambda b: (b, 0, 0)),
                  pl.BlockSpec((1, wb, WIN_ROW), lambda b: (b, 0, 0)),
                  pl.BlockSpec((1, r, WIN_ROW), lambda b: (b, 0, 0)),
                  pl.BlockSpec((H_C, r, wb), lambda b: (0, 0, 0)),
                  pl.BlockSpec((H_C, r, r), lambda b: (0, 0, 0))],
        out_specs=pl.BlockSpec((1, r, 512), lambda b: (b, 0, 0)),
        out_shape=jax.ShapeDtypeStruct((batch, r, 512), F32),
        compiler_params=_cparams("arbitrary"),
    )(q, state, new_rows, wbs, wbn)


def _suffix_sum(x):
    n = x.shape[0]
    row = _iota(x.shape, 0)
    sh = 1
    while sh < n:
        x = x + jnp.where(row < n - sh, pltpu.roll(x, n - sh, axis=0), 0.0)
        sh *= 2
    return x


def _sb_decode_body(pt_ref, wq_ref, new_ref, *rest, pp, valid):
    pages = rest[:pp]
    o_ref, carry_ref, acc_ref, a_ref, v_ref = rest[pp:]
    st = pl.program_id(1)
    r = SAMPLE_ROWS
    wq = wq_ref[0].astype(BF16)

    @pl.when(st == 0)
    def _():
        kn = new_ref[0]
        z = _dot(kn[:, 0:256].astype(BF16), wq)
        lsz = _log_sigmoid(z)
        row = _iota((r, 128), 0)
        mask = (row < (_iota((r, 128), 1) & (r - 1))) & (row < valid)
        lf = jnp.where(mask, lsz - z, 0.0)
        a = jnp.where(mask, jnp.exp(lsz + _suffix_sum(lf) - lf), 0.0)
        pad = lambda x: jnp.concatenate([x, jnp.zeros((PAGE - r, x.shape[1]), x.dtype)], axis=0)
        acc_ref[...] = _dot_tn(pad(a).astype(BF16), pad(kn[:, 256:512]).astype(BF16))
        carry_ref[...] = jnp.sum(lf, axis=0, keepdims=True)

    carry = carry_ref[...]
    for j in range(pp - 1, -1, -1):
        pg = pages[j][0]
        z = _dot(pg[:, 0:256].astype(BF16), wq)
        lsz = _log_sigmoid(z)
        lf = lsz - z
        suf = _suffix_sum(lf)
        a_ref[PAGE * j:PAGE * (j + 1), :] = jnp.exp(lsz + (suf - lf) + carry).astype(BF16)
        v_ref[PAGE * j:PAGE * (j + 1), :] = pg[:, 256:512].astype(BF16)
        carry = carry + suf[0:1]
    carry_ref[...] = carry
    acc_ref[...] += _dot_tn(a_ref[...], v_ref[...])

    @pl.when(st == pl.num_programs(1) - 1)
    def _():
        o_ref[0] = acc_ref[...]


def _sb_decode(page_table, wq, new_rows, cache, valid):
    batch, n_pages = page_table.shape
    pp = min(PAGES_PER_STEP, n_pages)
    assert n_pages % pp == 0
    r = SAMPLE_ROWS
    grid_spec = pltpu.PrefetchScalarGridSpec(
        num_scalar_prefetch=1,
        grid=(batch, n_pages // pp),
        in_specs=[pl.BlockSpec((1, 256, 128), lambda b, s, pt: (b, 0, 0)),
                  pl.BlockSpec((1, r, SB_ROW), lambda b, s, pt: (b, 0, 0))]
        + _page_specs((1, PAGE, SB_ROW), pp, 0, lambda b, s, j, pt: pt[b, n_pages - (s + 1) * pp + j]),
        out_specs=pl.BlockSpec((1, 128, 256), lambda b, s, pt: (b, 0, 0)),
        scratch_shapes=[pltpu.VMEM((1, 128), F32), pltpu.VMEM((128, 256), F32),
                        pltpu.VMEM((pp * PAGE, 128), BF16), pltpu.VMEM((pp * PAGE, 256), BF16)])
    return pl.pallas_call(
        functools.partial(_sb_decode_body, pp=pp, valid=valid),
        grid_spec=grid_spec,
        out_shape=jax.ShapeDtypeStruct((batch, 128, 256), F32),
        compiler_params=_cparams("arbitrary", "arbitrary"),
    )(page_table, wq, new_rows, *([cache] * pp))


def _rope_tables(pos):
    half = ROPE_D // 2
    inv = ROPE_BASE ** (-jnp.arange(half, dtype=F32) / half)
    ang = pos.astype(F32)[:, None] * inv[None, :]
    cos, sin = jnp.cos(ang), jnp.sin(ang)
    return jnp.tile(cos, (1, 8)), jnp.tile(sin, (1, 8))


def _rot_cols(w):
    half = ROPE_D // 2
    return jnp.concatenate([-w[..., half:], w[..., :half]], axis=-1)


def _prep_even(w_in, w_qb, w_kb, w_vb):
    zkr = w_in[:, 2432:2464]
    w_e = jnp.concatenate([w_in[:, :2432], jnp.tile(zkr, (1, 4)), jnp.tile(_rot_cols(zkr), (1, 4))], axis=1)
    wqb = w_qb.reshape(Q_LORA, H_B, NOPE + ROPE_D)
    wr = wqb[:, :, NOPE:]
    w_q = jnp.concatenate([wqb[:, :, :NOPE].reshape(Q_LORA, 512), wr.reshape(Q_LORA, 256),
                           _rot_cols(wr).reshape(Q_LORA, 256)], axis=1)
    eye = jnp.eye(H_B, dtype=F32)
    wkb = jnp.einsum('hcn,hg->hngc', w_kb, eye).reshape(H_B * NOPE, H_B * KV_LORA)
    wvb = jnp.einsum('hcd,hg->hcgd', w_vb, eye).reshape(H_B * KV_LORA, H_B * VD_B)
    return w_e.astype(BF16), w_q.astype(BF16), wkb.astype(BF16), wvb.astype(BF16)


def _prep_odd(w_in):
    w = jnp.concatenate([w_in[:, :1280], w_in[:, 1304:2328], w_in[:, 1280:1304],
                         jnp.zeros((D_MODEL, ODD_COLS - 2328), w_in.dtype)], axis=1)
    return w.astype(BF16)


def _gate_expand():
    e = np.zeros((128, 3 * 512), np.float32)
    for h in range(H_C):
        for j in range(3):
            e[3 * h + j, 512 * j + DH * h:512 * j + DH * (h + 1)] = 1.0
    return jnp.asarray(e, BF16)


def _hgrn_params(hgrn_lb, norm_g, lj):
    lb_all = jax.nn.softmax(hgrn_lb.astype(F32), axis=0)
    lb = (jnp.cumsum(lb_all, axis=0) - lb_all[0])[lj]
    lb_pos = lb > 0
    rows = [jnp.log(jnp.where(lb_pos, lb, 1.0)), jnp.log1p(-lb), lb_pos.astype(F32), 1.0 - lb, norm_g[lj]]
    return jnp.concatenate([jnp.stack(rows), jnp.zeros((3, 512), F32)], axis=0)


def _t5_bucket(dist):
    exact = NUM_BUCKETS // 2
    d = jnp.maximum(dist, 0)
    large = exact + (jnp.log(jnp.maximum(d, 1).astype(F32) / exact)
                     / math.log(MAX_DISTANCE / exact) * (NUM_BUCKETS - exact)).astype(jnp.int32)
    return jnp.where(d < exact, d, jnp.minimum(large, NUM_BUCKETS - 1))


N_DIST = 256


def _bias_of(tab, dist):
    return tab[:, np.clip(dist, 0, N_DIST - 1)]


def _pad_rows(a, batch, n_tok):
    a = a.reshape(batch, n_tok, -1)
    return jnp.pad(a, ((0, 0), (0, SAMPLE_ROWS - n_tok), (0, 0)))


def _last_rows(rows, n):
    t = rows.shape[1]
    if t < n:
        rows = jnp.pad(rows, ((0, 0), (n - t, 0), (0, 0)))
    return rows[:, rows.shape[1] - n:]


def _run_prompt(x, mod, w, tabs, batch, seq, win_len):
    assert seq % HGRN_CHUNK == 0 and (batch * seq) % _tile_rows(batch * seq) == 0
    tq = min(TQ, seq)
    pos = np.arange(seq)
    cos, sin = _rope_tables(jnp.asarray(pos, jnp.int32))
    tab = tabs['tab']
    ij = np.arange(tq)[:, None] - np.arange(tq)[None, :]
    bd, bs = _bias_of(tab, ij), _bias_of(tab, tq + ij)
    nblk = seq // NSA_BLOCK
    nbp = -(-nblk // 128) * 128
    cb = _bias_of(tab, pos[:, None] - (np.arange(nbp)[None, :] * NSA_BLOCK + NSA_BLOCK - 1))
    states = []
    for l in range(DEPTH):
        lj = l // 2
        m = mod[l]
        x = _ffn(x, m, w['norm_g'][l], w['wg'][l, 0], w['wu'][l, 0], w['wd'][l, 0], 0, seq)
        if l % 2 == 0:
            w_e, w_q, wkb, wvb = w['even'][lj]
            zh, kfull, rows, qlat, qrope = _inproj_even(
                x, m, w['norm_g'][l], w_e, cos, sin, w['gq'][lj], w['gkv'][lj], w_q, wkb, seq)
            tb = min(TM_DENSE, seq)
            o_a, s_new = _hgrn(zh, w['hgrn_par'][lj], batch, seq, tb, HGRN_CHUNK, HGRN_SUB, HGRN_CHUNK)
            o_b = _mla_prompt(qlat, qrope, kfull, wvb, batch, seq)
            x = _outproj(x, m, [o_a, o_b], [], w['wout_even'][lj], seq, odd=False)
            states += [rows.reshape(batch, seq, MLA_ROW), s_new]
        else:
            qn, nsa, win, qs, sb, zg = _inproj_odd(x, m, w['norm_g'][l], w['odd'][lj], seq)
            cmp = _means_prompt(nsa).reshape(batch, nblk, 256)
            cmp = jnp.pad(cmp, ((0, 0), (0, nbp - nblk), (0, 0)))
            o_cmp, sel = _cmpsel_prompt(qn, cmp, cb, batch, seq)
            o_slc = _nsa_prompt_attn(qn, nsa, sel, bd, bs, tabs['cfar'], batch, seq)
            o_win = _nsa_prompt_attn(qn, win, None, bd, bs, tabs['cfar'], batch, seq)
            o_sb = _sb_prompt(qs, sb, batch, seq)
            x = _outproj(x, m, [o_cmp, o_slc, o_win, zg, o_sb], [w['gate_e']], w['wout_odd'][lj], seq, odd=True)
            states += [nsa.reshape(batch, seq, NSA_ROW), _last_rows(win.reshape(batch, seq, WIN_ROW), win_len),
                       sb.reshape(batch, seq, SB_ROW)]
        x = _ffn(x, m, w['norm_g'][l], w['wg'][l, 1], w['wu'][l, 1], w['wd'][l, 1], 2, seq,
                 final_g=w['final_g'] if l == DEPTH - 1 else None)
    return x.reshape(batch, seq, D_MODEL), states


def _run_sample(x, mod, w, tabs, batch, n_tok, page_table, pasts):
    n = batch * n_tok
    n_pages = page_table.shape[1]
    p_len = n_pages * PAGE
    r = SAMPLE_ROWS
    assert n_tok <= r and n_tok <= NSA_BLOCK and n == _tile_rows(n)
    pos = p_len + np.arange(r)
    cos, sin = _rope_tables(jnp.asarray(np.tile(pos[:n_tok], batch), jnp.int32))
    tab = tabs['tab']
    nb = p_len // NSA_BLOCK
    n_past_sel = min(TOP_N, nb + 1) - 1
    cb = _bias_of(tab, pos[:, None] - (np.arange(nb)[None, :] * NSA_BLOCK + NSA_BLOCK - 1))
    tr = np.arange(r)
    states = []
    for l in range(DEPTH):
        lj = l // 2
        m = mod[l]
        x = _ffn(x, m, w['norm_g'][l], w['wg'][l, 0], w['wu'][l, 0], w['wd'][l, 0], 0, n)
        if l % 2 == 0:
            cache_mla, state_hgrn = pasts[l]
            w_e, w_q, wkb, wvb = w['even'][lj]
            zh, kfull, rows, qlat, qrope = _inproj_even(
                x, m, w['norm_g'][l], w_e, cos, sin, w['gq'][lj], w['gkv'][lj], w_q, wkb, n)
            zh8 = _pad_rows(zh, batch, n_tok).reshape(batch * r, 2048)
            o_a8, s_new = _hgrn(zh8, w['hgrn_par'][lj], batch, r, r, r, r, n_tok, s0=state_hgrn)
            o_a = o_a8.reshape(batch, r, 512)[:, :n_tok].reshape(n, 512)
            qf = jnp.concatenate([qlat.reshape(batch, n_tok, H_B, KV_LORA),
                                  qrope.reshape(batch, n_tok, H_B, ROPE_D)], axis=-1)
            qf = jnp.pad(qf, ((0, 0), (0, r - n_tok), (0, 0), (0, 0)))
            qf = jnp.swapaxes(qf, 1, 2).reshape(batch, H_B * r, MLA_ROW)
            o_b8 = _mla_decode(page_table, qf, _pad_rows(rows, batch, n_tok), wvb, cache_mla, n_tok)
            o_b = o_b8[:, :n_tok].reshape(n, 512)
            x = _outproj(x, m, [o_a, o_b], [], w['wout_even'][lj], n, odd=False)
            states += [rows.reshape(batch, n_tok, MLA_ROW), s_new]
        else:
            cache_nsa, state_win, cache_sb = pasts[l]
            wb = state_win.shape[1]
            assert p_len >= wb and wb <= WINDOW
            qn, nsa, win, qs, sb, zg = _inproj_odd(x, m, w['norm_g'][l], w['odd'][lj], n)
            qn8 = _pad_rows(qn, batch, n_tok)
            nsa8 = _pad_rows(nsa, batch, n_tok)
            cmp = _means_decode(page_table, cache_nsa)
            o_cmp8, picked = _cmpsel_decode(qn8, cmp, cb, n_past_sel)
            idx = picked[..., :n_past_sel].reshape(-1)
            o_slc = _slc_decode(idx, page_table, qn.reshape(batch, n_tok, H_C, DH), nsa8, tabs['tab3'],
                                cache_nsa, n_tok, p_len)
            wbs = _bias_of(tab, wb + tr[:, None] - np.arange(wb)[None, :])
            wbn = _bias_of(tab, tr[:, None] - tr[None, :])
            o_win8 = _win_decode(qn8, state_win, _pad_rows(win, batch, n_tok), wbs, wbn, n_tok)
            q5 = _pad_rows(qs * ATT_SCALE, batch, n_tok).reshape(batch, r, KV_D, HPK_D, DH)
            q5 = jnp.transpose(q5, (0, 2, 4, 3, 1)).reshape(batch, KV_D, DH, HPK_D * r)
            wq = jnp.einsum('bgdc,gh->bgdhc', q5, jnp.eye(KV_D, dtype=F32)).reshape(batch, KV_D * DH, KV_D * HPK_D * r)
            wq = jnp.pad(wq, ((0, 0), (0, 0), (0, 128 - wq.shape[2])))
            o_raw = _sb_decode(page_table, wq, _pad_rows(sb, batch, n_tok), cache_sb, n_tok)
            o6 = o_raw[:, :KV_D * HPK_D * r].reshape(batch, KV_D, HPK_D, r, KV_D, DH)
            o_sb = jnp.stack([o6[:, g, :, :, g] for g in range(KV_D)], axis=1)
            o_sb = jnp.transpose(o_sb, (0, 3, 1, 2, 4))[:, :n_tok].reshape(n, 512)
            take = lambda a: a[:, :n_tok].reshape(n, 512)
            x = _outproj(x, m, [take(o_cmp8), o_slc.reshape(n, 512), take(o_win8), zg, o_sb], [w['gate_e']],
                         w['wout_odd'][lj], n, odd=True)
            new_win = jnp.concatenate([state_win, win.reshape(batch, n_tok, WIN_ROW)], axis=1)[:, n_tok:]
            states += [nsa.reshape(batch, n_tok, NSA_ROW), new_win, sb.reshape(batch, n_tok, SB_ROW)]
        x = _ffn(x, m, w['norm_g'][l], w['wg'][l, 1], w['wu'][l, 1], w['wd'][l, 1], 2, n,
                 final_g=w['final_g'] if l == DEPTH - 1 else None)
    return x.reshape(batch, n_tok, D_MODEL), states


def kernel(x_prompt, x_sample, cache_mla_l0, state_hgrn_l0, cache_nsa_l1, state_win_l1, cache_sb_l1, cache_mla_l2, state_hgrn_l2, cache_nsa_l3, state_win_l3, cache_sb_l3, page_table, c_prompt, c_sample, w_ada, b_ada, norm_g, ffn_w_gate, ffn_w_up, ffn_w_down, w_in_even, w_out_even, hgrn_lb, hgrn_norm_g, mla_q_norm_g, mla_kv_norm_g, mla_w_qb, mla_w_kb, mla_w_vb, w_in_odd, w_out_odd, rel_bias, final_norm_g):
    bp, seq, _ = x_prompt.shape
    bs, n_tok, _ = x_sample.shape
    n_even = w_in_even.shape[0]
    n_odd = w_in_odd.shape[0]
    w = dict(
        norm_g=norm_g, final_g=final_norm_g,
        wg=ffn_w_gate.astype(BF16), wu=ffn_w_up.astype(BF16), wd=ffn_w_down.astype(BF16),
        even=[_prep_even(w_in_even[j], mla_w_qb[j], mla_w_kb[j], mla_w_vb[j]) for j in range(n_even)],
        odd=[_prep_odd(w_in_odd[j]) for j in range(n_odd)],
        wout_even=w_out_even.astype(BF16), wout_odd=w_out_odd.astype(BF16),
        gq=mla_q_norm_g.reshape(n_even, 1, Q_LORA), gkv=mla_kv_norm_g.reshape(n_even, 1, KV_LORA),
        hgrn_par=[_hgrn_params(hgrn_lb, hgrn_norm_g, j) for j in range(n_even)],
        gate_e=_gate_expand())
    tab = rel_bias[_t5_bucket(jnp.arange(N_DIST, dtype=jnp.int32))].T.astype(F32)
    tabs = dict(tab=tab, cfar=jnp.broadcast_to(tab[:, N_DIST - 1:], (H_C, 128)),
                tab3=jnp.concatenate(_split3(tab), axis=0))
    mod = _ada_mod(jnp.concatenate([c_prompt, c_sample], axis=0), w_ada, b_ada)
    mod = mod.reshape(DEPTH, bp + bs, N_MOD, D_MODEL)
    mod_p = mod[:, :bp, :, None, :]
    mod_s = jnp.transpose(jnp.repeat(mod[:, bp:], n_tok, axis=1), (0, 2, 1, 3))[:, None]
    win_len = state_win_l1.shape[1]
    y_p, st_p = _run_prompt(x_prompt.reshape(bp * seq, D_MODEL), mod_p, w, tabs, bp, seq, win_len)
    pasts = [(cache_mla_l0, state_hgrn_l0), (cache_nsa_l1, state_win_l1, cache_sb_l1),
             (cache_mla_l2, state_hgrn_l2), (cache_nsa_l3, state_win_l3, cache_sb_l3)]
    y_s, st_s = _run_sample(x_sample.reshape(bs * n_tok, D_MODEL), mod_s, w, tabs, bs, n_tok, page_table, pasts)
    out = [y_p, y_s]
    for a, b in zip(st_p, st_s):
        out += [a, b]
    return tuple(out)
```

```python
import functools
import math

import numpy as np
import jax
import jax.numpy as jnp
from jax import lax
from jax.experimental import pallas as pl
from jax.experimental.pallas import tpu as pltpu

F32 = jnp.float32
BF16 = jnp.bfloat16

D_MODEL = 1024
DEPTH = 4
PAGE = 128
NORM_EPS = 1e-6
NEG = -1e30
PICKED = -3e38
N_MOD = 9
D_FF = 2816

H_A = 4
DK_A = 128
HGRN_CHUNK = 64
HGRN_SUB = 16

H_B = 8
Q_LORA = 256
KV_LORA = 128
NOPE = 64
ROPE_D = 32
VD_B = 64
ROPE_BASE = 10000.0
MLA_ROW = KV_LORA + ROPE_D
MLA_SCALE = (NOPE + ROPE_D) ** -0.5

DH = 64
H_C = 8
G_C = 2
HPG_C = 4
NSA_BLOCK = 64
TOP_N = 8
WINDOW = 512
FORCE_BONUS = 100.0
NSA_ROW = 512
WIN_ROW = 256
H_D = 8
KV_D = 4
HPK_D = 2
SB_ROW = 512
ATT_SCALE = DH ** -0.5
NUM_BUCKETS = 32
MAX_DISTANCE = 128

V7X_VMEM_BYTES = 64 * 1024 * 1024
VMEM_LIMIT = V7X_VMEM_BYTES - 8 * 1024 * 1024

TM_DENSE = 512
FF_CHUNK = 1408
TQ = 256
SAMPLE_ROWS = 8
PAGES_PER_STEP = 16


def _cparams(*sem):
    return pltpu.CompilerParams(dimension_semantics=sem, vmem_limit_bytes=VMEM_LIMIT)


def _iota(shape, dim):
    return lax.broadcasted_iota(jnp.int32, shape, dim)


def _rms(x):
    return x * lax.rsqrt(jnp.mean(x * x, axis=-1, keepdims=True) + NORM_EPS)


def _silu(x):
    return x * jax.nn.sigmoid(x)


def _log_sigmoid(x):
    return jnp.minimum(x, 0.0) - jnp.log1p(jnp.exp(-jnp.abs(x)))


def _dot(a, b):
    return jnp.dot(a, b, preferred_element_type=F32)


def _dot_nt(a, b):
    return lax.dot_general(a, b, (((1,), (1,)), ((), ())), preferred_element_type=F32)


def _dot_tn(a, b):
    return lax.dot_general(a, b, (((0,), (0,)), ((), ())), preferred_element_type=F32)


def _split3(x):
    hi = x.astype(BF16)
    r = x - hi.astype(F32)
    mid = r.astype(BF16)
    lo = (r - mid.astype(F32)).astype(BF16)
    return hi, mid, lo


def _ada_body(c_ref, w_ref, b_ref, o_ref):
    h = _silu(c_ref[...]).astype(BF16)
    o_ref[0] = _dot(h, w_ref[0].astype(BF16)) + b_ref[0]


def _ada_mod(c_all, w_ada, b_ada):
    nc = c_all.shape[0]
    depth, _, ncol = w_ada.shape
    tn = 1024
    return pl.pallas_call(
        _ada_body,
        grid=(depth, ncol // tn),
        in_specs=[pl.BlockSpec((nc, D_MODEL), lambda l, j: (0, 0)),
                  pl.BlockSpec((1, D_MODEL, tn), lambda l, j: (l, 0, j)),
                  pl.BlockSpec((1, 1, tn), lambda l, j: (l, 0, j))],
        out_specs=pl.BlockSpec((1, nc, tn), lambda l, j: (l, 0, j)),
        out_shape=jax.ShapeDtypeStruct((depth, nc, ncol), F32),
        compiler_params=_cparams("arbitrary", "arbitrary"),
        name="ada_mod",
    )(c_all, w_ada, b_ada.reshape(depth, 1, ncol))


def _tile_rows(n):
    return min(TM_DENSE, n)


def _mod_spec(mod, tm, seq_len):
    s, _, r, _ = mod.shape
    if r == 1:
        tiles_per_seq = seq_len // tm
        return pl.BlockSpec((1, N_MOD, 1, D_MODEL), lambda i: (i // tiles_per_seq, 0, 0, 0))
    return pl.BlockSpec((1, N_MOD, tm, D_MODEL), lambda i: (0, 0, i, 0))


def _prenorm(x, mod_ref, g_ref, sub):
    shift = mod_ref[0, 3 * sub]
    scale = mod_ref[0, 3 * sub + 1]
    return _rms(x) * g_ref[sub:sub + 1, :] * (1.0 + scale) + shift


def _const_spec(shape):
    nd = len(shape)
    return pl.BlockSpec(shape, lambda i: (0,) * nd, pipeline_mode=pl.Buffered(1))


def _ffn_body(x_ref, mod_ref, g_ref, wg_ref, wu_ref, wd_ref, *rest, sub, final):
    o_ref = rest[-1]
    x = x_ref[...]
    h = _prenorm(x, mod_ref, g_ref, sub).astype(BF16)
    acc = None
    for c0 in range(0, D_FF, FF_CHUNK):
        a = _dot(h, wg_ref[:, c0:c0 + FF_CHUNK])
        u = _dot(h, wu_ref[:, c0:c0 + FF_CHUNK])
        t = (_silu(a) * u).astype(BF16)
        part = _dot(t, wd_ref[c0:c0 + FF_CHUNK, :])
        acc = part if acc is None else acc + part
    y = x + 0.5 * mod_ref[0, 3 * sub + 2] * acc
    if final:
        y = _rms(y) * rest[0][...]
    o_ref[...] = y


def _ffn(x, mod, norm_g, wg, wu, wd, sub, seq_len, final_g=None):
    n = x.shape[0]
    tm = _tile_rows(n)
    final = final_g is not None
    in_specs = [pl.BlockSpec((tm, D_MODEL), lambda i: (i, 0)),
                _mod_spec(mod, tm, seq_len),
                _const_spec((3, D_MODEL)),
                _const_spec((D_MODEL, D_FF)), _const_spec((D_MODEL, D_FF)), _const_spec((D_FF, D_MODEL))]
    args = [x, mod, norm_g, wg, wu, wd]
    if final:
        in_specs.append(_const_spec((1, D_MODEL)))
        args.append(final_g.reshape(1, D_MODEL))
    return pl.pallas_call(
        functools.partial(_ffn_body, sub=sub, final=final),
        grid=(n // tm,),
        in_specs=in_specs,
        out_specs=pl.BlockSpec((tm, D_MODEL), lambda i: (i, 0)),
        out_shape=jax.ShapeDtypeStruct((n, D_MODEL), F32),
        compiler_params=_cparams("arbitrary"),
        name="ffn",
    )(*args)


EVEN_COLS = 2688
ODD_COLS = 2432


def _inproj_even_body(x_ref, mod_ref, g_ref, w_ref, cos_ref, sin_ref, gq_ref, gkv_ref, wq_ref, wkb_ref,
                      zh_ref, kfull_ref, rows_ref, qlat_ref, qrope_ref):
    h = _prenorm(x_ref[...], mod_ref, g_ref, 1).astype(BF16)
    zh_ref[...] = _dot(h, w_ref[:, 0:2048])
    z = _dot(h, w_ref[:, 2048:EVEN_COLS])
    cos = cos_ref[...]
    sin = sin_ref[...]
    ckv = _rms(z[:, 256:384]) * gkv_ref[...]
    krope = z[:, 384:512] * cos + z[:, 512:640] * sin
    kfull_ref[:, 0:128] = ckv
    kfull_ref[:, 128:256] = krope
    rows_ref[:, 0:128] = ckv
    rows_ref[:, 128:MLA_ROW] = krope[:, 0:ROPE_D]
    qa = (_rms(z[:, 0:256]) * gq_ref[...]).astype(BF16)
    qz = _dot(qa, wq_ref[...])
    qlat_ref[...] = _dot(qz[:, 0:512].astype(BF16), wkb_ref[...]) * MLA_SCALE
    cos2 = jnp.concatenate([cos, cos], axis=1)
    sin2 = jnp.concatenate([sin, sin], axis=1)
    qrope_ref[...] = (qz[:, 512:768] * cos2 + qz[:, 768:1024] * sin2) * MLA_SCALE


def _inproj_even(x, mod, norm_g, w, cos, sin, gq, gkv, wq, wkb, seq_len):
    n = x.shape[0]
    tm = _tile_rows(n)
    tab_tiles = cos.shape[0] // tm
    tok = lambda c: pl.BlockSpec((tm, c), lambda i: (i, 0))
    tab = pl.BlockSpec((tm, 128), lambda i: (i % tab_tiles, 0))
    return pl.pallas_call(
        _inproj_even_body,
        grid=(n // tm,),
        in_specs=[tok(D_MODEL), _mod_spec(mod, tm, seq_len), _const_spec((3, D_MODEL)),
                  _const_spec((D_MODEL, EVEN_COLS)), tab, tab,
                  _const_spec((1, Q_LORA)), _const_spec((1, KV_LORA)),
                  _const_spec((Q_LORA, 1024)), _const_spec((512, 1024))],
        out_specs=[tok(2048), tok(256), tok(MLA_ROW), tok(1024), tok(256)],
        out_shape=[jax.ShapeDtypeStruct((n, c), F32) for c in (2048, 256, MLA_ROW, 1024, 256)],
        compiler_params=_cparams("arbitrary"),
        name="inproj_even",
    )(x, mod, norm_g, w, cos, sin, gq, gkv, wq, wkb)


def _inproj_odd_body(x_ref, mod_ref, g_ref, w_ref, qn_ref, nsa_ref, win_ref, qs_ref, sb_ref, zg_ref):
    h = _prenorm(x_ref[...], mod_ref, g_ref, 1).astype(BF16)
    qn_ref[...] = _dot(h, w_ref[:, 0:512])
    nsa_ref[...] = _dot(h, w_ref[:, 512:1024])
    win_ref[...] = _dot(h, w_ref[:, 1024:1280])
    qs_ref[...] = _dot(h, w_ref[:, 1280:1792])
    sb_ref[...] = _dot(h, w_ref[:, 1792:2304])
    zg_ref[...] = _dot(h, w_ref[:, 2304:ODD_COLS])


def _inproj_odd(x, mod, norm_g, w, seq_len):
    n = x.shape[0]
    tm = _tile_rows(n)
    tok = lambda c: pl.BlockSpec((tm, c), lambda i: (i, 0))
    cols = (512, NSA_ROW, WIN_ROW, 512, SB_ROW, 128)
    return pl.pallas_call(
        _inproj_odd_body,
        grid=(n // tm,),
        in_specs=[tok(D_MODEL), _mod_spec(mod, tm, seq_len), _const_spec((3, D_MODEL)),
                  _const_spec((D_MODEL, ODD_COLS))],
        out_specs=[tok(c) for c in cols],
        out_shape=[jax.ShapeDtypeStruct((n, c), F32) for c in cols],
        compiler_params=_cparams("arbitrary"),
        name="inproj_odd",
    )(x, mod, norm_g, w)


def _outproj_even_body(x_ref, mod_ref, oa_ref, ob_ref, w_ref, o_ref):
    mix = (_dot(oa_ref[...].astype(BF16), w_ref[0:512, :])
           + _dot(ob_ref[...].astype(BF16), w_ref[512:1024, :]))
    o_ref[...] = x_ref[...] + mod_ref[0, 5] * mix


def _outproj_odd_body(x_ref, mod_ref, ocmp_ref, oslc_ref, owin_ref, zg_ref, osb_ref, e_ref, w_ref, o_ref):
    hi, mid, lo = _split3(jax.nn.sigmoid(zg_ref[...]))
    e = e_ref[...]
    gexp = _dot(hi, e) + _dot(mid, e) + _dot(lo, e)
    nsa = (gexp[:, 0:512] * ocmp_ref[...] + gexp[:, 512:1024] * oslc_ref[...]
           + gexp[:, 1024:1536] * owin_ref[...])
    mix = (_dot(nsa.astype(BF16), w_ref[0:512, :])
           + _dot(osb_ref[...].astype(BF16), w_ref[512:1024, :]))
    o_ref[...] = x_ref[...] + mod_ref[0, 5] * mix


def _outproj(x, mod, parts, consts, w, seq_len, odd):
    n = x.shape[0]
    tm = _tile_rows(n)
    tok = lambda c: pl.BlockSpec((tm, c), lambda i: (i, 0))
    in_specs = ([tok(D_MODEL), _mod_spec(mod, tm, seq_len)] + [tok(p.shape[1]) for p in parts]
                + [_const_spec(c.shape) for c in consts] + [_const_spec((1024, D_MODEL))])
    return pl.pallas_call(
        _outproj_odd_body if odd else _outproj_even_body,
        grid=(n // tm,),
        in_specs=in_specs,
        out_specs=tok(D_MODEL),
        out_shape=jax.ShapeDtypeStruct((n, D_MODEL), F32),
        compiler_params=_cparams("arbitrary"),
        name="outproj_odd" if odd else "outproj_even",
    )(x, mod, *parts, *consts, w)


def _softmax_init(m_ref, l_ref, acc_ref):
    m_ref[...] = jnp.full(m_ref.shape, NEG, F32)
    l_ref[...] = jnp.zeros(l_ref.shape, F32)
    acc_ref[...] = jnp.zeros(acc_ref.shape, F32)


def _softmax_step(s, mask, v, m_ref, l_ref, acc_ref, v_is_transposed=False):
    if mask is not None:
        s = jnp.where(mask, s, NEG)
    m_prev = m_ref[...]
    m_new = jnp.maximum(m_prev, jnp.max(s, axis=1, keepdims=True))
    p = jnp.exp(s - m_new)
    if mask is not None:
        p = jnp.where(mask, p, 0.0)
    alpha = jnp.exp(m_prev - m_new)
    l_ref[...] = alpha * l_ref[...] + jnp.sum(p, axis=1, keepdims=True)
    pv = _dot_nt(p.astype(BF16), v) if v_is_transposed else _dot(p.astype(BF16), v)
    acc_ref[...] = alpha * acc_ref[...] + pv
    m_ref[...] = m_new


def _softmax_out(l_ref, acc_ref):
    l = l_ref[...]
    return jnp.where(l > 0.0, acc_ref[...] / jnp.where(l > 0.0, l, 1.0), 0.0)


def _tsoftmax_step(s_t, mask_t, v_t, m_ref, l_ref, acc_ref):
    if mask_t is not None:
        s_t = jnp.where(mask_t, s_t, NEG)
    m_prev = m_ref[...]
    m_new = jnp.maximum(m_prev, jnp.max(s_t, axis=0, keepdims=True))
    p = jnp.exp(s_t - m_new)
    alpha = jnp.exp(m_prev - m_new)
    l_ref[...] = alpha * l_ref[...] + jnp.sum(p, axis=0, keepdims=True)
    acc_ref[...] = alpha * acc_ref[...] + _dot(v_t, p.astype(BF16))
    m_ref[...] = m_new


def _tsoftmax_out(l_ref, acc_ref):
    return acc_ref[...] * (1.0 / l_ref[...])


def _mla_stack_q_t(qlat, qrope):
    tq = qlat.shape[0]
    ql_t = qlat.T
    qr_t = qrope.T
    sub = _iota((128, tq), 0)
    parts = []
    for h in range(H_B):
        qr = qr_t[128 * (h // 4):128 * (h // 4 + 1)]
        qr = jnp.where((sub // ROPE_D) == (h % 4), qr, 0.0)
        parts.append(jnp.concatenate([ql_t[128 * h:128 * (h + 1)], qr], axis=0))
    return jnp.concatenate(parts, axis=1).astype(BF16)


def _mla_prompt_body(qlat_ref, qrope_ref, kfull_ref, wvbt_ref, o_ref, k_scr, vt_scr, m_ref, l_ref, acc_ref, *, tq):
    qi = pl.program_id(1)

    @pl.when(qi == 0)
    def _():
        kf = kfull_ref[0]
        k_scr[...] = kf.astype(BF16)
        for kb in range(kf.shape[0] // tq):
            vt_scr[kb] = kf[kb * tq:(kb + 1) * tq, 0:KV_LORA].T.astype(BF16)

    q_t = _mla_stack_q_t(qlat_ref[...], qrope_ref[...])
    _softmax_init(m_ref, l_ref, acc_ref)
    shape = (tq, H_B * tq)

    def step(kb, diag):
        k = k_scr[pl.ds(pl.multiple_of(kb * tq, tq), tq), :]
        mask = (_iota(shape, 0) <= (_iota(shape, 1) & (tq - 1))) if diag else None
        _tsoftmax_step(_dot(k, q_t), mask, vt_scr[kb], m_ref, l_ref, acc_ref)

    def far(kb, c):
        step(kb, False)
        return c

    lax.fori_loop(0, qi, far, 0)
    step(qi, True)
    o_t = _tsoftmax_out(l_ref, acc_ref)
    olat_t = jnp.concatenate([o_t[:, h * tq:(h + 1) * tq] for h in range(H_B)], axis=0)
    o_ref[...] = _dot(wvbt_ref[...], olat_t.astype(BF16)).T


def _mla_prompt(qlat, qrope, kfull, wvbt, batch, seq):
    tq = min(TQ, seq)
    assert tq & (tq - 1) == 0 and seq % tq == 0
    nq = seq // tq
    tok = lambda c: pl.BlockSpec((tq, c), lambda b, i: (b * nq + i, 0))
    return pl.pallas_call(
        functools.partial(_mla_prompt_body, tq=tq),
        grid=(batch, nq),
        in_specs=[tok(1024), tok(256), pl.BlockSpec((1, seq, 256), lambda b, i: (b, 0, 0)),
                  pl.BlockSpec((512, 1024), lambda b, i: (0, 0))],
        out_specs=tok(512),
        out_shape=jax.ShapeDtypeStruct((batch * seq, 512), F32),
        scratch_shapes=[pltpu.VMEM((seq, 256), BF16), pltpu.VMEM((nq, KV_LORA, tq), BF16),
                        pltpu.VMEM((1, H_B * tq), F32), pltpu.VMEM((1, H_B * tq), F32),
                        pltpu.VMEM((KV_LORA, H_B * tq), F32)],
        compiler_params=_cparams("arbitrary", "arbitrary"),
        name="mla_prompt",
    )(qlat, qrope, kfull.reshape(batch, seq, 256), wvbt)


def _hgrn_chunk(zq, zf, zi, zg, par, st, chunk, sub, valid):
    log_lb, log1m_lb, lb_pos, one_m_lb, gn = par[0:1], par[1:2], par[2:3], par[3:4], par[4:5]
    q = _silu(zq)
    ls = _log_sigmoid(zf)
    b = log1m_lb + ls
    lae = jnp.maximum(log_lb, b) + jnp.log1p(jnp.exp(-jnp.abs(log_lb - b)))
    logf = jnp.where(lb_pos > 0.5, lae, ls)
    kin = one_m_lb * jax.nn.sigmoid(-zf)
    v = zi
    row = _iota((chunk, 128), 0)
    g = logf
    sh = 1
    while sh < chunk:
        g = g + jnp.where(row >= sh, pltpu.roll(g, sh, axis=0), 0.0)
        sh *= 2
    o = _dot_nt((q * jnp.exp(g)).astype(BF16), st.astype(BF16))
    nsub = chunk // sub
    v_bf = v.astype(BF16)
    if nsub > 1:
        ends = [g[sub * j + sub - 1:sub * j + sub] for j in range(nsub)]
        esub = jnp.concatenate([jnp.broadcast_to(e, (sub, 128)) for e in ends], axis=0)
        kt = (kin * jnp.exp(esub - g)).astype(BF16)
        col_a = _iota((chunk, chunk), 1)
        row_a = _iota((chunk, chunk), 0)
        a = jnp.zeros((chunk, chunk), F32)
        for j in range(nsub - 1):
            qj = (q * jnp.exp(jnp.minimum(g - ends[j], 0.0))).astype(BF16)
            aj = _dot_nt(qj, kt)
            a = jnp.where((col_a >= sub * j) & (col_a < sub * (j + 1)) & (row_a >= sub * (j + 1)), aj, a)
        o = o + _dot(a.astype(BF16), v_bf)
    row_s = _iota((sub, 128), 0)
    diag = []
    for i in range(nsub):
        g_i = g[sub * i:sub * (i + 1)]
        q_i = q[sub * i:sub * (i + 1)]
        o_i = jnp.zeros((sub, 128), F32)
        for s in range(min(sub, valid - sub * i)):
            r = sub * i + s
            e = jnp.exp(jnp.minimum(g_i - g[r:r + 1], 0.0))
            x = jnp.where(row_s >= s, q_i * (kin[r:r + 1] * e), 0.0)
            o_i = o_i + jnp.sum(x, axis=1, keepdims=True) * v[r:r + 1]
        diag.append(o_i)
    o = o + (diag[0] if nsub == 1 else jnp.concatenate(diag, axis=0))
    g_last = g[valid - 1:valid]
    khat = kin * jnp.exp(jnp.minimum(g_last - g, 0.0))
    if valid < chunk:
        khat = jnp.where(row < valid, khat, 0.0)
    st_new = st * jnp.exp(g_last) + _dot_tn(v_bf, khat.astype(BF16))
    return _rms(o) * gn * _silu(zg), st_new


def _hgrn_body(zq_ref, zf_ref, zi_ref, zg_ref, par_ref, *rest, chunk, sub, valid, n_chunks, has_s0):
    if has_s0:
        s0_ref, o_ref, sout_ref, st_ref = rest
    else:
        o_ref, sout_ref, st_ref = rest
    tb = pl.program_id(1)

    @pl.when(tb == 0)
    def _():
        for h in range(H_A):
            st_ref[h] = s0_ref[0, h].T if has_s0 else jnp.zeros((128, 128), F32)

    def chunk_body(c, carry):
        r0 = pl.multiple_of(c * chunk, chunk)
        for h in range(H_A):
            cs = slice(128 * h, 128 * (h + 1))
            rs = pl.ds(r0, chunk)
            o, st_new = _hgrn_chunk(zq_ref[rs, cs], zf_ref[rs, cs], zi_ref[rs, cs], zg_ref[rs, cs],
                                    par_ref[:, cs], st_ref[h], chunk, sub, valid)
            o_ref[rs, cs] = o
            st_ref[h] = st_new
        return carry

    lax.fori_loop(0, n_chunks, chunk_body, 0)

    @pl.when(tb == pl.num_programs(1) - 1)
    def _():
        for h in range(H_A):
            sout_ref[0, h] = st_ref[h].T


def _hgrn(zh, par, batch, rows_per_seq, tb, chunk, sub, valid, s0=None):
    nt = rows_per_seq // tb
    col = lambda j: pl.BlockSpec((tb, 512), lambda b, t: (b * nt + t, j))
    in_specs = [col(0), col(1), col(2), col(3), pl.BlockSpec((8, 512), lambda b, t: (0, 0))]
    args = [zh, zh, zh, zh, par]
    st_spec = pl.BlockSpec((1, H_A, 128, 128), lambda b, t: (b, 0, 0, 0))
    if s0 is not None:
        in_specs.append(st_spec)
        args.append(s0)
    return pl.pallas_call(
        functools.partial(_hgrn_body, chunk=chunk, sub=sub, valid=valid, n_chunks=tb // chunk,
                          has_s0=s0 is not None),
        grid=(batch, nt),
        in_specs=in_specs,
        out_specs=[pl.BlockSpec((tb, 512), lambda b, t: (b * nt + t, 0)), st_spec],
        out_shape=[jax.ShapeDtypeStruct((batch * rows_per_seq, 512), F32),
                   jax.ShapeDtypeStruct((batch, H_A, 128, 128), F32)],
        scratch_shapes=[pltpu.VMEM((H_A, 128, 128), F32)],
        compiler_params=_cparams("arbitrary", "arbitrary"),
        name="hgrn",
    )(*args)


def _block_means(x):
    nb = x.shape[0] // NSA_BLOCK
    return jnp.sum(x.reshape(nb, NSA_BLOCK, x.shape[1]), axis=1) * (1.0 / NSA_BLOCK)


def _means_prompt_body(x_ref, o_ref):
    o_ref[...] = _block_means(x_ref[...])


def _means_prompt(nsa_rows):
    n = nsa_rows.shape[0]
    tm = _tile_rows(n)
    return pl.pallas_call(
        _means_prompt_body,
        grid=(n // tm,),
        in_specs=[pl.BlockSpec((tm, 256), lambda i: (i, 0))],
        out_specs=pl.BlockSpec((tm // NSA_BLOCK, 256), lambda i: (i, 0)),
        out_shape=jax.ShapeDtypeStruct((n // NSA_BLOCK, 256), F32),
        compiler_params=_cparams("arbitrary"),
        name="nsa_means_prompt",
    )(nsa_rows)


def _cmp_head(qh, ck, cv, bias, vis):
    s = _dot_nt(qh.astype(BF16), ck) * ATT_SCALE + bias
    if vis is not None:
        s = jnp.where(vis, s, NEG)
    e = jnp.exp(s - jnp.max(s, axis=1, keepdims=True))
    p = e / jnp.sum(e, axis=1, keepdims=True)
    if vis is not None:
        p = jnp.where(vis, p, 0.0)
    return _dot(p.astype(BF16), cv), p


def _top_blocks(score, blk, n_sel):
    nb = score.shape[1]
    for _ in range(n_sel):
        m = jnp.max(score, axis=1, keepdims=True)
        idx = jnp.min(jnp.where(score == m, blk, nb), axis=1, keepdims=True)
        yield idx, m > 0.5 * NEG
        score = jnp.where(blk == idx, PICKED, score)


def _cmpsel_prompt_body(q_ref, cmp_ref, near_ref, cfar_ref, ocmp_ref, sel_ref, *, tq, nbp, n_sel):
    qi = pl.program_id(1)
    q_t = q_ref[...].T
    shape = (nbp, tq)
    qpos = qi * tq + _iota(shape, 1)
    blk = _iota(shape, 0)
    vis = (blk * NSA_BLOCK + NSA_BLOCK - 1) <= qpos
    cur = jnp.right_shift(qpos, 6)
    allowed = blk <= cur
    forced = (blk == 0) | (blk == cur) | (blk == cur - 1)
    near1 = blk == cur - 1
    near2 = blk == cur - 2
    outs = []
    for g in range(G_C):
        ck = cmp_ref[0, :, DH * g:DH * (g + 1)].astype(BF16)
        cv_t = cmp_ref[0, :, 128 + DH * g:128 + DH * (g + 1)].T.astype(BF16)
        imp = jnp.zeros(shape, F32)
        for j in range(HPG_C):
            h = HPG_C * g + j
            bias = jnp.where(near1, near_ref[0, h:h + 1, :],
                             jnp.where(near2, near_ref[1, h:h + 1, :], cfar_ref[h:h + 1, 0:1]))
            bias = jnp.where(blk == cur, cfar_ref[h:h + 1, 1:2], bias)
            s = _dot(ck, q_t[DH * h:DH * (h + 1)].astype(BF16)) * ATT_SCALE + bias
            s = jnp.where(vis, s, NEG)
            e = jnp.exp(s - jnp.max(s, axis=0, keepdims=True))
            p = jnp.where(vis, e / jnp.sum(e, axis=0, keepdims=True), 0.0)
            outs.append(_dot(cv_t, p.astype(BF16)))
            imp = imp + p
        score = jnp.where(allowed, imp + jnp.where(forced, FORCE_BONUS, 0.0), NEG)
        sel = jnp.zeros(shape, F32)
        for _ in range(n_sel):
            m = jnp.max(score, axis=0, keepdims=True)
            idx = jnp.min(jnp.where(score == m, blk, nbp), axis=0, keepdims=True)
            hit = blk == idx
            sel = jnp.where(hit & (m > 0.5 * NEG), 1.0, sel)
            score = jnp.where(hit, PICKED, score)
        sel_ref[0, g] = sel.astype(BF16)
    ocmp_ref[...] = jnp.concatenate(outs, axis=0).T


def _cmpsel_prompt(q, cmp, near, cfar, batch, seq):
    tq = min(TQ, seq)
    nq = seq // tq
    nbp = cmp.shape[1]
    n_sel = min(TOP_N, -(-seq // NSA_BLOCK))
    return pl.pallas_call(
        functools.partial(_cmpsel_prompt_body, tq=tq, nbp=nbp, n_sel=n_sel),
        grid=(batch, nq),
        in_specs=[pl.BlockSpec((tq, 512), lambda b, i: (b * nq + i, 0)),
                  pl.BlockSpec((1, nbp, 256), lambda b, i: (b, 0, 0)),
                  pl.BlockSpec((2, H_C, tq), lambda b, i: (0, 0, 0)),
                  pl.BlockSpec((H_C, 128), lambda b, i: (0, 0))],
        out_specs=[pl.BlockSpec((tq, 512), lambda b, i: (b * nq + i, 0)),
                   pl.BlockSpec((1, G_C, nbp, tq), lambda b, i: (b, 0, 0, i))],
        out_shape=[jax.ShapeDtypeStruct((batch * seq, 512), F32),
                   jax.ShapeDtypeStruct((batch, G_C, nbp, seq), BF16)],
        compiler_params=_cparams("arbitrary", "arbitrary"),
        name="nsa_cmpsel_prompt",
    )(q, cmp, near, cfar)


def _heads_t(q_t, h0, nh):
    return jnp.concatenate([q_t[DH * h:DH * (h + 1)] for h in range(h0, h0 + nh)], axis=1).astype(BF16)


def _fill_kv(src_ref, k_col, v_col, n_groups, k_scr, vt_scr, tq):
    for g in range(n_groups):
        k_scr[g] = src_ref[0, :, k_col + DH * g:k_col + DH * (g + 1)].astype(BF16)
    for kb in range(vt_scr.shape[0]):
        vt_scr[kb] = src_ref[0, kb * tq:(kb + 1) * tq, v_col:v_col + DH * n_groups].T.astype(BF16)


def _nsa_prompt_body(q_ref, nsa_ref, win_ref, sel_ref, bdt_ref, bst_ref, cfar_ref, oslc_ref, owin_ref,
                     ks_scr, vs_scr, kw_scr, vw_scr, m_ref, l_ref, acc_ref, *, tq, nbp):
    qi = pl.program_id(1)

    @pl.when(qi == 0)
    def _():
        _fill_kv(nsa_ref, 256, 384, G_C, ks_scr, vs_scr, tq)
        _fill_kv(win_ref, 0, 128, G_C, kw_scr, vw_scr, tq)

    q_t = (q_ref[...] * ATT_SCALE).T
    per_tile = tq // NSA_BLOCK
    n_back = WINDOW // tq
    shape = (tq, HPG_C * tq)
    key_i, qry_i = _iota(shape, 0), _iota(shape, 1) & (tq - 1)
    state = (m_ref, l_ref, acc_ref)
    slc_out, win_out = [], []
    for g in range(G_C):
        heads = range(HPG_C * g, HPG_C * (g + 1))
        q_g = _heads_t(q_t, HPG_C * g, HPG_C)
        far_row = jnp.concatenate([jnp.broadcast_to(cfar_ref[h:h + 1, 0:1], (1, tq)) for h in heads], axis=1)

        def tile(kb, mode, mask_t, k_scr, vt_scr, g=g, q_g=q_g, far_row=far_row, heads=heads):
            k = k_scr[g, pl.ds(pl.multiple_of(kb * tq, tq), tq), :]
            if mode == "far":
                bias = far_row
            else:
                tab = bdt_ref if mode == "diag" else bst_ref
                bias = jnp.concatenate([tab[h] for h in heads], axis=1)
            _tsoftmax_step(_dot(k, q_g) + bias, mask_t, vt_scr[kb, DH * g:DH * (g + 1), :], *state)

        sel_t = sel_ref[0, g]

        def slc_tile(kb, mode, tile=tile, sel_t=sel_t):
            expand = _iota((tq, nbp), 1) == kb * per_tile + jnp.right_shift(_iota((tq, nbp), 0), 6)
            m = _dot(jnp.where(expand, 1.0, 0.0).astype(BF16), sel_t) > 0.5
            if mode == "diag":
                m = m & (_iota((tq, tq), 0) <= _iota((tq, tq), 1))
            tile(kb, mode, jnp.concatenate([m] * HPG_C, axis=1), ks_scr, vs_scr)

        _softmax_init(*state)

        def far(kb, c, slc_tile=slc_tile):
            slc_tile(kb, "far")
            return c

        lax.fori_loop(0, jnp.maximum(qi - 1, 0), far, 0)

        @pl.when(qi >= 1)
        def _(slc_tile=slc_tile):
            slc_tile(qi - 1, "sub")

        slc_tile(qi, "diag")
        o_t = _tsoftmax_out(l_ref, acc_ref)
        slc_out += [o_t[:, j * tq:(j + 1) * tq] for j in range(HPG_C)]

        _softmax_init(*state)
        for d in range(n_back, 0, -1):
            mask_t = (key_i >= qry_i) if d == n_back else None

            @pl.when(qi >= d)
            def _(d=d, mask_t=mask_t, tile=tile):
                tile(qi - d, "sub" if d == 1 else "far", mask_t, kw_scr, vw_scr)

        tile(qi, "diag", key_i <= qry_i, kw_scr, vw_scr)
        o_t = _tsoftmax_out(l_ref, acc_ref)
        win_out += [o_t[:, j * tq:(j + 1) * tq] for j in range(HPG_C)]
    oslc_ref[...] = jnp.concatenate(slc_out, axis=0).T
    owin_ref[...] = jnp.concatenate(win_out, axis=0).T


def _nsa_prompt_attn(q, nsa, win, sel, bdt, bst, cfar, batch, seq):
    tq = min(TQ, seq)
    assert tq & (tq - 1) == 0 and seq % tq == 0 and WINDOW % tq == 0 and tq >= MAX_DISTANCE
    nq = seq // tq
    nbp = sel.shape[2]
    tok = pl.BlockSpec((tq, 512), lambda b, i: (b * nq + i, 0))
    seq_spec = lambda w: pl.BlockSpec((1, seq, w), lambda b, i: (b, 0, 0))
    full = lambda a: pl.BlockSpec(a.shape, lambda b, i: (0,) * a.ndim)
    r = HPG_C * tq
    kv_scratch = [pltpu.VMEM((G_C, seq, DH), BF16), pltpu.VMEM((nq, G_C * DH, tq), BF16)]
    return pl.pallas_call(
        functools.partial(_nsa_prompt_body, tq=tq, nbp=nbp),
        grid=(batch, nq),
        in_specs=[tok, seq_spec(NSA_ROW), seq_spec(WIN_ROW),
                  pl.BlockSpec((1, G_C, nbp, tq), lambda b, i: (b, 0, 0, i)), full(bdt), full(bst), full(cfar)],
        out_specs=[tok, tok],
        out_shape=[jax.ShapeDtypeStruct((batch * seq, 512), F32)] * 2,
        scratch_shapes=kv_scratch + kv_scratch + [pltpu.VMEM((1, r), F32), pltpu.VMEM((1, r), F32),
                                                  pltpu.VMEM((DH, r), F32)],
        compiler_params=_cparams("arbitrary", "arbitrary"),
        name="nsa_slc_win_prompt",
    )(q, nsa.reshape(batch, seq, NSA_ROW), win.reshape(batch, seq, WIN_ROW), sel, bdt, bst, cfar)


def _sb_prompt_body(q_ref, rows_ref, u_ref, o_ref, k_scr, vt_scr, carry_ref, acc_ref, *, tq):
    qi = pl.program_id(1)

    @pl.when(qi == 0)
    def _():
        _fill_kv(rows_ref, 0, 256, KV_D, k_scr, vt_scr, tq)

    q_t = (q_ref[...] * ATT_SCALE).T
    shape = (tq, HPK_D * tq)
    before = _iota(shape, 0) < (_iota(shape, 1) & (tq - 1))
    outs = []
    for g in range(KV_D):
        q_g = _heads_t(q_t, HPK_D * g, HPK_D)
        carry_ref[...] = jnp.zeros(carry_ref.shape, F32)
        acc_ref[...] = jnp.zeros(acc_ref.shape, F32)

        def step(kb, diag, g=g, q_g=q_g):
            k = k_scr[g, pl.ds(pl.multiple_of(kb * tq, tq), tq), :]
            z = _dot(k, q_g)
            lsz = _log_sigmoid(z)
            lf = lsz - z
            if diag:
                lf = jnp.where(before, lf, 0.0)
            hi = lf.astype(BF16)
            mid = (lf - hi.astype(F32)).astype(BF16)
            u = u_ref[...]
            between = _dot(u, hi) + _dot(u, mid)
            a = jnp.exp(lsz + between + carry_ref[...])
            if diag:
                a = jnp.where(before, a, 0.0)
            acc_ref[...] += _dot(vt_scr[kb, DH * g:DH * (g + 1), :], a.astype(BF16))
            carry_ref[...] += jnp.sum(lf, axis=0, keepdims=True)

        step(qi, True)

        def back(it, c, step=step):
            step(qi - 1 - it, False)
            return c

        lax.fori_loop(0, qi, back, 0)
        o_t = acc_ref[...]
        outs += [o_t[:, j * tq:(j + 1) * tq] for j in range(HPK_D)]
    o_ref[...] = jnp.concatenate(outs, axis=0).T


def _sb_prompt(q, rows, batch, seq):
    tq = min(TQ, seq)
    assert tq & (tq - 1) == 0 and seq % tq == 0
    nq = seq // tq
    later = np.arange(tq)[None, :] > np.arange(tq)[:, None]
    u = jnp.asarray(later, BF16)
    tok = pl.BlockSpec((tq, 512), lambda b, i: (b * nq + i, 0))
    r = HPK_D * tq
    return pl.pallas_call(
        functools.partial(_sb_prompt_body, tq=tq),
        grid=(batch, nq),
        in_specs=[tok, pl.BlockSpec((1, seq, SB_ROW), lambda b, i: (b, 0, 0)),
                  pl.BlockSpec((tq, tq), lambda b, i: (0, 0))],
        out_specs=tok,
        out_shape=jax.ShapeDtypeStruct((batch * seq, 512), F32),
        scratch_shapes=[pltpu.VMEM((KV_D, seq, DH), BF16), pltpu.VMEM((nq, KV_D * DH, tq), BF16),
                        pltpu.VMEM((1, r), F32), pltpu.VMEM((DH, r), F32)],
        compiler_params=_cparams("arbitrary", "arbitrary"),
        name="sb_prompt",
    )(q, rows.reshape(batch, seq, SB_ROW), u)


def _page_specs(block, pp, col_block, page_of):
    return [pl.BlockSpec(block, lambda b, s, pt, j=j: (page_of(b, s, j, pt), 0, col_block)) for j in range(pp)]


def _mla_decode_body(pt_ref, q_ref, knew_ref, wvb_ref, *rest, pp, valid):
    pages = rest[:pp]
    o_ref, m_ref, l_ref, acc_ref = rest[pp:]
    st = pl.program_id(1)
    r = SAMPLE_ROWS
    q = q_ref[0].astype(BF16)

    @pl.when(st == 0)
    def _():
        _softmax_init(m_ref, l_ref, acc_ref)

    k_t = jnp.concatenate([pages[j][0] for j in range(pp)], axis=1).astype(BF16)
    _softmax_step(_dot(q, k_t), None, k_t[0:KV_LORA], m_ref, l_ref, acc_ref, v_is_transposed=True)

    @pl.when(st == pl.num_programs(1) - 1)
    def _():
        kn = knew_ref[0].astype(BF16)
        shape = (H_B * r, r)
        col = _iota(shape, 1)
        mask = (col <= (_iota(shape, 0) & (r - 1))) & (col < valid)
        _softmax_step(_dot_nt(q, kn), mask, kn[:, 0:KV_LORA], m_ref, l_ref, acc_ref)
        o = _softmax_out(l_ref, acc_ref)
        olat = jnp.concatenate([o[r * h:r * (h + 1)] for h in range(H_B)], axis=1)
        o_ref[0] = _dot(olat.astype(BF16), wvb_ref[...])


def _mla_decode(page_table, q, knew, wvb, cache_t, valid):
    batch, n_pages = page_table.shape
    pp = min(2 * PAGES_PER_STEP, n_pages)
    assert n_pages % pp == 0
    r = SAMPLE_ROWS
    grid_spec = pltpu.PrefetchScalarGridSpec(
        num_scalar_prefetch=1,
        grid=(batch, n_pages // pp),
        in_specs=[pl.BlockSpec((1, H_B * r, MLA_ROW), lambda b, s, pt: (b, 0, 0)),
                  pl.BlockSpec((1, r, MLA_ROW), lambda b, s, pt: (b, 0, 0)),
                  pl.BlockSpec((1024, 512), lambda b, s, pt: (0, 0))]
        + _page_specs((1, MLA_ROW, PAGE), pp, 0, lambda b, s, j, pt: pt[b, s * pp + j]),
        out_specs=pl.BlockSpec((1, r, 512), lambda b, s, pt: (b, 0, 0)),
        scratch_shapes=[pltpu.VMEM((H_B * r, 1), F32), pltpu.VMEM((H_B * r, 1), F32),
                        pltpu.VMEM((H_B * r, KV_LORA), F32)])
    return pl.pallas_call(
        functools.partial(_mla_decode_body, pp=pp, valid=valid),
        grid_spec=grid_spec,
        out_shape=jax.ShapeDtypeStruct((batch, r, 512), F32),
        compiler_params=_cparams("arbitrary", "arbitrary"),
        name="mla_decode",
    )(page_table, q, knew, wvb, *([cache_t] * pp))


def _means_decode_body(pt_ref, *rest, pp):
    pages = rest[:pp]
    o_ref = rest[pp]
    o_ref[0] = _block_means(jnp.concatenate([pages[j][0] for j in range(pp)], axis=0))


def _means_decode(page_table, cache):
    batch, n_pages = page_table.shape
    pp = min(2 * PAGES_PER_STEP, n_pages)
    assert n_pages % pp == 0
    per_page = PAGE // NSA_BLOCK
    grid_spec = pltpu.PrefetchScalarGridSpec(
        num_scalar_prefetch=1,
        grid=(batch, n_pages // pp),
        in_specs=_page_specs((1, PAGE, 256), pp, 0, lambda b, s, j, pt: pt[b, s * pp + j]),
        out_specs=pl.BlockSpec((1, pp * per_page, 256), lambda b, s, pt: (b, s, 0)))
    return pl.pallas_call(
        functools.partial(_means_decode_body, pp=pp),
        grid_spec=grid_spec,
        out_shape=jax.ShapeDtypeStruct((batch, n_pages * per_page, 256), F32),
        compiler_params=_cparams("arbitrary", "arbitrary"),
        name="nsa_means_decode",
    )(page_table, *([cache] * pp))


def _cmpsel_decode_body(q_ref, cmp_ref, cb_ref, ocmp_ref, idx_ref, *, nb, n_past_sel):
    r = SAMPLE_ROWS
    blk = _iota((1, nb), 1)
    forced = (blk == 0) | (blk == nb - 1)
    lane = _iota((r, 128), 1)
    for g in range(G_C):
        ck = cmp_ref[0, :, DH * g:DH * (g + 1)].astype(BF16)
        cv = cmp_ref[0, :, 128 + DH * g:128 + DH * (g + 1)].astype(BF16)
        imp = jnp.zeros((r, nb), F32)
        for j in range(HPG_C):
            h = HPG_C * g + j
            o, p = _cmp_head(q_ref[0, :, DH * h:DH * (h + 1)], ck, cv, cb_ref[h], None)
            ocmp_ref[0, :, DH * h:DH * (h + 1)] = o
            imp = imp + p
        score = imp + jnp.where(forced, FORCE_BONUS, 0.0)
        picked = jnp.zeros((r, 128), jnp.int32)
        for slot, (idx, _) in enumerate(_top_blocks(score, blk, n_past_sel)):
            picked = jnp.where(lane == slot, idx, picked)
        idx_ref[0, g] = picked


def _cmpsel_decode(q, cmp, cb, n_past_sel):
    batch, nb, _ = cmp.shape
    r = SAMPLE_ROWS
    return pl.pallas_call(
        functools.partial(_cmpsel_decode_body, nb=nb, n_past_sel=n_past_sel),
        grid=(batch,),
        in_specs=[pl.BlockSpec((1, r, 512), lambda b: (b, 0, 0)),
                  pl.BlockSpec((1, nb, 256), lambda b: (b, 0, 0)),
                  pl.BlockSpec((H_C, r, nb), lambda b: (0, 0, 0))],
        out_specs=[pl.BlockSpec((1, r, 512), lambda b: (b, 0, 0)),
                   pl.BlockSpec((1, G_C, r, 128), lambda b: (b, 0, 0, 0))],
        out_shape=[jax.ShapeDtypeStruct((batch, r, 512), F32),
                   jax.ShapeDtypeStruct((batch, G_C, r, 128), jnp.int32)],
        compiler_params=_cparams("arbitrary"),
        name="nsa_cmpsel_decode",
    )(q, cmp, cb)


def _dist_bias(tab3_ref, d0, width):
    nd = tab3_ref.shape[1]
    dist = jnp.clip(d0 - _iota((nd, width), 1), 0, nd - 1)
    onehot = jnp.where(_iota((nd, width), 0) == dist, 1.0, 0.0).astype(BF16)
    b = _dot(tab3_ref[...], onehot)
    return b[0:8] + b[8:16] + b[16:24]


def _slc_decode_body(idx_ref, pt_ref, q_ref, new_ref, tab3_ref, *rest, n_slots, p_len, valid):
    blocks = rest[:G_C * n_slots]
    o_ref = rest[G_C * n_slots]
    b, t = pl.program_id(0), pl.program_id(1)
    r = SAMPLE_ROWS
    q = (q_ref[0, 0] * ATT_SCALE).astype(BF16)
    qpos = p_len + t
    head_row = _iota((H_C, 1), 0)
    out = jnp.zeros((H_C, DH), F32)
    for g in range(G_C):
        pieces = []
        for s in range(n_slots):
            n = idx_ref[((b * G_C + g) * r + t) * n_slots + s]
            kv = blocks[g * n_slots + s][0]
            k = kv[:, DH * g:DH * (g + 1)].astype(BF16)
            v = kv[:, 128 + DH * g:128 + DH * (g + 1)].astype(BF16)
            sc = _dot_nt(q, k) + _dist_bias(tab3_ref, qpos - n * NSA_BLOCK, NSA_BLOCK)
            pieces.append((sc, v))
        kn = new_ref[0, :, 256 + DH * g:256 + DH * (g + 1)].astype(BF16)
        vn = new_ref[0, :, 384 + DH * g:384 + DH * (g + 1)].astype(BF16)
        col = _iota((H_C, r), 1)
        cur_ok = (col <= t) & (col < valid)
        sc_new = jnp.where(cur_ok, _dot_nt(q, kn) + _dist_bias(tab3_ref, t, r), NEG)
        m = jnp.max(sc_new, axis=1, keepdims=True)
        for sc, _ in pieces:
            m = jnp.maximum(m, jnp.max(sc, axis=1, keepdims=True))
        p_new = jnp.where(cur_ok, jnp.exp(sc_new - m), 0.0)
        l = jnp.sum(p_new, axis=1, keepdims=True)
        acc = _dot(p_new.astype(BF16), vn)
        for sc, v in pieces:
            p = jnp.exp(sc - m)
            l = l + jnp.sum(p, axis=1, keepdims=True)
            acc = acc + _dot(p.astype(BF16), v)
        in_group = (head_row >= HPG_C * g) & (head_row < HPG_C * (g + 1))
        out = jnp.where(in_group, acc / l, out)
    o_ref[0, 0] = out


def _slc_decode(idx, page_table, q, new_rows, tab3, cache, n_tok, p_len):
    batch, n_pages = page_table.shape
    r = SAMPLE_ROWS
    n_slots = idx.shape[0] // (batch * G_C * r)
    per_page = PAGE // NSA_BLOCK

    def slot_spec(g, s):
        def index(b, t, idx_ref, pt):
            n = idx_ref[((b * G_C + g) * r + t) * n_slots + s]
            return (pt[b, n // per_page], n % per_page, 1)
        return pl.BlockSpec((1, NSA_BLOCK, 256), index)

    grid_spec = pltpu.PrefetchScalarGridSpec(
        num_scalar_prefetch=2,
        grid=(batch, n_tok),
        in_specs=[pl.BlockSpec((1, 1, H_C, DH), lambda b, t, i, pt: (b, t, 0, 0)),
                  pl.BlockSpec((1, r, NSA_ROW), lambda b, t, i, pt: (b, 0, 0)),
                  pl.BlockSpec(tab3.shape, lambda b, t, i, pt: (0, 0))]
        + [slot_spec(g, s) for g in range(G_C) for s in range(n_slots)],
        out_specs=pl.BlockSpec((1, 1, H_C, DH), lambda b, t, i, pt: (b, t, 0, 0)))
    return pl.pallas_call(
        functools.partial(_slc_decode_body, n_slots=n_slots, p_len=p_len, valid=n_tok),
        grid_spec=grid_spec,
        out_shape=jax.ShapeDtypeStruct((batch, n_tok, H_C, DH), F32),
        compiler_params=_cparams("arbitrary", "arbitrary"),
        name="nsa_slc_decode",
    )(idx, page_table, q, new_rows, tab3, *([cache] * (G_C * n_slots)))


def _win_decode_body(q_ref, state_ref, new_ref, wbs_ref, wbn_ref, o_ref, *, valid):
    r = SAMPLE_ROWS
    wb = state_ref.shape[1]
    for g in range(G_C):
        q4 = (jnp.concatenate([q_ref[0, :, DH * h:DH * (h + 1)] for h in range(HPG_C * g, HPG_C * (g + 1))],
                              axis=0) * ATT_SCALE).astype(BF16)
        ks = state_ref[0, :, DH * g:DH * (g + 1)].astype(BF16)
        vs = state_ref[0, :, 128 + DH * g:128 + DH * (g + 1)].astype(BF16)
        kn = new_ref[0, :, DH * g:DH * (g + 1)].astype(BF16)
        vn = new_ref[0, :, 128 + DH * g:128 + DH * (g + 1)].astype(BF16)
        tok = _iota((HPG_C * r, 1), 0) & (r - 1)
        ok_s = _iota((HPG_C * r, wb), 1) >= tok + (wb - WINDOW)
        col_n = _iota((HPG_C * r, r), 1)
        ok_n = (col_n <= tok) & (col_n < valid)
        bias_s = jnp.concatenate([wbs_ref[h] for h in range(HPG_C * g, HPG_C * (g + 1))], axis=0)
        bias_n = jnp.concatenate([wbn_ref[h] for h in range(HPG_C * g, HPG_C * (g + 1))], axis=0)
        s_s = jnp.where(ok_s, _dot_nt(q4, ks) + bias_s, NEG)
        s_n = jnp.where(ok_n, _dot_nt(q4, kn) + bias_n, NEG)
        m = jnp.maximum(jnp.max(s_s, axis=1, keepdims=True), jnp.max(s_n, axis=1, keepdims=True))
        p_s = jnp.where(ok_s, jnp.exp(s_s - m), 0.0)
        p_n = jnp.where(ok_n, jnp.exp(s_n - m), 0.0)
        l = jnp.sum(p_s, axis=1, keepdims=True) + jnp.sum(p_n, axis=1, keepdims=True)
        o = (_dot(p_s.astype(BF16), vs) + _dot(p_n.astype(BF16), vn)) / l
        for j in range(HPG_C):
            h = HPG_C * g + j
            o_ref[0, :, DH * h:DH * (h + 1)] = o[r * j:r * (j + 1)]


def _win_decode(q, state, new_rows, wbs, wbn, valid):
    batch, wb, _ = state.shape
    r = SAMPLE_ROWS
    return pl.pallas_call(
        functools.partial(_win_decode_body, valid=valid),
        grid=(batch,),
        in_specs=[pl.BlockSpec((1, r, 512), lambda b: (b, 0, 0)),
                  pl.BlockSpec((1, wb, WIN_ROW), lambda b: (b, 0, 0)),
                  pl.BlockSpec((1, r, WIN_ROW), lambda b: (b, 0, 0)),
                  pl.BlockSpec((H_C, r, wb), lambda b: (0, 0, 0)),
                  pl.BlockSpec((H_C, r, r), lambda b: (0, 0, 0))],
        out_specs=pl.BlockSpec((1, r, 512), lambda b: (b, 0, 0)),
        out_shape=jax.ShapeDtypeStruct((batch, r, 512), F32),
        compiler_params=_cparams("arbitrary"),
        name="nsa_win_decode",
    )(q, state, new_rows, wbs, wbn)


def _suffix_sum(x):
    n = x.shape[0]
    row = _iota(x.shape, 0)
    sh = 1
    while sh < n:
        x = x + jnp.where(row < n - sh, pltpu.roll(x, n - sh, axis=0), 0.0)
        sh *= 2
    return x


def _sb_decode_body(pt_ref, wq_ref, new_ref, *rest, pp, valid):
    pages = rest[:pp]
    o_ref, carry_ref, acc_ref, a_ref, v_ref = rest[pp:]
    st = pl.program_id(1)
    r = SAMPLE_ROWS
    half = 64
    wq = wq_ref[0].astype(BF16)
    lane = _iota((1, 128), 1)

    @pl.when(st == 0)
    def _():
        kn = new_ref[0]
        kk = kn[:, 0:256].astype(BF16)
        z = _dot(jnp.concatenate([kk, kk], axis=1), wq)
        lsz = _log_sigmoid(z)
        row = _iota((r, 128), 0)
        mask = (row < (_iota((r, 128), 1) & (r - 1))) & (row < valid)
        lf = jnp.where(mask, lsz - z, 0.0)
        a = jnp.where(mask, jnp.exp(lsz + _suffix_sum(lf) - lf), 0.0)
        pad = lambda x: jnp.concatenate([x, jnp.zeros((PAGE - r, x.shape[1]), x.dtype)], axis=0)
        vn = jnp.concatenate([kn[:, 256:512], jnp.zeros((r, 256), F32)], axis=1)
        acc_ref[...] = _dot_tn(pad(a).astype(BF16), pad(vn).astype(BF16))
        carry_ref[...] = jnp.sum(lf, axis=0, keepdims=True)

    carry = carry_ref[...]
    for i in range(pp // 2 - 1, -1, -1):
        early, late = pages[2 * i][0], pages[2 * i + 1][0]
        z = _dot(jnp.concatenate([early[:, 0:256], late[:, 0:256]], axis=1).astype(BF16), wq)
        lsz = _log_sigmoid(z)
        lf = lsz - z
        suf = _suffix_sum(lf)
        tot = suf[0:1]
        swapped = pltpu.roll(jnp.broadcast_to(tot, (8, 128)), half, axis=1)[0:1]
        between = (suf - lf) + (carry + jnp.where(lane < half, swapped, 0.0))
        a_ref[PAGE * i:PAGE * (i + 1), :] = jnp.exp(lsz + between).astype(BF16)
        v_ref[PAGE * i:PAGE * (i + 1), :] = jnp.concatenate([early[:, 256:512], late[:, 256:512]],
                                                            axis=1).astype(BF16)
        carry = carry + tot + swapped
    carry_ref[...] = carry
    acc_ref[...] += _dot_tn(a_ref[...], v_ref[...])

    @pl.when(st == pl.num_programs(1) - 1)
    def _():
        o_ref[0] = acc_ref[...]


def _sb_decode(page_table, wq, new_rows, cache, valid):
    batch, n_pages = page_table.shape
    pp = min(PAGES_PER_STEP, n_pages)
    assert n_pages % pp == 0 and pp % 2 == 0
    r = SAMPLE_ROWS
    grid_spec = pltpu.PrefetchScalarGridSpec(
        num_scalar_prefetch=1,
        grid=(batch, n_pages // pp),
        in_specs=[pl.BlockSpec((1, 512, 128), lambda b, s, pt: (b, 0, 0)),
                  pl.BlockSpec((1, r, SB_ROW), lambda b, s, pt: (b, 0, 0))]
        + _page_specs((1, PAGE, SB_ROW), pp, 0, lambda b, s, j, pt: pt[b, n_pages - (s + 1) * pp + j]),
        out_specs=pl.BlockSpec((1, 128, 512), lambda b, s, pt: (b, 0, 0)),
        scratch_shapes=[pltpu.VMEM((1, 128), F32), pltpu.VMEM((128, 512), F32),
                        pltpu.VMEM((pp // 2 * PAGE, 128), BF16), pltpu.VMEM((pp // 2 * PAGE, 512), BF16)])
    return pl.pallas_call(
        functools.partial(_sb_decode_body, pp=pp, valid=valid),
        grid_spec=grid_spec,
        out_shape=jax.ShapeDtypeStruct((batch, 128, 512), F32),
        compiler_params=_cparams("arbitrary", "arbitrary"),
        name="sb_decode",
    )(page_table, wq, new_rows, *([cache] * pp))


def _rope_tables(pos):
    half = ROPE_D // 2
    inv = ROPE_BASE ** (-jnp.arange(half, dtype=F32) / half)
    ang = pos.astype(F32)[:, None] * inv[None, :]
    cos, sin = jnp.cos(ang), jnp.sin(ang)
    return jnp.tile(cos, (1, 8)), jnp.tile(sin, (1, 8))


def _rot_cols(w):
    half = ROPE_D // 2
    return jnp.concatenate([-w[..., half:], w[..., :half]], axis=-1)


def _prep_even(w_in, w_qb, w_kb, w_vb):
    zkr = w_in[:, 2432:2464]
    w_e = jnp.concatenate([w_in[:, :2432], jnp.tile(zkr, (1, 4)), jnp.tile(_rot_cols(zkr), (1, 4))], axis=1)
    wqb = w_qb.reshape(Q_LORA, H_B, NOPE + ROPE_D)
    wr = wqb[:, :, NOPE:]
    w_q = jnp.concatenate([wqb[:, :, :NOPE].reshape(Q_LORA, 512), wr.reshape(Q_LORA, 256),
                           _rot_cols(wr).reshape(Q_LORA, 256)], axis=1)
    eye = jnp.eye(H_B, dtype=F32)
    wkb = jnp.einsum('hcn,hg->hngc', w_kb, eye).reshape(H_B * NOPE, H_B * KV_LORA)
    wvb = jnp.einsum('hcd,hg->hcgd', w_vb, eye).reshape(H_B * KV_LORA, H_B * VD_B)
    return w_e.astype(BF16), w_q.astype(BF16), wkb.astype(BF16), wvb.astype(BF16)


def _prep_odd(w_in):
    w = jnp.concatenate([w_in[:, :1280], w_in[:, 1304:2328], w_in[:, 1280:1304],
                         jnp.zeros((D_MODEL, ODD_COLS - 2328), w_in.dtype)], axis=1)
    return w.astype(BF16)


def _gate_expand():
    e = np.zeros((128, 3 * 512), np.float32)
    for h in range(H_C):
        for j in range(3):
            e[3 * h + j, 512 * j + DH * h:512 * j + DH * (h + 1)] = 1.0
    return jnp.asarray(e, BF16)


def _hgrn_params(hgrn_lb, norm_g, lj):
    lb_all = jax.nn.softmax(hgrn_lb.astype(F32), axis=0)
    lb = (jnp.cumsum(lb_all, axis=0) - lb_all[0])[lj]
    lb_pos = lb > 0
    rows = [jnp.log(jnp.where(lb_pos, lb, 1.0)), jnp.log1p(-lb), lb_pos.astype(F32), 1.0 - lb, norm_g[lj]]
    return jnp.concatenate([jnp.stack(rows), jnp.zeros((3, 512), F32)], axis=0)


def _t5_bucket(dist):
    exact = NUM_BUCKETS // 2
    d = jnp.maximum(dist, 0)
    large = exact + (jnp.log(jnp.maximum(d, 1).astype(F32) / exact)
                     / math.log(MAX_DISTANCE / exact) * (NUM_BUCKETS - exact)).astype(jnp.int32)
    return jnp.where(d < exact, d, jnp.minimum(large, NUM_BUCKETS - 1))


N_DIST = 256


def _toeplitz_t(tab, offset, n):
    i = jnp.arange(n, dtype=jnp.int32)[None, :]
    j = jnp.arange(n, dtype=jnp.int32)[:, None]
    idx = jnp.clip(offset + i - j, 0, N_DIST - 1).reshape(1, n * n)
    onehot = (idx == jnp.arange(N_DIST, dtype=jnp.int32)[:, None]).astype(F32)
    return jnp.dot(tab, onehot, precision=lax.Precision.HIGHEST).reshape(H_C, n, n)


def _decode_bias_rows(tab, base, n_keys, r):
    far = jnp.broadcast_to(tab[:, N_DIST - 1:], (H_C, n_keys + r))
    rev = jnp.concatenate([far, tab[:, ::-1], jnp.zeros((H_C, n_keys + r), F32)], axis=1)
    rows = []
    for t in range(r):
        start = n_keys + r + N_DIST - 1 - (base + t)
        rows.append(rev[:, start:start + n_keys])
    return jnp.stack(rows, axis=1)


def _pad_rows(a, batch, n_tok):
    a = a.reshape(batch, n_tok, -1)
    return jnp.pad(a, ((0, 0), (0, SAMPLE_ROWS - n_tok), (0, 0)))


def _last_rows(rows, n):
    t = rows.shape[1]
    if t < n:
        rows = jnp.pad(rows, ((0, 0), (n - t, 0), (0, 0)))
    return rows[:, rows.shape[1] - n:]


def _run_prompt(x, mod, w, tabs, batch, seq, win_len):
    assert seq % HGRN_CHUNK == 0 and (batch * seq) % _tile_rows(batch * seq) == 0
    tq = min(TQ, seq)
    pos = np.arange(seq)
    cos, sin = _rope_tables(jnp.asarray(pos, jnp.int32))
    tab = tabs['tab']
    bdt, bst = _toeplitz_t(tab, 0, tq), _toeplitz_t(tab, tq, tq)
    nblk = seq // NSA_BLOCK
    nbp = -(-nblk // 128) * 128
    reps = tq // NSA_BLOCK
    near = jnp.stack([jnp.tile(tab[:, 1:NSA_BLOCK + 1], (1, reps)),
                      jnp.tile(tab[:, NSA_BLOCK + 1:2 * NSA_BLOCK + 1], (1, reps))])
    states = []
    for l in range(DEPTH):
        lj = l // 2
        m = mod[l]
        x = _ffn(x, m, w['norm_g'][l], w['wg'][l, 0], w['wu'][l, 0], w['wd'][l, 0], 0, seq)
        if l % 2 == 0:
            w_e, w_q, wkb, wvb = w['even'][lj]
            zh, kfull, rows, qlat, qrope = _inproj_even(
                x, m, w['norm_g'][l], w_e, cos, sin, w['gq'][lj], w['gkv'][lj], w_q, wkb, seq)
            tb = min(TM_DENSE, seq)
            o_a, s_new = _hgrn(zh, w['hgrn_par'][lj], batch, seq, tb, HGRN_CHUNK, HGRN_SUB, HGRN_CHUNK)
            o_b = _mla_prompt(qlat, qrope, kfull, wvb.T, batch, seq)
            x = _outproj(x, m, [o_a, o_b], [], w['wout_even'][lj], seq, odd=False)
            states += [rows.reshape(batch, seq, MLA_ROW), s_new]
        else:
            qn, nsa, win, qs, sb, zg = _inproj_odd(x, m, w['norm_g'][l], w['odd'][lj], seq)
            cmp = _means_prompt(nsa).reshape(batch, nblk, 256)
            cmp = jnp.pad(cmp, ((0, 0), (0, nbp - nblk), (0, 0)))
            o_cmp, sel = _cmpsel_prompt(qn, cmp, near, tabs['cfar'], batch, seq)
            o_slc, o_win = _nsa_prompt_attn(qn, nsa, win, sel, bdt, bst, tabs['cfar'], batch, seq)
            o_sb = _sb_prompt(qs, sb, batch, seq)
            x = _outproj(x, m, [o_cmp, o_slc, o_win, zg, o_sb], [w['gate_e']], w['wout_odd'][lj], seq, odd=True)
            states += [nsa.reshape(batch, seq, NSA_ROW), _last_rows(win.reshape(batch, seq, WIN_ROW), win_len),
                       sb.reshape(batch, seq, SB_ROW)]
        x = _ffn(x, m, w['norm_g'][l], w['wg'][l, 1], w['wu'][l, 1], w['wd'][l, 1], 2, seq,
                 final_g=w['final_g'] if l == DEPTH - 1 else None)
    return x.reshape(batch, seq, D_MODEL), states


def _run_sample(x, mod, w, tabs, batch, n_tok, page_table, pasts):
    n = batch * n_tok
    n_pages = page_table.shape[1]
    p_len = n_pages * PAGE
    r = SAMPLE_ROWS
    assert n_tok <= r and n_tok <= NSA_BLOCK and n == _tile_rows(n)
    pos = p_len + np.arange(r)
    cos, sin = _rope_tables(jnp.asarray(np.tile(pos[:n_tok], batch), jnp.int32))
    tab = tabs['tab']
    nb = p_len // NSA_BLOCK
    n_past_sel = min(TOP_N, nb + 1) - 1
    assert nb >= 2 and p_len % NSA_BLOCK == 0
    cb = jnp.concatenate([jnp.broadcast_to(tab[:, None, N_DIST - 1:], (H_C, r, nb - 2)),
                          tab[:, NSA_BLOCK + 1:NSA_BLOCK + 1 + r, None], tab[:, 1:1 + r, None]], axis=2)
    states = []
    for l in range(DEPTH):
        lj = l // 2
        m = mod[l]
        x = _ffn(x, m, w['norm_g'][l], w['wg'][l, 0], w['wu'][l, 0], w['wd'][l, 0], 0, n)
        if l % 2 == 0:
            cache_mla, state_hgrn = pasts[l]
            w_e, w_q, wkb, wvb = w['even'][lj]
            zh, kfull, rows, qlat, qrope = _inproj_even(
                x, m, w['norm_g'][l], w_e, cos, sin, w['gq'][lj], w['gkv'][lj], w_q, wkb, n)
            zh8 = _pad_rows(zh, batch, n_tok).reshape(batch * r, 2048)
            o_a8, s_new = _hgrn(zh8, w['hgrn_par'][lj], batch, r, r, r, r, n_tok, s0=state_hgrn)
            o_a = o_a8.reshape(batch, r, 512)[:, :n_tok].reshape(n, 512)
            qf = jnp.concatenate([qlat.reshape(batch, n_tok, H_B, KV_LORA),
                                  qrope.reshape(batch, n_tok, H_B, ROPE_D)], axis=-1)
            qf = jnp.pad(qf, ((0, 0), (0, r - n_tok), (0, 0), (0, 0)))
            qf = jnp.swapaxes(qf, 1, 2).reshape(batch, H_B * r, MLA_ROW)
            o_b8 = _mla_decode(page_table, qf, _pad_rows(rows, batch, n_tok), wvb, jnp.swapaxes(cache_mla, 1, 2),
                               n_tok)
            o_b = o_b8[:, :n_tok].reshape(n, 512)
            x = _outproj(x, m, [o_a, o_b], [], w['wout_even'][lj], n, odd=False)
            states += [rows.reshape(batch, n_tok, MLA_ROW), s_new]
        else:
            cache_nsa, state_win, cache_sb = pasts[l]
            wb = state_win.shape[1]
            assert p_len >= wb and wb <= WINDOW
            qn, nsa, win, qs, sb, zg = _inproj_odd(x, m, w['norm_g'][l], w['odd'][lj], n)
            qn8 = _pad_rows(qn, batch, n_tok)
            nsa8 = _pad_rows(nsa, batch, n_tok)
            cmp = _means_decode(page_table, cache_nsa)
            o_cmp8, picked = _cmpsel_decode(qn8, cmp, cb, n_past_sel)
            idx = picked[..., :n_past_sel].reshape(-1)
            o_slc = _slc_decode(idx, page_table, qn.reshape(batch, n_tok, H_C, DH), nsa8, tabs['tab3'],
                                cache_nsa, n_tok, p_len)
            wbs = _decode_bias_rows(tab, wb, wb, r)
            wbn = _decode_bias_rows(tab, 0, r, r)
            o_win8 = _win_decode(qn8, state_win, _pad_rows(win, batch, n_tok), wbs, wbn, n_tok)
            q5 = _pad_rows(qs * ATT_SCALE, batch, n_tok).reshape(batch, r, KV_D, HPK_D, DH)
            q5 = jnp.transpose(q5, (0, 2, 4, 3, 1)).reshape(batch, KV_D, DH, HPK_D * r)
            wq = jnp.einsum('bgdc,gh->bgdhc', q5, jnp.eye(KV_D, dtype=F32)).reshape(batch, KV_D * DH, KV_D * HPK_D * r)
            nc = KV_D * HPK_D * r
            zc = jnp.zeros_like(wq)
            wq = jnp.concatenate([jnp.concatenate([wq, zc], axis=2), jnp.concatenate([zc, wq], axis=2)], axis=1)
            o_raw = _sb_decode(page_table, wq, _pad_rows(sb, batch, n_tok), cache_sb, n_tok)
            o6 = (o_raw[:, :nc, :256] + o_raw[:, nc:, 256:]).reshape(batch, KV_D, HPK_D, r, KV_D, DH)
            o_sb = jnp.stack([o6[:, g, :, :, g] for g in range(KV_D)], axis=1)
            o_sb = jnp.transpose(o_sb, (0, 3, 1, 2, 4))[:, :n_tok].reshape(n, 512)
            take = lambda a: a[:, :n_tok].reshape(n, 512)
            x = _outproj(x, m, [take(o_cmp8), o_slc.reshape(n, 512), take(o_win8), zg, o_sb], [w['gate_e']],
                         w['wout_odd'][lj], n, odd=True)
            new_win = jnp.concatenate([state_win, win.reshape(batch, n_tok, WIN_ROW)], axis=1)[:, n_tok:]
            states += [nsa.reshape(batch, n_tok, NSA_ROW), new_win, sb.reshape(batch, n_tok, SB_ROW)]
        x = _ffn(x, m, w['norm_g'][l], w['wg'][l, 1], w['wu'][l, 1], w['wd'][l, 1], 2, n,
                 final_g=w['final_g'] if l == DEPTH - 1 else None)
    return x.reshape(batch, n_tok, D_MODEL), states


def kernel(x_prompt, x_sample, cache_mla_l0, state_hgrn_l0, cache_nsa_l1, state_win_l1, cache_sb_l1, cache_mla_l2, state_hgrn_l2, cache_nsa_l3, state_win_l3, cache_sb_l3, page_table, c_prompt, c_sample, w_ada, b_ada, norm_g, ffn_w_gate, ffn_w_up, ffn_w_down, w_in_even, w_out_even, hgrn_lb, hgrn_norm_g, mla_q_norm_g, mla_kv_norm_g, mla_w_qb, mla_w_kb, mla_w_vb, w_in_odd, w_out_odd, rel_bias, final_norm_g):
    bp, seq, _ = x_prompt.shape
    bs, n_tok, _ = x_sample.shape
    n_even = w_in_even.shape[0]
    n_odd = w_in_odd.shape[0]
    w = dict(
        norm_g=norm_g, final_g=final_norm_g,
        wg=ffn_w_gate.astype(BF16), wu=ffn_w_up.astype(BF16), wd=ffn_w_down.astype(BF16),
        even=[_prep_even(w_in_even[j], mla_w_qb[j], mla_w_kb[j], mla_w_vb[j]) for j in range(n_even)],
        odd=[_prep_odd(w_in_odd[j]) for j in range(n_odd)],
        wout_even=w_out_even.astype(BF16), wout_odd=w_out_odd.astype(BF16),
        gq=mla_q_norm_g.reshape(n_even, 1, Q_LORA), gkv=mla_kv_norm_g.reshape(n_even, 1, KV_LORA),
        hgrn_par=[_hgrn_params(hgrn_lb, hgrn_norm_g, j) for j in range(n_even)],
        gate_e=_gate_expand())
    tab = rel_bias[_t5_bucket(jnp.arange(N_DIST, dtype=jnp.int32))].T.astype(F32)
    cfar = jnp.concatenate([tab[:, N_DIST - 1:], tab[:, 0:1], jnp.zeros((H_C, 126), F32)], axis=1)
    tabs = dict(tab=tab, cfar=cfar, tab3=jnp.concatenate(_split3(tab), axis=0))
    mod = _ada_mod(jnp.concatenate([c_prompt, c_sample], axis=0), w_ada, b_ada)
    mod = mod.reshape(DEPTH, bp + bs, N_MOD, D_MODEL)
    mod_p = mod[:, :bp, :, None, :]
    mod_s = jnp.transpose(jnp.repeat(mod[:, bp:], n_tok, axis=1), (0, 2, 1, 3))[:, None]
    win_len = state_win_l1.shape[1]
    y_p, st_p = _run_prompt(x_prompt.reshape(bp * seq, D_MODEL), mod_p, w, tabs, bp, seq, win_len)
    pasts = [(cache_mla_l0, state_hgrn_l0), (cache_nsa_l1, state_win_l1, cache_sb_l1),
             (cache_mla_l2, state_hgrn_l2), (cache_nsa_l3, state_win_l3, cache_sb_l3)]
    y_s, st_s = _run_sample(x_sample.reshape(bs * n_tok, D_MODEL), mod_s, w, tabs, bs, n_tok, page_table, pasts)
    out = [y_p, y_s]
    for a, b in zip(st_p, st_s):
        out += [a, b]
    return tuple(out)
```

```python
import functools
import math

import numpy as np
import jax
import jax.numpy as jnp
from jax import lax
from jax.experimental import pallas as pl
from jax.experimental.pallas import tpu as pltpu

F32 = jnp.float32
BF16 = jnp.bfloat16

D_MODEL = 1024
DEPTH = 4
PAGE = 128
NORM_EPS = 1e-6
NEG = -1e30
PICKED = -3e38
N_MOD = 9
D_FF = 2816

H_A = 4
DK_A = 128
HGRN_CHUNK = 64
HGRN_SUB = 16

H_B = 8
Q_LORA = 256
KV_LORA = 128
NOPE = 64
ROPE_D = 32
VD_B = 64
ROPE_BASE = 10000.0
MLA_ROW = KV_LORA + ROPE_D
MLA_SCALE = (NOPE + ROPE_D) ** -0.5

DH = 64
H_C = 8
G_C = 2
HPG_C = 4
NSA_BLOCK = 64
TOP_N = 8
WINDOW = 512
FORCE_BONUS = 100.0
NSA_ROW = 512
WIN_ROW = 256
H_D = 8
KV_D = 4
HPK_D = 2
SB_ROW = 512
ATT_SCALE = DH ** -0.5
NUM_BUCKETS = 32
MAX_DISTANCE = 128

V7X_VMEM_BYTES = 64 * 1024 * 1024
VMEM_LIMIT = V7X_VMEM_BYTES - 8 * 1024 * 1024

TM_DENSE = 512
FF_CHUNK = 1408
TQ = 256
SAMPLE_ROWS = 8
PAGES_PER_STEP = 16


def _cparams(*sem):
    return pltpu.CompilerParams(dimension_semantics=sem, vmem_limit_bytes=VMEM_LIMIT)


def _iota(shape, dim):
    return lax.broadcasted_iota(jnp.int32, shape, dim)


def _rms(x):
    return x * lax.rsqrt(jnp.mean(x * x, axis=-1, keepdims=True) + NORM_EPS)


def _silu(x):
    return x * jax.nn.sigmoid(x)


def _softplus_neg_abs(x):
    return jnp.log(1.0 + jnp.exp(-jnp.abs(x)))


def _log_sigmoid(x):
    return jnp.minimum(x, 0.0) - _softplus_neg_abs(x)


def _dot(a, b):
    return jnp.dot(a, b, preferred_element_type=F32)


def _dot_nt(a, b):
    return lax.dot_general(a, b, (((1,), (1,)), ((), ())), preferred_element_type=F32)


def _dot_tn(a, b):
    return lax.dot_general(a, b, (((0,), (0,)), ((), ())), preferred_element_type=F32)


def _split3(x):
    hi = x.astype(BF16)
    r = x - hi.astype(F32)
    mid = r.astype(BF16)
    lo = (r - mid.astype(F32)).astype(BF16)
    return hi, mid, lo


def _ada_body(c_ref, w_ref, b_ref, o_ref):
    h = _silu(c_ref[...]).astype(BF16)
    o_ref[0] = _dot(h, w_ref[0].astype(BF16)) + b_ref[0]


def _ada_mod(c_all, w_ada, b_ada):
    nc = c_all.shape[0]
    depth, _, ncol = w_ada.shape
    tn = 1024
    return pl.pallas_call(
        _ada_body,
        grid=(depth, ncol // tn),
        in_specs=[pl.BlockSpec((nc, D_MODEL), lambda l, j: (0, 0)),
                  pl.BlockSpec((1, D_MODEL, tn), lambda l, j: (l, 0, j)),
                  pl.BlockSpec((1, 1, tn), lambda l, j: (l, 0, j))],
        out_specs=pl.BlockSpec((1, nc, tn), lambda l, j: (l, 0, j)),
        out_shape=jax.ShapeDtypeStruct((depth, nc, ncol), F32),
        compiler_params=_cparams("arbitrary", "arbitrary"),
        name="ada_mod",
    )(c_all, w_ada, b_ada.reshape(depth, 1, ncol))


def _tile_rows(n):
    return min(TM_DENSE, n)


def _mod_spec(mod, tm, seq_len):
    s, _, r, _ = mod.shape
    if r == 1:
        tiles_per_seq = seq_len // tm
        return pl.BlockSpec((1, N_MOD, 1, D_MODEL), lambda i: (i // tiles_per_seq, 0, 0, 0))
    return pl.BlockSpec((1, N_MOD, tm, D_MODEL), lambda i: (0, 0, i, 0))


def _prenorm(x, mod_ref, g_ref, sub):
    shift = mod_ref[0, 3 * sub]
    scale = mod_ref[0, 3 * sub + 1]
    return _rms(x) * g_ref[sub:sub + 1, :] * (1.0 + scale) + shift


def _const_spec(shape):
    nd = len(shape)
    return pl.BlockSpec(shape, lambda i: (0,) * nd, pipeline_mode=pl.Buffered(1))


def _ffn_body(x_ref, mod_ref, g_ref, wg_ref, wu_ref, wd_ref, *rest, sub, final):
    o_ref = rest[-1]
    x = x_ref[...]
    h = _prenorm(x, mod_ref, g_ref, sub).astype(BF16)
    acc = None
    for c0 in range(0, D_FF, FF_CHUNK):
        a = _dot(h, wg_ref[:, c0:c0 + FF_CHUNK])
        u = _dot(h, wu_ref[:, c0:c0 + FF_CHUNK])
        t = (_silu(a) * u).astype(BF16)
        part = _dot(t, wd_ref[c0:c0 + FF_CHUNK, :])
        acc = part if acc is None else acc + part
    y = x + 0.5 * mod_ref[0, 3 * sub + 2] * acc
    if final:
        y = _rms(y) * rest[0][...]
    o_ref[...] = y


def _ffn(x, mod, norm_g, wg, wu, wd, layer, which, sub, seq_len, final_g=None):
    n = x.shape[0]
    tm = _tile_rows(n)
    final = final_g is not None
    pick = lambda r, c: pl.BlockSpec((None, None, r, c), lambda i: (layer, which, 0, 0),
                                     pipeline_mode=pl.Buffered(1))
    in_specs = [pl.BlockSpec((tm, D_MODEL), lambda i: (i, 0)),
                _mod_spec(mod, tm, seq_len),
                _const_spec((3, D_MODEL)),
                pick(D_MODEL, D_FF), pick(D_MODEL, D_FF), pick(D_FF, D_MODEL)]
    args = [x, mod, norm_g, wg, wu, wd]
    if final:
        in_specs.append(_const_spec((1, D_MODEL)))
        args.append(final_g.reshape(1, D_MODEL))
    return pl.pallas_call(
        functools.partial(_ffn_body, sub=sub, final=final),
        grid=(n // tm,),
        in_specs=in_specs,
        out_specs=pl.BlockSpec((tm, D_MODEL), lambda i: (i, 0)),
        out_shape=jax.ShapeDtypeStruct((n, D_MODEL), F32),
        compiler_params=_cparams("arbitrary"),
        name="ffn",
    )(*args)


EVEN_COLS = 2688
ODD_COLS = 2432


def _inproj_even_body(x_ref, mod_ref, g_ref, w_ref, cos_ref, sin_ref, gq_ref, gkv_ref, wq_ref, wkb_ref,
                      zh_ref, kfull_ref, rows_ref, qlat_ref, qrope_ref):
    h = _prenorm(x_ref[...], mod_ref, g_ref, 1).astype(BF16)
    zh_ref[...] = _dot(h, w_ref[:, 0:2048])
    z = _dot(h, w_ref[:, 2048:EVEN_COLS])
    cos = cos_ref[...]
    sin = sin_ref[...]
    ckv = _rms(z[:, 256:384]) * gkv_ref[...]
    krope = z[:, 384:512] * cos + z[:, 512:640] * sin
    kfull_ref[:, 0:128] = ckv
    kfull_ref[:, 128:256] = krope
    rows_ref[:, 0:128] = ckv
    rows_ref[:, 128:MLA_ROW] = krope[:, 0:ROPE_D]
    qa = (_rms(z[:, 0:256]) * gq_ref[...]).astype(BF16)
    qz = _dot(qa, wq_ref[...])
    qlat_ref[...] = _dot(qz[:, 0:512].astype(BF16), wkb_ref[...]) * MLA_SCALE
    cos2 = jnp.concatenate([cos, cos], axis=1)
    sin2 = jnp.concatenate([sin, sin], axis=1)
    qrope_ref[...] = (qz[:, 512:768] * cos2 + qz[:, 768:1024] * sin2) * MLA_SCALE


def _inproj_even(x, mod, norm_g, w, cos, sin, gq, gkv, wq, wkb, seq_len):
    n = x.shape[0]
    tm = _tile_rows(n)
    tab_tiles = cos.shape[0] // tm
    tok = lambda c: pl.BlockSpec((tm, c), lambda i: (i, 0))
    tab = pl.BlockSpec((tm, 128), lambda i: (i % tab_tiles, 0))
    return pl.pallas_call(
        _inproj_even_body,
        grid=(n // tm,),
        in_specs=[tok(D_MODEL), _mod_spec(mod, tm, seq_len), _const_spec((3, D_MODEL)),
                  _const_spec((D_MODEL, EVEN_COLS)), tab, tab,
                  _const_spec((1, Q_LORA)), _const_spec((1, KV_LORA)),
                  _const_spec((Q_LORA, 1024)), _const_spec((512, 1024))],
        out_specs=[tok(2048), tok(256), tok(MLA_ROW), tok(1024), tok(256)],
        out_shape=[jax.ShapeDtypeStruct((n, c), F32) for c in (2048, 256, MLA_ROW, 1024, 256)],
        compiler_params=_cparams("arbitrary"),
        name="inproj_even",
    )(x, mod, norm_g, w, cos, sin, gq, gkv, wq, wkb)


def _inproj_odd_body(x_ref, mod_ref, g_ref, w_ref, qn_ref, nsa_ref, win_ref, qs_ref, sb_ref, zg_ref):
    h = _prenorm(x_ref[...], mod_ref, g_ref, 1).astype(BF16)
    qn_ref[...] = _dot(h, w_ref[:, 0:512])
    nsa_ref[...] = _dot(h, w_ref[:, 512:1024])
    win_ref[...] = _dot(h, w_ref[:, 1024:1280])
    qs_ref[...] = _dot(h, w_ref[:, 1280:1792])
    sb_ref[...] = _dot(h, w_ref[:, 1792:2304])
    zg_ref[...] = _dot(h, w_ref[:, 2304:ODD_COLS])


def _inproj_odd(x, mod, norm_g, w, seq_len):
    n = x.shape[0]
    tm = _tile_rows(n)
    tok = lambda c: pl.BlockSpec((tm, c), lambda i: (i, 0))
    cols = (512, NSA_ROW, WIN_ROW, 512, SB_ROW, 128)
    return pl.pallas_call(
        _inproj_odd_body,
        grid=(n // tm,),
        in_specs=[tok(D_MODEL), _mod_spec(mod, tm, seq_len), _const_spec((3, D_MODEL)),
                  _const_spec((D_MODEL, ODD_COLS))],
        out_specs=[tok(c) for c in cols],
        out_shape=[jax.ShapeDtypeStruct((n, c), F32) for c in cols],
        compiler_params=_cparams("arbitrary"),
        name="inproj_odd",
    )(x, mod, norm_g, w)


def _outproj_even_body(x_ref, mod_ref, oa_ref, ob_ref, w_ref, o_ref):
    mix = (_dot(oa_ref[...].astype(BF16), w_ref[0:512, :])
           + _dot(ob_ref[...].astype(BF16), w_ref[512:1024, :]))
    o_ref[...] = x_ref[...] + mod_ref[0, 5] * mix


def _outproj_odd_body(x_ref, mod_ref, ocmp_ref, oslc_ref, owin_ref, zg_ref, osb_ref, e_ref, w_ref, o_ref):
    hi, mid, lo = _split3(jax.nn.sigmoid(zg_ref[...]))
    e = e_ref[...]
    gexp = _dot(hi, e) + _dot(mid, e) + _dot(lo, e)
    nsa = (gexp[:, 0:512] * ocmp_ref[...] + gexp[:, 512:1024] * oslc_ref[...]
           + gexp[:, 1024:1536] * owin_ref[...])
    mix = (_dot(nsa.astype(BF16), w_ref[0:512, :])
           + _dot(osb_ref[...].astype(BF16), w_ref[512:1024, :]))
    o_ref[...] = x_ref[...] + mod_ref[0, 5] * mix


def _outproj(x, mod, parts, consts, w, seq_len, odd):
    n = x.shape[0]
    tm = _tile_rows(n)
    tok = lambda c: pl.BlockSpec((tm, c), lambda i: (i, 0))
    in_specs = ([tok(D_MODEL), _mod_spec(mod, tm, seq_len)] + [tok(p.shape[1]) for p in parts]
                + [_const_spec(c.shape) for c in consts] + [_const_spec((1024, D_MODEL))])
    return pl.pallas_call(
        _outproj_odd_body if odd else _outproj_even_body,
        grid=(n // tm,),
        in_specs=in_specs,
        out_specs=tok(D_MODEL),
        out_shape=jax.ShapeDtypeStruct((n, D_MODEL), F32),
        compiler_params=_cparams("arbitrary"),
        name="outproj_odd" if odd else "outproj_even",
    )(x, mod, *parts, *consts, w)


def _softmax_init(m_ref, l_ref, acc_ref):
    m_ref[...] = jnp.full(m_ref.shape, NEG, F32)
    l_ref[...] = jnp.zeros(l_ref.shape, F32)
    acc_ref[...] = jnp.zeros(acc_ref.shape, F32)


def _softmax_step(s, mask, v, m_ref, l_ref, acc_ref, v_is_transposed=False):
    if mask is not None:
        s = jnp.where(mask, s, NEG)
    m_prev = m_ref[...]
    m_new = jnp.maximum(m_prev, jnp.max(s, axis=1, keepdims=True))
    p = jnp.exp(s - m_new)
    if mask is not None:
        p = jnp.where(mask, p, 0.0)
    alpha = jnp.exp(m_prev - m_new)
    l_ref[...] = alpha * l_ref[...] + jnp.sum(p, axis=1, keepdims=True)
    pv = _dot_nt(p.astype(BF16), v) if v_is_transposed else _dot(p.astype(BF16), v)
    acc_ref[...] = alpha * acc_ref[...] + pv
    m_ref[...] = m_new


def _softmax_out(l_ref, acc_ref):
    l = l_ref[...]
    return jnp.where(l > 0.0, acc_ref[...] / jnp.where(l > 0.0, l, 1.0), 0.0)


def _tsoftmax_step(s_t, mask_t, v_t, m_ref, l_ref, acc_ref):
    if mask_t is not None:
        s_t = jnp.where(mask_t, s_t, NEG)
    m_prev = m_ref[...]
    m_new = jnp.maximum(m_prev, jnp.max(s_t, axis=0, keepdims=True))
    p = jnp.exp(s_t - m_new)
    alpha = jnp.exp(m_prev - m_new)
    l_ref[...] = alpha * l_ref[...] + jnp.sum(p, axis=0, keepdims=True)
    acc_ref[...] = alpha * acc_ref[...] + _dot(v_t, p.astype(BF16))
    m_ref[...] = m_new


def _tsoftmax_out(l_ref, acc_ref):
    return acc_ref[...] * (1.0 / l_ref[...])


def _mla_stack_q_t(qlat, qrope):
    tq = qlat.shape[0]
    ql_t = qlat.T
    qr_t = qrope.T
    sub = _iota((128, tq), 0)
    parts = []
    for h in range(H_B):
        qr = qr_t[128 * (h // 4):128 * (h // 4 + 1)]
        qr = jnp.where((sub // ROPE_D) == (h % 4), qr, 0.0)
        parts.append(jnp.concatenate([ql_t[128 * h:128 * (h + 1)], qr], axis=0))
    return jnp.concatenate(parts, axis=1).astype(BF16)


def _mla_prompt_body(qlat_ref, qrope_ref, kfull_ref, wvbt_ref, o_ref, k_scr, vt_scr, m_ref, l_ref, acc_ref, *, tq):
    qi = pl.program_id(1)

    @pl.when(qi == 0)
    def _():
        kf = kfull_ref[0]
        k_scr[...] = kf.astype(BF16)
        for kb in range(kf.shape[0] // tq):
            vt_scr[kb] = kf[kb * tq:(kb + 1) * tq, 0:KV_LORA].T.astype(BF16)

    q_t = _mla_stack_q_t(qlat_ref[...], qrope_ref[...])
    _softmax_init(m_ref, l_ref, acc_ref)
    shape = (tq, H_B * tq)

    def step(kb, diag):
        k = k_scr[pl.ds(pl.multiple_of(kb * tq, tq), tq), :]
        mask = (_iota(shape, 0) <= (_iota(shape, 1) & (tq - 1))) if diag else None
        _tsoftmax_step(_dot(k, q_t), mask, vt_scr[kb], m_ref, l_ref, acc_ref)

    def far(kb, c):
        step(kb, False)
        return c

    lax.fori_loop(0, qi, far, 0)
    step(qi, True)
    o_t = _tsoftmax_out(l_ref, acc_ref)
    olat_t = jnp.concatenate([o_t[:, h * tq:(h + 1) * tq] for h in range(H_B)], axis=0)
    o_ref[...] = _dot(wvbt_ref[...], olat_t.astype(BF16)).T


def _mla_prompt(qlat, qrope, kfull, wvbt, batch, seq):
    tq = min(TQ, seq)
    assert tq & (tq - 1) == 0 and seq % tq == 0
    nq = seq // tq
    tok = lambda c: pl.BlockSpec((tq, c), lambda b, i: (b * nq + i, 0))
    return pl.pallas_call(
        functools.partial(_mla_prompt_body, tq=tq),
        grid=(batch, nq),
        in_specs=[tok(1024), tok(256), pl.BlockSpec((1, seq, 256), lambda b, i: (b, 0, 0)),
                  pl.BlockSpec((512, 1024), lambda b, i: (0, 0))],
        out_specs=tok(512),
        out_shape=jax.ShapeDtypeStruct((batch * seq, 512), F32),
        scratch_shapes=[pltpu.VMEM((seq, 256), BF16), pltpu.VMEM((nq, KV_LORA, tq), BF16),
                        pltpu.VMEM((1, H_B * tq), F32), pltpu.VMEM((1, H_B * tq), F32),
                        pltpu.VMEM((KV_LORA, H_B * tq), F32)],
        compiler_params=_cparams("arbitrary", "arbitrary"),
        name="mla_prompt",
    )(qlat, qrope, kfull.reshape(batch, seq, 256), wvbt)


def _hgrn_chunk(zq, zf, zi, zg, par, st, chunk, sub, valid):
    log_lb, log1m_lb, lb_pos, one_m_lb, gn = par[0:1], par[1:2], par[2:3], par[3:4], par[4:5]
    q = _silu(zq)
    ls = _log_sigmoid(zf)
    b = log1m_lb + ls
    lae = jnp.maximum(log_lb, b) + _softplus_neg_abs(log_lb - b)
    logf = jnp.where(lb_pos > 0.5, lae, ls)
    kin = one_m_lb * jax.nn.sigmoid(-zf)
    v = zi
    row = _iota((chunk, 128), 0)
    g = logf
    sh = 1
    while sh < chunk:
        g = g + jnp.where(row >= sh, pltpu.roll(g, sh, axis=0), 0.0)
        sh *= 2
    o = _dot_nt((q * jnp.exp(g)).astype(BF16), st.astype(BF16))
    nsub = chunk // sub
    v_bf = v.astype(BF16)
    if nsub > 1:
        ends = [g[sub * j + sub - 1:sub * j + sub] for j in range(nsub)]
        esub = jnp.concatenate([jnp.broadcast_to(e, (sub, 128)) for e in ends], axis=0)
        kt = (kin * jnp.exp(esub - g)).astype(BF16)
        col_a = _iota((chunk, chunk), 1)
        row_a = _iota((chunk, chunk), 0)
        a = jnp.zeros((chunk, chunk), F32)
        for j in range(nsub - 1):
            qj = (q * jnp.exp(jnp.minimum(g - ends[j], 0.0))).astype(BF16)
            aj = _dot_nt(qj, kt)
            a = jnp.where((col_a >= sub * j) & (col_a < sub * (j + 1)) & (row_a >= sub * (j + 1)), aj, a)
        o = o + _dot(a.astype(BF16), v_bf)
    row_s = _iota((sub, 128), 0)
    diag = []
    for i in range(nsub):
        g_i = g[sub * i:sub * (i + 1)]
        q_i = q[sub * i:sub * (i + 1)]
        o_i = jnp.zeros((sub, 128), F32)
        for s in range(min(sub, valid - sub * i)):
            r = sub * i + s
            e = jnp.exp(jnp.minimum(g_i - g[r:r + 1], 0.0))
            x = jnp.where(row_s >= s, q_i * (kin[r:r + 1] * e), 0.0)
            o_i = o_i + jnp.sum(x, axis=1, keepdims=True) * v[r:r + 1]
        diag.append(o_i)
    o = o + (diag[0] if nsub == 1 else jnp.concatenate(diag, axis=0))
    g_last = g[valid - 1:valid]
    khat = kin * jnp.exp(jnp.minimum(g_last - g, 0.0))
    if valid < chunk:
        khat = jnp.where(row < valid, khat, 0.0)
    st_new = st * jnp.exp(g_last) + _dot_tn(v_bf, khat.astype(BF16))
    return _rms(o) * gn * _silu(zg), st_new


def _hgrn_body(zq_ref, zf_ref, zi_ref, zg_ref, par_ref, *rest, chunk, sub, valid, n_chunks, has_s0):
    if has_s0:
        s0_ref, o_ref, sout_ref, st_ref = rest
    else:
        o_ref, sout_ref, st_ref = rest
    tb = pl.program_id(1)

    @pl.when(tb == 0)
    def _():
        for h in range(H_A):
            st_ref[h] = s0_ref[0, h].T if has_s0 else jnp.zeros((128, 128), F32)

    def chunk_body(c, carry):
        r0 = pl.multiple_of(c * chunk, chunk)
        for h in range(H_A):
            cs = slice(128 * h, 128 * (h + 1))
            rs = pl.ds(r0, chunk)
            o, st_new = _hgrn_chunk(zq_ref[rs, cs], zf_ref[rs, cs], zi_ref[rs, cs], zg_ref[rs, cs],
                                    par_ref[:, cs], st_ref[h], chunk, sub, valid)
            o_ref[rs, cs] = o
            st_ref[h] = st_new
        return carry

    lax.fori_loop(0, n_chunks, chunk_body, 0)

    @pl.when(tb == pl.num_programs(1) - 1)
    def _():
        for h in range(H_A):
            sout_ref[0, h] = st_ref[h].T


def _hgrn(zh, par, batch, rows_per_seq, tb, chunk, sub, valid, s0=None):
    nt = rows_per_seq // tb
    col = lambda j: pl.BlockSpec((tb, 512), lambda b, t: (b * nt + t, j))
    in_specs = [col(0), col(1), col(2), col(3), pl.BlockSpec((8, 512), lambda b, t: (0, 0))]
    args = [zh, zh, zh, zh, par]
    st_spec = pl.BlockSpec((1, H_A, 128, 128), lambda b, t: (b, 0, 0, 0))
    if s0 is not None:
        in_specs.append(st_spec)
        args.append(s0)
    return pl.pallas_call(
        functools.partial(_hgrn_body, chunk=chunk, sub=sub, valid=valid, n_chunks=tb // chunk,
                          has_s0=s0 is not None),
        grid=(batch, nt),
        in_specs=in_specs,
        out_specs=[pl.BlockSpec((tb, 512), lambda b, t: (b * nt + t, 0)), st_spec],
        out_shape=[jax.ShapeDtypeStruct((batch * rows_per_seq, 512), F32),
                   jax.ShapeDtypeStruct((batch, H_A, 128, 128), F32)],
        scratch_shapes=[pltpu.VMEM((H_A, 128, 128), F32)],
        compiler_params=_cparams("arbitrary", "arbitrary"),
        name="hgrn",
    )(*args)


def _block_means(x):
    nb = x.shape[0] // NSA_BLOCK
    return jnp.sum(x.reshape(nb, NSA_BLOCK, x.shape[1]), axis=1) * (1.0 / NSA_BLOCK)


def _means_prompt_body(x_ref, o_ref):
    o_ref[...] = _block_means(x_ref[...])


def _means_prompt(nsa_rows):
    n = nsa_rows.shape[0]
    tm = _tile_rows(n)
    return pl.pallas_call(
        _means_prompt_body,
        grid=(n // tm,),
        in_specs=[pl.BlockSpec((tm, 256), lambda i: (i, 0))],
        out_specs=pl.BlockSpec((tm // NSA_BLOCK, 256), lambda i: (i, 0)),
        out_shape=jax.ShapeDtypeStruct((n // NSA_BLOCK, 256), F32),
        compiler_params=_cparams("arbitrary"),
        name="nsa_means_prompt",
    )(nsa_rows)


def _cmp_head(qh, ck, cv, bias, vis):
    s = _dot_nt(qh.astype(BF16), ck) * ATT_SCALE + bias
    if vis is not None:
        s = jnp.where(vis, s, NEG)
    e = jnp.exp(s - jnp.max(s, axis=1, keepdims=True))
    p = e / jnp.sum(e, axis=1, keepdims=True)
    if vis is not None:
        p = jnp.where(vis, p, 0.0)
    return _dot(p.astype(BF16), cv), p


def _top_blocks(score, blk, n_sel):
    nb = score.shape[1]
    for _ in range(n_sel):
        m = jnp.max(score, axis=1, keepdims=True)
        idx = jnp.min(jnp.where(score == m, blk, nb), axis=1, keepdims=True)
        yield idx, m > 0.5 * NEG
        score = jnp.where(blk == idx, PICKED, score)


def _cmpsel_prompt_body(q_ref, cmp_ref, near_ref, cfar_ref, ocmp_ref, sel_ref, *, tq, nbp, n_sel):
    qi = pl.program_id(1)
    q_t = q_ref[...].T
    shape = (nbp, tq)
    qpos = qi * tq + _iota(shape, 1)
    blk = _iota(shape, 0)
    vis = (blk * NSA_BLOCK + NSA_BLOCK - 1) <= qpos
    cur = jnp.right_shift(qpos, 6)
    allowed = blk <= cur
    forced = (blk == 0) | (blk == cur) | (blk == cur - 1)
    near1 = blk == cur - 1
    near2 = blk == cur - 2
    outs = []
    for g in range(G_C):
        ck = cmp_ref[0, :, DH * g:DH * (g + 1)].astype(BF16)
        cv_t = cmp_ref[0, :, 128 + DH * g:128 + DH * (g + 1)].T.astype(BF16)
        imp = jnp.zeros(shape, F32)
        for j in range(HPG_C):
            h = HPG_C * g + j
            bias = jnp.where(near1, near_ref[0, h:h + 1, :],
                             jnp.where(near2, near_ref[1, h:h + 1, :], cfar_ref[h:h + 1, 0:1]))
            bias = jnp.where(blk == cur, cfar_ref[h:h + 1, 1:2], bias)
            s = _dot(ck, q_t[DH * h:DH * (h + 1)].astype(BF16)) * ATT_SCALE + bias
            s = jnp.where(vis, s, NEG)
            e = jnp.exp(s - jnp.max(s, axis=0, keepdims=True))
            p = jnp.where(vis, e / jnp.sum(e, axis=0, keepdims=True), 0.0)
            outs.append(_dot(cv_t, p.astype(BF16)))
            imp = imp + p
        score = jnp.where(allowed, imp + jnp.where(forced, FORCE_BONUS, 0.0), NEG)
        sel = jnp.zeros(shape, F32)
        for _ in range(n_sel):
            m = jnp.max(score, axis=0, keepdims=True)
            idx = jnp.min(jnp.where(score == m, blk, nbp), axis=0, keepdims=True)
            hit = blk == idx
            sel = jnp.where(hit & (m > 0.5 * NEG), 1.0, sel)
            score = jnp.where(hit, PICKED, score)
        sel_ref[0, g] = sel.astype(BF16)
    ocmp_ref[...] = jnp.concatenate(outs, axis=0).T


def _cmpsel_prompt(q, cmp, near, cfar, batch, seq):
    tq = min(TQ, seq)
    nq = seq // tq
    nbp = cmp.shape[1]
    n_sel = min(TOP_N, -(-seq // NSA_BLOCK))
    return pl.pallas_call(
        functools.partial(_cmpsel_prompt_body, tq=tq, nbp=nbp, n_sel=n_sel),
        grid=(batch, nq),
        in_specs=[pl.BlockSpec((tq, 512), lambda b, i: (b * nq + i, 0)),
                  pl.BlockSpec((1, nbp, 256), lambda b, i: (b, 0, 0)),
                  pl.BlockSpec((2, H_C, tq), lambda b, i: (0, 0, 0)),
                  pl.BlockSpec((H_C, 128), lambda b, i: (0, 0))],
        out_specs=[pl.BlockSpec((tq, 512), lambda b, i: (b * nq + i, 0)),
                   pl.BlockSpec((1, G_C, nbp, tq), lambda b, i: (b, 0, 0, i))],
        out_shape=[jax.ShapeDtypeStruct((batch * seq, 512), F32),
                   jax.ShapeDtypeStruct((batch, G_C, nbp, seq), BF16)],
        compiler_params=_cparams("arbitrary", "arbitrary"),
        name="nsa_cmpsel_prompt",
    )(q, cmp, near, cfar)


def _heads_t(q_t, h0, nh):
    return jnp.concatenate([q_t[DH * h:DH * (h + 1)] for h in range(h0, h0 + nh)], axis=1).astype(BF16)


def _fill_kv(src_ref, k_col, v_col, n_groups, k_scr, vt_scr, tq):
    for g in range(n_groups):
        k_scr[g] = src_ref[0, :, k_col + DH * g:k_col + DH * (g + 1)].astype(BF16)
    for kb in range(vt_scr.shape[0]):
        vt_scr[kb] = src_ref[0, kb * tq:(kb + 1) * tq, v_col:v_col + DH * n_groups].T.astype(BF16)


def _nsa_prompt_body(q_ref, nsa_ref, win_ref, sel_ref, bdt_ref, bst_ref, cfar_ref, oslc_ref, owin_ref,
                     ks_scr, vs_scr, kw_scr, vw_scr, m_ref, l_ref, acc_ref, *, tq, nbp):
    qi = pl.program_id(1)

    @pl.when(qi == 0)
    def _():
        _fill_kv(nsa_ref, 256, 384, G_C, ks_scr, vs_scr, tq)
        _fill_kv(win_ref, 0, 128, G_C, kw_scr, vw_scr, tq)

    q_t = (q_ref[...] * ATT_SCALE).T
    per_tile = tq // NSA_BLOCK
    n_back = WINDOW // tq
    shape = (tq, HPG_C * tq)
    key_i, qry_i = _iota(shape, 0), _iota(shape, 1) & (tq - 1)
    groups = range(G_C)
    heads = [range(HPG_C * g, HPG_C * (g + 1)) for g in groups]
    q_gs = [_heads_t(q_t, HPG_C * g, HPG_C) for g in groups]
    far_rows = [jnp.concatenate([jnp.broadcast_to(cfar_ref[h:h + 1, 0:1], (1, tq)) for h in heads[g]], axis=1)
                for g in groups]

    def tile(g, kb, mode, mask_t, k_scr, vt_scr):
        k = k_scr[g, pl.ds(pl.multiple_of(kb * tq, tq), tq), :]
        if mode == "far":
            bias = far_rows[g]
        else:
            tab = bdt_ref if mode == "diag" else bst_ref
            bias = jnp.concatenate([tab[h] for h in heads[g]], axis=1)
        _tsoftmax_step(_dot(k, q_gs[g]) + bias, mask_t, vt_scr[kb, DH * g:DH * (g + 1), :],
                       m_ref.at[g], l_ref.at[g], acc_ref.at[g])

    def init():
        for g in groups:
            _softmax_init(m_ref.at[g], l_ref.at[g], acc_ref.at[g])

    def outputs():
        return [_tsoftmax_out(l_ref.at[g], acc_ref.at[g])[:, j * tq:(j + 1) * tq]
                for g in groups for j in range(HPG_C)]

    def slc_tile(kb, mode):
        expand = _iota((tq, nbp), 1) == kb * per_tile + jnp.right_shift(_iota((tq, nbp), 0), 6)
        expand = jnp.where(expand, 1.0, 0.0).astype(BF16)
        for g in groups:
            m = _dot(expand, sel_ref[0, g]) > 0.5
            if mode == "diag":
                m = m & (_iota((tq, tq), 0) <= _iota((tq, tq), 1))
            tile(g, kb, mode, jnp.concatenate([m] * HPG_C, axis=1), ks_scr, vs_scr)

    init()

    def far(kb, c):
        slc_tile(kb, "far")
        return c

    lax.fori_loop(0, jnp.maximum(qi - 1, 0), far, 0)

    @pl.when(qi >= 1)
    def _():
        slc_tile(qi - 1, "sub")

    slc_tile(qi, "diag")
    oslc_ref[...] = jnp.concatenate(outputs(), axis=0).T

    init()
    for d in range(n_back, 0, -1):
        mask_t = (key_i >= qry_i) if d == n_back else None

        @pl.when(qi >= d)
        def _(d=d, mask_t=mask_t):
            for g in groups:
                tile(g, qi - d, "sub" if d == 1 else "far", mask_t, kw_scr, vw_scr)

    for g in groups:
        tile(g, qi, "diag", key_i <= qry_i, kw_scr, vw_scr)
    owin_ref[...] = jnp.concatenate(outputs(), axis=0).T


def _nsa_prompt_attn(q, nsa, win, sel, bdt, bst, cfar, batch, seq):
    tq = min(TQ, seq)
    assert tq & (tq - 1) == 0 and seq % tq == 0 and WINDOW % tq == 0 and tq >= MAX_DISTANCE
    nq = seq // tq
    nbp = sel.shape[2]
    tok = pl.BlockSpec((tq, 512), lambda b, i: (b * nq + i, 0))
    seq_spec = lambda w: pl.BlockSpec((1, seq, w), lambda b, i: (b, 0, 0))
    full = lambda a: pl.BlockSpec(a.shape, lambda b, i: (0,) * a.ndim)
    r = HPG_C * tq
    kv_scratch = [pltpu.VMEM((G_C, seq, DH), BF16), pltpu.VMEM((nq, G_C * DH, tq), BF16)]
    return pl.pallas_call(
        functools.partial(_nsa_prompt_body, tq=tq, nbp=nbp),
        grid=(batch, nq),
        in_specs=[tok, seq_spec(NSA_ROW), seq_spec(WIN_ROW),
                  pl.BlockSpec((1, G_C, nbp, tq), lambda b, i: (b, 0, 0, i)), full(bdt), full(bst), full(cfar)],
        out_specs=[tok, tok],
        out_shape=[jax.ShapeDtypeStruct((batch * seq, 512), F32)] * 2,
        scratch_shapes=kv_scratch + kv_scratch + [pltpu.VMEM((G_C, 1, r), F32), pltpu.VMEM((G_C, 1, r), F32),
                                                  pltpu.VMEM((G_C, DH, r), F32)],
        compiler_params=_cparams("arbitrary", "arbitrary"),
        name="nsa_slc_win_prompt",
    )(q, nsa.reshape(batch, seq, NSA_ROW), win.reshape(batch, seq, WIN_ROW), sel, bdt, bst, cfar)


def _sb_prompt_body(q_ref, rows_ref, u_ref, o_ref, k_scr, vt_scr, carry_ref, acc_ref, *, tq):
    qi = pl.program_id(1)

    @pl.when(qi == 0)
    def _():
        _fill_kv(rows_ref, 0, 256, KV_D, k_scr, vt_scr, tq)

    q_t = (q_ref[...] * ATT_SCALE).T
    shape = (tq, HPK_D * tq)
    before = _iota(shape, 0) < (_iota(shape, 1) & (tq - 1))
    q_gs = [_heads_t(q_t, HPK_D * g, HPK_D) for g in range(KV_D)]
    carry_ref[...] = jnp.zeros(carry_ref.shape, F32)
    acc_ref[...] = jnp.zeros(acc_ref.shape, F32)

    def step(kb, diag):
        for g in range(KV_D):
            k = k_scr[g, pl.ds(pl.multiple_of(kb * tq, tq), tq), :]
            z = _dot(k, q_gs[g])
            lsz = _log_sigmoid(z)
            lf = lsz - z
            if diag:
                lf = jnp.where(before, lf, 0.0)
            hi = lf.astype(BF16)
            mid = (lf - hi.astype(F32)).astype(BF16)
            u = u_ref[...]
            between = _dot(u, hi) + _dot(u, mid)
            a = jnp.exp(lsz + between + carry_ref[g])
            if diag:
                a = jnp.where(before, a, 0.0)
            acc_ref[g] += _dot(vt_scr[kb, DH * g:DH * (g + 1), :], a.astype(BF16))
            carry_ref[g] += jnp.sum(lf, axis=0, keepdims=True)

    step(qi, True)

    def back(it, c):
        step(qi - 1 - it, False)
        return c

    lax.fori_loop(0, qi, back, 0)
    outs = [acc_ref[g][:, j * tq:(j + 1) * tq] for g in range(KV_D) for j in range(HPK_D)]
    o_ref[...] = jnp.concatenate(outs, axis=0).T


def _sb_prompt(q, rows, batch, seq):
    tq = min(TQ, seq)
    assert tq & (tq - 1) == 0 and seq % tq == 0
    nq = seq // tq
    later = np.arange(tq)[None, :] > np.arange(tq)[:, None]
    u = jnp.asarray(later, BF16)
    tok = pl.BlockSpec((tq, 512), lambda b, i: (b * nq + i, 0))
    r = HPK_D * tq
    return pl.pallas_call(
        functools.partial(_sb_prompt_body, tq=tq),
        grid=(batch, nq),
        in_specs=[tok, pl.BlockSpec((1, seq, SB_ROW), lambda b, i: (b, 0, 0)),
                  pl.BlockSpec((tq, tq), lambda b, i: (0, 0))],
        out_specs=tok,
        out_shape=jax.ShapeDtypeStruct((batch * seq, 512), F32),
        scratch_shapes=[pltpu.VMEM((KV_D, seq, DH), BF16), pltpu.VMEM((nq, KV_D * DH, tq), BF16),
                        pltpu.VMEM((KV_D, 1, r), F32), pltpu.VMEM((KV_D, DH, r), F32)],
        compiler_params=_cparams("arbitrary", "arbitrary"),
        name="sb_prompt",
    )(q, rows.reshape(batch, seq, SB_ROW), u)


def _page_specs(block, pp, col_block, page_of):
    return [pl.BlockSpec(block, lambda b, s, pt, j=j: (page_of(b, s, j, pt), 0, col_block)) for j in range(pp)]


def _mla_decode_body(pt_ref, q_ref, knew_ref, wvb_ref, *rest, pp, valid):
    pages = rest[:pp]
    o_ref, m_ref, l_ref, acc_ref = rest[pp:]
    st = pl.program_id(1)
    r = SAMPLE_ROWS
    q = q_ref[0].astype(BF16)

    @pl.when(st == 0)
    def _():
        _softmax_init(m_ref, l_ref, acc_ref)

    k_t = jnp.concatenate([pages[j][0] for j in range(pp)], axis=1).astype(BF16)
    _softmax_step(_dot(q, k_t), None, k_t[0:KV_LORA], m_ref, l_ref, acc_ref, v_is_transposed=True)

    @pl.when(st == pl.num_programs(1) - 1)
    def _():
        kn = knew_ref[0].astype(BF16)
        shape = (H_B * r, r)
        col = _iota(shape, 1)
        mask = (col <= (_iota(shape, 0) & (r - 1))) & (col < valid)
        _softmax_step(_dot_nt(q, kn), mask, kn[:, 0:KV_LORA], m_ref, l_ref, acc_ref)
        o = _softmax_out(l_ref, acc_ref)
        olat = jnp.concatenate([o[r * h:r * (h + 1)] for h in range(H_B)], axis=1)
        o_ref[0] = _dot(olat.astype(BF16), wvb_ref[...])


def _mla_decode(page_table, q, knew, wvb, cache_t, valid):
    batch, n_pages = page_table.shape
    pp = min(2 * PAGES_PER_STEP, n_pages)
    assert n_pages % pp == 0
    r = SAMPLE_ROWS
    grid_spec = pltpu.PrefetchScalarGridSpec(
        num_scalar_prefetch=1,
        grid=(batch, n_pages // pp),
        in_specs=[pl.BlockSpec((1, H_B * r, MLA_ROW), lambda b, s, pt: (b, 0, 0)),
                  pl.BlockSpec((1, r, MLA_ROW), lambda b, s, pt: (b, 0, 0)),
                  pl.BlockSpec((1024, 512), lambda b, s, pt: (0, 0))]
        + _page_specs((1, MLA_ROW, PAGE), pp, 0, lambda b, s, j, pt: pt[b, s * pp + j]),
        out_specs=pl.BlockSpec((1, r, 512), lambda b, s, pt: (b, 0, 0)),
        scratch_shapes=[pltpu.VMEM((H_B * r, 1), F32), pltpu.VMEM((H_B * r, 1), F32),
                        pltpu.VMEM((H_B * r, KV_LORA), F32)])
    return pl.pallas_call(
        functools.partial(_mla_decode_body, pp=pp, valid=valid),
        grid_spec=grid_spec,
        out_shape=jax.ShapeDtypeStruct((batch, r, 512), F32),
        compiler_params=_cparams("arbitrary", "arbitrary"),
        name="mla_decode",
    )(page_table, q, knew, wvb, *([cache_t] * pp))


def _means_decode_body(pt_ref, *rest, pp):
    pages = rest[:pp]
    o_ref = rest[pp]
    o_ref[0] = _block_means(jnp.concatenate([pages[j][0] for j in range(pp)], axis=0))


def _means_decode(page_table, cache):
    batch, n_pages = page_table.shape
    pp = min(2 * PAGES_PER_STEP, n_pages)
    assert n_pages % pp == 0
    per_page = PAGE // NSA_BLOCK
    grid_spec = pltpu.PrefetchScalarGridSpec(
        num_scalar_prefetch=1,
        grid=(batch, n_pages // pp),
        in_specs=_page_specs((1, PAGE, 256), pp, 0, lambda b, s, j, pt: pt[b, s * pp + j]),
        out_specs=pl.BlockSpec((1, pp * per_page, 256), lambda b, s, pt: (b, s, 0)))
    return pl.pallas_call(
        functools.partial(_means_decode_body, pp=pp),
        grid_spec=grid_spec,
        out_shape=jax.ShapeDtypeStruct((batch, n_pages * per_page, 256), F32),
        compiler_params=_cparams("arbitrary", "arbitrary"),
        name="nsa_means_decode",
    )(page_table, *([cache] * pp))


def _cmpsel_decode_body(q_ref, cmp_ref, cb_ref, ocmp_ref, idx_ref, *, nb, n_past_sel):
    r = SAMPLE_ROWS
    blk = _iota((1, nb), 1)
    forced = (blk == 0) | (blk == nb - 1)
    lane = _iota((r, 128), 1)
    for g in range(G_C):
        ck = cmp_ref[0, :, DH * g:DH * (g + 1)].astype(BF16)
        cv = cmp_ref[0, :, 128 + DH * g:128 + DH * (g + 1)].astype(BF16)
        imp = jnp.zeros((r, nb), F32)
        for j in range(HPG_C):
            h = HPG_C * g + j
            o, p = _cmp_head(q_ref[0, :, DH * h:DH * (h + 1)], ck, cv, cb_ref[h], None)
            ocmp_ref[0, :, DH * h:DH * (h + 1)] = o
            imp = imp + p
        score = imp + jnp.where(forced, FORCE_BONUS, 0.0)
        picked = jnp.zeros((r, 128), jnp.int32)
        for slot, (idx, _) in enumerate(_top_blocks(score, blk, n_past_sel)):
            picked = jnp.where(lane == slot, idx, picked)
        idx_ref[0, g] = picked


def _cmpsel_decode(q, cmp, cb, n_past_sel):
    batch, nb, _ = cmp.shape
    r = SAMPLE_ROWS
    return pl.pallas_call(
        functools.partial(_cmpsel_decode_body, nb=nb, n_past_sel=n_past_sel),
        grid=(batch,),
        in_specs=[pl.BlockSpec((1, r, 512), lambda b: (b, 0, 0)),
                  pl.BlockSpec((1, nb, 256), lambda b: (b, 0, 0)),
                  pl.BlockSpec((H_C, r, nb), lambda b: (0, 0, 0))],
        out_specs=[pl.BlockSpec((1, r, 512), lambda b: (b, 0, 0)),
                   pl.BlockSpec((1, G_C, r, 128), lambda b: (b, 0, 0, 0))],
        out_shape=[jax.ShapeDtypeStruct((batch, r, 512), F32),
                   jax.ShapeDtypeStruct((batch, G_C, r, 128), jnp.int32)],
        compiler_params=_cparams("arbitrary"),
        name="nsa_cmpsel_decode",
    )(q, cmp, cb)


def _dist_bias(tab3_ref, d0, width):
    nd = tab3_ref.shape[1]
    dist = jnp.clip(d0 - _iota((nd, width), 1), 0, nd - 1)
    onehot = jnp.where(_iota((nd, width), 0) == dist, 1.0, 0.0).astype(BF16)
    b = _dot(tab3_ref[...], onehot)
    return b[0:8] + b[8:16] + b[16:24]


def _slc_decode_body(idx_ref, pt_ref, q_ref, new_ref, tab3_ref, *rest, n_slots, p_len, valid):
    blocks = rest[:G_C * n_slots]
    o_ref = rest[G_C * n_slots]
    b, t = pl.program_id(0), pl.program_id(1)
    r = SAMPLE_ROWS
    q = (q_ref[0, 0] * ATT_SCALE).astype(BF16)
    qpos = p_len + t
    head_row = _iota((H_C, 1), 0)
    out = jnp.zeros((H_C, DH), F32)
    for g in range(G_C):
        pieces = []
        for s in range(n_slots):
            n = idx_ref[((b * G_C + g) * r + t) * n_slots + s]
            kv = blocks[g * n_slots + s][0]
            k = kv[:, DH * g:DH * (g + 1)].astype(BF16)
            v = kv[:, 128 + DH * g:128 + DH * (g + 1)].astype(BF16)
            sc = _dot_nt(q, k) + _dist_bias(tab3_ref, qpos - n * NSA_BLOCK, NSA_BLOCK)
            pieces.append((sc, v))
        kn = new_ref[0, :, 256 + DH * g:256 + DH * (g + 1)].astype(BF16)
        vn = new_ref[0, :, 384 + DH * g:384 + DH * (g + 1)].astype(BF16)
        col = _iota((H_C, r), 1)
        cur_ok = (col <= t) & (col < valid)
        sc_new = jnp.where(cur_ok, _dot_nt(q, kn) + _dist_bias(tab3_ref, t, r), NEG)
        m = jnp.max(sc_new, axis=1, keepdims=True)
        for sc, _ in pieces:
            m = jnp.maximum(m, jnp.max(sc, axis=1, keepdims=True))
        p_new = jnp.where(cur_ok, jnp.exp(sc_new - m), 0.0)
        l = jnp.sum(p_new, axis=1, keepdims=True)
        acc = _dot(p_new.astype(BF16), vn)
        for sc, v in pieces:
            p = jnp.exp(sc - m)
            l = l + jnp.sum(p, axis=1, keepdims=True)
            acc = acc + _dot(p.astype(BF16), v)
        in_group = (head_row >= HPG_C * g) & (head_row < HPG_C * (g + 1))
        out = jnp.where(in_group, acc / l, out)
    o_ref[0, 0] = out


def _slc_decode(idx, page_table, q, new_rows, tab3, cache, n_tok, p_len):
    batch, n_pages = page_table.shape
    r = SAMPLE_ROWS
    n_slots = idx.shape[0] // (batch * G_C * r)
    per_page = PAGE // NSA_BLOCK

    def slot_spec(g, s):
        def index(b, t, idx_ref, pt):
            n = idx_ref[((b * G_C + g) * r + t) * n_slots + s]
            return (pt[b, n // per_page], n % per_page, 1)
        return pl.BlockSpec((1, NSA_BLOCK, 256), index)

    grid_spec = pltpu.PrefetchScalarGridSpec(
        num_scalar_prefetch=2,
        grid=(batch, n_tok),
        in_specs=[pl.BlockSpec((1, 1, H_C, DH), lambda b, t, i, pt: (b, t, 0, 0)),
                  pl.BlockSpec((1, r, NSA_ROW), lambda b, t, i, pt: (b, 0, 0)),
                  pl.BlockSpec(tab3.shape, lambda b, t, i, pt: (0, 0))]
        + [slot_spec(g, s) for g in range(G_C) for s in range(n_slots)],
        out_specs=pl.BlockSpec((1, 1, H_C, DH), lambda b, t, i, pt: (b, t, 0, 0)))
    return pl.pallas_call(
        functools.partial(_slc_decode_body, n_slots=n_slots, p_len=p_len, valid=n_tok),
        grid_spec=grid_spec,
        out_shape=jax.ShapeDtypeStruct((batch, n_tok, H_C, DH), F32),
        compiler_params=_cparams("arbitrary", "arbitrary"),
        name="nsa_slc_decode",
    )(idx, page_table, q, new_rows, tab3, *([cache] * (G_C * n_slots)))


def _win_decode_body(q_ref, state_ref, new_ref, wbs_ref, wbn_ref, o_ref, *, valid):
    r = SAMPLE_ROWS
    wb = state_ref.shape[1]
    for g in range(G_C):
        q4 = (jnp.concatenate([q_ref[0, :, DH * h:DH * (h + 1)] for h in range(HPG_C * g, HPG_C * (g + 1))],
                              axis=0) * ATT_SCALE).astype(BF16)
        ks = state_ref[0, :, DH * g:DH * (g + 1)].astype(BF16)
        vs = state_ref[0, :, 128 + DH * g:128 + DH * (g + 1)].astype(BF16)
        kn = new_ref[0, :, DH * g:DH * (g + 1)].astype(BF16)
        vn = new_ref[0, :, 128 + DH * g:128 + DH * (g + 1)].astype(BF16)
        tok = _iota((HPG_C * r, 1), 0) & (r - 1)
        ok_s = _iota((HPG_C * r, wb), 1) >= tok + (wb - WINDOW)
        col_n = _iota((HPG_C * r, r), 1)
        ok_n = (col_n <= tok) & (col_n < valid)
        bias_s = jnp.concatenate([wbs_ref[h] for h in range(HPG_C * g, HPG_C * (g + 1))], axis=0)
        bias_n = jnp.concatenate([wbn_ref[h] for h in range(HPG_C * g, HPG_C * (g + 1))], axis=0)
        s_s = jnp.where(ok_s, _dot_nt(q4, ks) + bias_s, NEG)
        s_n = jnp.where(ok_n, _dot_nt(q4, kn) + bias_n, NEG)
        m = jnp.maximum(jnp.max(s_s, axis=1, keepdims=True), jnp.max(s_n, axis=1, keepdims=True))
        p_s = jnp.where(ok_s, jnp.exp(s_s - m), 0.0)
        p_n = jnp.where(ok_n, jnp.exp(s_n - m), 0.0)
        l = jnp.sum(p_s, axis=1, keepdims=True) + jnp.sum(p_n, axis=1, keepdims=True)
        o = (_dot(p_s.astype(BF16), vs) + _dot(p_n.astype(BF16), vn)) / l
        for j in range(HPG_C):
            h = HPG_C * g + j
            o_ref[0, :, DH * h:DH * (h + 1)] = o[r * j:r * (j + 1)]


def _win_decode(q, state, new_rows, wbs, wbn, valid):
    batch, wb, _ = state.shape
    r = SAMPLE_ROWS
    return pl.pallas_call(
        functools.partial(_win_decode_body, valid=valid),
        grid=(batch,),
        in_specs=[pl.BlockSpec((1, r, 512), lambda b: (b, 0, 0)),
                  pl.BlockSpec((1, wb, WIN_ROW), lambda b: (b, 0, 0)),
                  pl.BlockSpec((1, r, WIN_ROW), lambda b: (b, 0, 0)),
                  pl.BlockSpec((H_C, r, wb), lambda b: (0, 0, 0)),
                  pl.BlockSpec((H_C, r, r), lambda b: (0, 0, 0))],
        out_specs=pl.BlockSpec((1, r, 512), lambda b: (b, 0, 0)),
        out_shape=jax.ShapeDtypeStruct((batch, r, 512), F32),
        compiler_params=_cparams("arbitrary"),
        name="nsa_win_decode",
    )(q, state, new_rows, wbs, wbn)


def _suffix_sum(x):
    n = x.shape[0]
    row = _iota(x.shape, 0)
    sh = 1
    while sh < n:
        x = x + jnp.where(row < n - sh, pltpu.roll(x, n - sh, axis=0), 0.0)
        sh *= 2
    return x


def _sb_decode_body(pt_ref, wq_ref, new_ref, *rest, pp, valid):
    pages = rest[:pp]
    o_ref, carry_ref, acc_ref, a_ref, v_ref = rest[pp:]
    st = pl.program_id(1)
    r = SAMPLE_ROWS
    half = 64
    wq = wq_ref[0].astype(BF16)
    lane = _iota((1, 128), 1)

    @pl.when(st == 0)
    def _():
        kn = new_ref[0]
        kk = kn[:, 0:256].astype(BF16)
        z = _dot(jnp.concatenate([kk, kk], axis=1), wq)
        lsz = _log_sigmoid(z)
        row = _iota((r, 128), 0)
        mask = (row < (_iota((r, 128), 1) & (r - 1))) & (row < valid)
        lf = jnp.where(mask, lsz - z, 0.0)
        a = jnp.where(mask, jnp.exp(lsz + _suffix_sum(lf) - lf), 0.0)
        pad = lambda x: jnp.concatenate([x, jnp.zeros((PAGE - r, x.shape[1]), x.dtype)], axis=0)
        vn = jnp.concatenate([kn[:, 256:512], jnp.zeros((r, 256), F32)], axis=1)
        acc_ref[...] = _dot_tn(pad(a).astype(BF16), pad(vn).astype(BF16))
        carry_ref[...] = jnp.sum(lf, axis=0, keepdims=True)

    carry = carry_ref[...]
    for i in range(pp // 2 - 1, -1, -1):
        early, late = pages[2 * i][0], pages[2 * i + 1][0]
        z = _dot(jnp.concatenate([early[:, 0:256], late[:, 0:256]], axis=1).astype(BF16), wq)
        lsz = _log_sigmoid(z)
        lf = lsz - z
        suf = _suffix_sum(lf)
        tot = suf[0:1]
        swapped = pltpu.roll(jnp.broadcast_to(tot, (8, 128)), half, axis=1)[0:1]
        between = (suf - lf) + (carry + jnp.where(lane < half, swapped, 0.0))
        a_ref[PAGE * i:PAGE * (i + 1), :] = jnp.exp(lsz + between).astype(BF16)
        v_ref[PAGE * i:PAGE * (i + 1), :] = jnp.concatenate([early[:, 256:512], late[:, 256:512]],
                                                            axis=1).astype(BF16)
        carry = carry + tot + swapped
    carry_ref[...] = carry
    acc_ref[...] += _dot_tn(a_ref[...], v_ref[...])

    @pl.when(st == pl.num_programs(1) - 1)
    def _():
        o_ref[0] = acc_ref[...]


def _sb_decode(page_table, wq, new_rows, cache, valid):
    batch, n_pages = page_table.shape
    pp = min(PAGES_PER_STEP, n_pages)
    assert n_pages % pp == 0 and pp % 2 == 0
    r = SAMPLE_ROWS
    grid_spec = pltpu.PrefetchScalarGridSpec(
        num_scalar_prefetch=1,
        grid=(batch, n_pages // pp),
        in_specs=[pl.BlockSpec((1, 512, 128), lambda b, s, pt: (b, 0, 0)),
                  pl.BlockSpec((1, r, SB_ROW), lambda b, s, pt: (b, 0, 0))]
        + _page_specs((1, PAGE, SB_ROW), pp, 0, lambda b, s, j, pt: pt[b, n_pages - (s + 1) * pp + j]),
        out_specs=pl.BlockSpec((1, 128, 512), lambda b, s, pt: (b, 0, 0)),
        scratch_shapes=[pltpu.VMEM((1, 128), F32), pltpu.VMEM((128, 512), F32),
                        pltpu.VMEM((pp // 2 * PAGE, 128), BF16), pltpu.VMEM((pp // 2 * PAGE, 512), BF16)])
    return pl.pallas_call(
        functools.partial(_sb_decode_body, pp=pp, valid=valid),
        grid_spec=grid_spec,
        out_shape=jax.ShapeDtypeStruct((batch, 128, 512), F32),
        compiler_params=_cparams("arbitrary", "arbitrary"),
        name="sb_decode",
    )(page_table, wq, new_rows, *([cache] * pp))


def _rope_tables(pos):
    half = ROPE_D // 2
    inv = ROPE_BASE ** (-jnp.arange(half, dtype=F32) / half)
    ang = pos.astype(F32)[:, None] * inv[None, :]
    cos, sin = jnp.cos(ang), jnp.sin(ang)
    return jnp.tile(cos, (1, 8)), jnp.tile(sin, (1, 8))


def _rot_cols(w):
    half = ROPE_D // 2
    return jnp.concatenate([-w[..., half:], w[..., :half]], axis=-1)


def _prep_even(w_in, w_qb, w_kb, w_vb):
    zkr = w_in[:, 2432:2464]
    w_e = jnp.concatenate([w_in[:, :2432], jnp.tile(zkr, (1, 4)), jnp.tile(_rot_cols(zkr), (1, 4))], axis=1)
    wqb = w_qb.reshape(Q_LORA, H_B, NOPE + ROPE_D)
    wr = wqb[:, :, NOPE:]
    w_q = jnp.concatenate([wqb[:, :, :NOPE].reshape(Q_LORA, 512), wr.reshape(Q_LORA, 256),
                           _rot_cols(wr).reshape(Q_LORA, 256)], axis=1)
    eye = jnp.eye(H_B, dtype=F32)
    wkb = jnp.einsum('hcn,hg->hngc', w_kb, eye).reshape(H_B * NOPE, H_B * KV_LORA)
    wvb = jnp.einsum('hcd,hg->hcgd', w_vb, eye).reshape(H_B * KV_LORA, H_B * VD_B)
    return w_e.astype(BF16), w_q.astype(BF16), wkb.astype(BF16), wvb.astype(BF16)


def _prep_odd(w_in):
    w = jnp.concatenate([w_in[:, :1280], w_in[:, 1304:2328], w_in[:, 1280:1304],
                         jnp.zeros((D_MODEL, ODD_COLS - 2328), w_in.dtype)], axis=1)
    return w.astype(BF16)


def _gate_expand():
    e = np.zeros((128, 3 * 512), np.float32)
    for h in range(H_C):
        for j in range(3):
            e[3 * h + j, 512 * j + DH * h:512 * j + DH * (h + 1)] = 1.0
    return jnp.asarray(e, BF16)


def _hgrn_params(hgrn_lb, norm_g, lj):
    lb_all = jax.nn.softmax(hgrn_lb.astype(F32), axis=0)
    lb = (jnp.cumsum(lb_all, axis=0) - lb_all[0])[lj]
    lb_pos = lb > 0
    rows = [jnp.log(jnp.where(lb_pos, lb, 1.0)), jnp.log1p(-lb), lb_pos.astype(F32), 1.0 - lb, norm_g[lj]]
    return jnp.concatenate([jnp.stack(rows), jnp.zeros((3, 512), F32)], axis=0)


def _t5_bucket(dist):
    exact = NUM_BUCKETS // 2
    d = jnp.maximum(dist, 0)
    large = exact + (jnp.log(jnp.maximum(d, 1).astype(F32) / exact)
                     / math.log(MAX_DISTANCE / exact) * (NUM_BUCKETS - exact)).astype(jnp.int32)
    return jnp.where(d < exact, d, jnp.minimum(large, NUM_BUCKETS - 1))


N_DIST = 256


def _toeplitz_t(tab, offset, n):
    i = jnp.arange(n, dtype=jnp.int32)[None, :]
    j = jnp.arange(n, dtype=jnp.int32)[:, None]
    idx = jnp.clip(offset + i - j, 0, N_DIST - 1).reshape(1, n * n)
    onehot = (idx == jnp.arange(N_DIST, dtype=jnp.int32)[:, None]).astype(F32)
    return jnp.dot(tab, onehot, precision=lax.Precision.HIGHEST).reshape(H_C, n, n)


def _decode_bias_rows(tab, base, n_keys, r):
    far = jnp.broadcast_to(tab[:, N_DIST - 1:], (H_C, n_keys + r))
    rev = jnp.concatenate([far, tab[:, ::-1], jnp.zeros((H_C, n_keys + r), F32)], axis=1)
    rows = []
    for t in range(r):
        start = n_keys + r + N_DIST - 1 - (base + t)
        rows.append(rev[:, start:start + n_keys])
    return jnp.stack(rows, axis=1)


def _pad_rows(a, batch, n_tok):
    a = a.reshape(batch, n_tok, -1)
    return jnp.pad(a, ((0, 0), (0, SAMPLE_ROWS - n_tok), (0, 0)))


def _last_rows(rows, n):
    t = rows.shape[1]
    if t < n:
        rows = jnp.pad(rows, ((0, 0), (n - t, 0), (0, 0)))
    return rows[:, rows.shape[1] - n:]


def _run_prompt(x, mod, w, tabs, batch, seq, win_len):
    assert seq % HGRN_CHUNK == 0 and (batch * seq) % _tile_rows(batch * seq) == 0
    tq = min(TQ, seq)
    pos = np.arange(seq)
    cos, sin = _rope_tables(jnp.asarray(pos, jnp.int32))
    tab = tabs['tab']
    bdt, bst = _toeplitz_t(tab, 0, tq), _toeplitz_t(tab, tq, tq)
    nblk = seq // NSA_BLOCK
    nbp = -(-nblk // 16) * 16
    reps = tq // NSA_BLOCK
    near = jnp.stack([jnp.tile(tab[:, 1:NSA_BLOCK + 1], (1, reps)),
                      jnp.tile(tab[:, NSA_BLOCK + 1:2 * NSA_BLOCK + 1], (1, reps))])
    states = []
    for l in range(DEPTH):
        lj = l // 2
        m = mod[l]
        x = _ffn(x, m, w['norm_g'][l], w['wg'], w['wu'], w['wd'], l, 0, 0, seq)
        if l % 2 == 0:
            w_e, w_q, wkb, wvb = w['even'][lj]
            zh, kfull, rows, qlat, qrope = _inproj_even(
                x, m, w['norm_g'][l], w_e, cos, sin, w['gq'][lj], w['gkv'][lj], w_q, wkb, seq)
            tb = min(TM_DENSE, seq)
            o_a, s_new = _hgrn(zh, w['hgrn_par'][lj], batch, seq, tb, HGRN_CHUNK, HGRN_SUB, HGRN_CHUNK)
            o_b = _mla_prompt(qlat, qrope, kfull, wvb.T, batch, seq)
            x = _outproj(x, m, [o_a, o_b], [], w['wout_even'][lj], seq, odd=False)
            states += [rows.reshape(batch, seq, MLA_ROW), s_new]
        else:
            qn, nsa, win, qs, sb, zg = _inproj_odd(x, m, w['norm_g'][l], w['odd'][lj], seq)
            cmp = _means_prompt(nsa).reshape(batch, nblk, 256)
            cmp = jnp.pad(cmp, ((0, 0), (0, nbp - nblk), (0, 0)))
            o_cmp, sel = _cmpsel_prompt(qn, cmp, near, tabs['cfar'], batch, seq)
            o_slc, o_win = _nsa_prompt_attn(qn, nsa, win, sel, bdt, bst, tabs['cfar'], batch, seq)
            o_sb = _sb_prompt(qs, sb, batch, seq)
            x = _outproj(x, m, [o_cmp, o_slc, o_win, zg, o_sb], [w['gate_e']], w['wout_odd'][lj], seq, odd=True)
            states += [nsa.reshape(batch, seq, NSA_ROW), _last_rows(win.reshape(batch, seq, WIN_ROW), win_len),
                       sb.reshape(batch, seq, SB_ROW)]
        x = _ffn(x, m, w['norm_g'][l], w['wg'], w['wu'], w['wd'], l, 1, 2, seq,
                 final_g=w['final_g'] if l == DEPTH - 1 else None)
    return x.reshape(batch, seq, D_MODEL), states


def _run_sample(x, mod, w, tabs, batch, n_tok, page_table, pasts):
    n = batch * n_tok
    n_pages = page_table.shape[1]
    p_len = n_pages * PAGE
    r = SAMPLE_ROWS
    assert n_tok <= r and n_tok <= NSA_BLOCK and n == _tile_rows(n)
    pos = p_len + np.arange(r)
    cos, sin = _rope_tables(jnp.asarray(np.tile(pos[:n_tok], batch), jnp.int32))
    tab = tabs['tab']
    nb = p_len // NSA_BLOCK
    n_past_sel = min(TOP_N, nb + 1) - 1
    assert nb >= 2 and p_len % NSA_BLOCK == 0
    cb = jnp.concatenate([jnp.broadcast_to(tab[:, None, N_DIST - 1:], (H_C, r, nb - 2)),
                          tab[:, NSA_BLOCK + 1:NSA_BLOCK + 1 + r, None], tab[:, 1:1 + r, None]], axis=2)
    states = []
    for l in range(DEPTH):
        lj = l // 2
        m = mod[l]
        x = _ffn(x, m, w['norm_g'][l], w['wg'], w['wu'], w['wd'], l, 0, 0, n)
        if l % 2 == 0:
            cache_mla, state_hgrn = pasts[l]
            w_e, w_q, wkb, wvb = w['even'][lj]
            zh, kfull, rows, qlat, qrope = _inproj_even(
                x, m, w['norm_g'][l], w_e, cos, sin, w['gq'][lj], w['gkv'][lj], w_q, wkb, n)
            zh8 = _pad_rows(zh, batch, n_tok).reshape(batch * r, 2048)
            o_a8, s_new = _hgrn(zh8, w['hgrn_par'][lj], batch, r, r, r, r, n_tok, s0=state_hgrn)
            o_a = o_a8.reshape(batch, r, 512)[:, :n_tok].reshape(n, 512)
            qf = jnp.concatenate([qlat.reshape(batch, n_tok, H_B, KV_LORA),
                                  qrope.reshape(batch, n_tok, H_B, ROPE_D)], axis=-1)
            qf = jnp.pad(qf, ((0, 0), (0, r - n_tok), (0, 0), (0, 0)))
            qf = jnp.swapaxes(qf, 1, 2).reshape(batch, H_B * r, MLA_ROW)
            o_b8 = _mla_decode(page_table, qf, _pad_rows(rows, batch, n_tok), wvb, jnp.swapaxes(cache_mla, 1, 2),
                               n_tok)
            o_b = o_b8[:, :n_tok].reshape(n, 512)
            x = _outproj(x, m, [o_a, o_b], [], w['wout_even'][lj], n, odd=False)
            states += [rows.reshape(batch, n_tok, MLA_ROW), s_new]
        else:
            cache_nsa, state_win, cache_sb = pasts[l]
            wb = state_win.shape[1]
            assert p_len >= wb and wb <= WINDOW
            qn, nsa, win, qs, sb, zg = _inproj_odd(x, m, w['norm_g'][l], w['odd'][lj], n)
            qn8 = _pad_rows(qn, batch, n_tok)
            nsa8 = _pad_rows(nsa, batch, n_tok)
            cmp = _means_decode(page_table, cache_nsa)
            o_cmp8, picked = _cmpsel_decode(qn8, cmp, cb, n_past_sel)
            idx = picked[..., :n_past_sel].reshape(-1)
            o_slc = _slc_decode(idx, page_table, qn.reshape(batch, n_tok, H_C, DH), nsa8, tabs['tab3'],
                                cache_nsa, n_tok, p_len)
            wbs = _decode_bias_rows(tab, wb, wb, r)
            wbn = _decode_bias_rows(tab, 0, r, r)
            o_win8 = _win_decode(qn8, state_win, _pad_rows(win, batch, n_tok), wbs, wbn, n_tok)
            q5 = _pad_rows(qs * ATT_SCALE, batch, n_tok).reshape(batch, r, KV_D, HPK_D, DH)
            q5 = jnp.transpose(q5, (0, 2, 4, 3, 1)).reshape(batch, KV_D, DH, HPK_D * r)
            wq = jnp.einsum('bgdc,gh->bgdhc', q5, jnp.eye(KV_D, dtype=F32)).reshape(batch, KV_D * DH, KV_D * HPK_D * r)
            nc = KV_D * HPK_D * r
            zc = jnp.zeros_like(wq)
            wq = jnp.concatenate([jnp.concatenate([wq, zc], axis=2), jnp.concatenate([zc, wq], axis=2)], axis=1)
            o_raw = _sb_decode(page_table, wq, _pad_rows(sb, batch, n_tok), cache_sb, n_tok)
            o6 = (o_raw[:, :nc, :256] + o_raw[:, nc:, 256:]).reshape(batch, KV_D, HPK_D, r, KV_D, DH)
            o_sb = jnp.stack([o6[:, g, :, :, g] for g in range(KV_D)], axis=1)
            o_sb = jnp.transpose(o_sb, (0, 3, 1, 2, 4))[:, :n_tok].reshape(n, 512)
            take = lambda a: a[:, :n_tok].reshape(n, 512)
            x = _outproj(x, m, [take(o_cmp8), o_slc.reshape(n, 512), take(o_win8), zg, o_sb], [w['gate_e']],
                         w['wout_odd'][lj], n, odd=True)
            new_win = jnp.concatenate([state_win, win.reshape(batch, n_tok, WIN_ROW)], axis=1)[:, n_tok:]
            states += [nsa.reshape(batch, n_tok, NSA_ROW), new_win, sb.reshape(batch, n_tok, SB_ROW)]
        x = _ffn(x, m, w['norm_g'][l], w['wg'], w['wu'], w['wd'], l, 1, 2, n,
                 final_g=w['final_g'] if l == DEPTH - 1 else None)
    return x.reshape(batch, n_tok, D_MODEL), states


def kernel(x_prompt, x_sample, cache_mla_l0, state_hgrn_l0, cache_nsa_l1, state_win_l1, cache_sb_l1, cache_mla_l2, state_hgrn_l2, cache_nsa_l3, state_win_l3, cache_sb_l3, page_table, c_prompt, c_sample, w_ada, b_ada, norm_g, ffn_w_gate, ffn_w_up, ffn_w_down, w_in_even, w_out_even, hgrn_lb, hgrn_norm_g, mla_q_norm_g, mla_kv_norm_g, mla_w_qb, mla_w_kb, mla_w_vb, w_in_odd, w_out_odd, rel_bias, final_norm_g):
    bp, seq, _ = x_prompt.shape
    bs, n_tok, _ = x_sample.shape
    n_even = w_in_even.shape[0]
    n_odd = w_in_odd.shape[0]
    w = dict(
        norm_g=norm_g, final_g=final_norm_g,
        wg=ffn_w_gate.astype(BF16), wu=ffn_w_up.astype(BF16), wd=ffn_w_down.astype(BF16),
        even=[_prep_even(w_in_even[j], mla_w_qb[j], mla_w_kb[j], mla_w_vb[j]) for j in range(n_even)],
        odd=[_prep_odd(w_in_odd[j]) for j in range(n_odd)],
        wout_even=w_out_even.astype(BF16), wout_odd=w_out_odd.astype(BF16),
        gq=mla_q_norm_g.reshape(n_even, 1, Q_LORA), gkv=mla_kv_norm_g.reshape(n_even, 1, KV_LORA),
        hgrn_par=[_hgrn_params(hgrn_lb, hgrn_norm_g, j) for j in range(n_even)],
        gate_e=_gate_expand())
    tab = rel_bias[_t5_bucket(jnp.arange(N_DIST, dtype=jnp.int32))].T.astype(F32)
    cfar = jnp.concatenate([tab[:, N_DIST - 1:], tab[:, 0:1], jnp.zeros((H_C, 126), F32)], axis=1)
    tabs = dict(tab=tab, cfar=cfar, tab3=jnp.concatenate(_split3(tab), axis=0))
    mod = _ada_mod(jnp.concatenate([c_prompt, c_sample], axis=0), w_ada, b_ada)
    mod = mod.reshape(DEPTH, bp + bs, N_MOD, D_MODEL)
    mod_p = mod[:, :bp, :, None, :]
    mod_s = jnp.transpose(jnp.repeat(mod[:, bp:], n_tok, axis=1), (0, 2, 1, 3))[:, None]
    win_len = state_win_l1.shape[1]
    y_p, st_p = _run_prompt(x_prompt.reshape(bp * seq, D_MODEL), mod_p, w, tabs, bp, seq, win_len)
    pasts = [(cache_mla_l0, state_hgrn_l0), (cache_nsa_l1, state_win_l1, cache_sb_l1),
             (cache_mla_l2, state_hgrn_l2), (cache_nsa_l3, state_win_l3, cache_sb_l3)]
    y_s, st_s = _run_sample(x_sample.reshape(bs * n_tok, D_MODEL), mod_s, w, tabs, bs, n_tok, page_table, pasts)
    out = [y_p, y_s]
    for a, b in zip(st_p, st_s):
        out += [a, b]
    return tuple(out)
```

```python
import functools
import math

import numpy as np
import jax
import jax.numpy as jnp
from jax import lax
from jax.experimental import pallas as pl
from jax.experimental.pallas import tpu as pltpu

F32 = jnp.float32
BF16 = jnp.bfloat16

D_MODEL = 1024
DEPTH = 4
PAGE = 128
NORM_EPS = 1e-6
NEG = -1e30
PICKED = -3e38
N_MOD = 9
D_FF = 2816

H_A = 4
DK_A = 128
HGRN_CHUNK = 64
HGRN_SUB = 16

H_B = 8
Q_LORA = 256
KV_LORA = 128
NOPE = 64
ROPE_D = 32
VD_B = 64
ROPE_BASE = 10000.0
MLA_ROW = KV_LORA + ROPE_D
MLA_SCALE = (NOPE + ROPE_D) ** -0.5

DH = 64
H_C = 8
G_C = 2
HPG_C = 4
NSA_BLOCK = 64
TOP_N = 8
WINDOW = 512
FORCE_BONUS = 100.0
NSA_ROW = 512
WIN_ROW = 256
H_D = 8
KV_D = 4
HPK_D = 2
SB_ROW = 512
ATT_SCALE = DH ** -0.5
NUM_BUCKETS = 32
MAX_DISTANCE = 128

V7X_VMEM_BYTES = 64 * 1024 * 1024
VMEM_LIMIT = V7X_VMEM_BYTES - 8 * 1024 * 1024

TM_DENSE = 512
FF_CHUNK = 1408
TQ = 256
TQ_MLA = 512
SAMPLE_ROWS = 8
PAGES_PER_STEP = 16


def _cparams(*sem):
    return pltpu.CompilerParams(dimension_semantics=sem, vmem_limit_bytes=VMEM_LIMIT)


def _iota(shape, dim):
    return lax.broadcasted_iota(jnp.int32, shape, dim)


def _rms(x):
    return x * lax.rsqrt(jnp.mean(x * x, axis=-1, keepdims=True) + NORM_EPS)


def _silu(x):
    return x * jax.nn.sigmoid(x)


def _softplus_neg_abs(x):
    return jnp.log(1.0 + jnp.exp(-jnp.abs(x)))


def _log_sigmoid(x):
    return jnp.minimum(x, 0.0) - _softplus_neg_abs(x)


LOG2E = math.log2(math.e)


def _log2_sigmoid(x2):
    return jnp.minimum(x2, 0.0) - jnp.log2(1.0 + jnp.exp2(-jnp.abs(x2)))


def _dot(a, b):
    return jnp.dot(a, b, preferred_element_type=F32)


def _dot_nt(a, b):
    return lax.dot_general(a, b, (((1,), (1,)), ((), ())), preferred_element_type=F32)


def _dot_tn(a, b):
    return lax.dot_general(a, b, (((0,), (0,)), ((), ())), preferred_element_type=F32)


def _split3(x):
    hi = x.astype(BF16)
    r = x - hi.astype(F32)
    mid = r.astype(BF16)
    lo = (r - mid.astype(F32)).astype(BF16)
    return hi, mid, lo


def _ada_body(c_ref, w_ref, b_ref, o_ref):
    h = _silu(c_ref[...]).astype(BF16)
    o_ref[0] = _dot(h, w_ref[0].astype(BF16)) + b_ref[0]


def _ada_mod(c_all, w_ada, b_ada):
    nc = c_all.shape[0]
    depth, _, ncol = w_ada.shape
    tn = 1024
    return pl.pallas_call(
        _ada_body,
        grid=(depth, ncol // tn),
        in_specs=[pl.BlockSpec((nc, D_MODEL), lambda l, j: (0, 0)),
                  pl.BlockSpec((1, D_MODEL, tn), lambda l, j: (l, 0, j)),
                  pl.BlockSpec((1, 1, tn), lambda l, j: (l, 0, j))],
        out_specs=pl.BlockSpec((1, nc, tn), lambda l, j: (l, 0, j)),
        out_shape=jax.ShapeDtypeStruct((depth, nc, ncol), F32),
        compiler_params=_cparams("arbitrary", "arbitrary"),
        name="ada_mod",
    )(c_all, w_ada, b_ada.reshape(depth, 1, ncol))


def _tile_rows(n):
    return min(TM_DENSE, n)


def _mod_spec(mod, tm, seq_len):
    s, _, r, _ = mod.shape
    if r == 1:
        tiles_per_seq = seq_len // tm
        return pl.BlockSpec((1, N_MOD, 1, D_MODEL), lambda i: (i // tiles_per_seq, 0, 0, 0))
    return pl.BlockSpec((1, N_MOD, tm, D_MODEL), lambda i: (0, 0, i, 0))


def _prenorm(x, mod_ref, g_ref, sub):
    shift = mod_ref[0, 3 * sub]
    scale = mod_ref[0, 3 * sub + 1]
    return _rms(x) * g_ref[sub:sub + 1, :] * (1.0 + scale) + shift


def _const_spec(shape):
    nd = len(shape)
    return pl.BlockSpec(shape, lambda i: (0,) * nd, pipeline_mode=pl.Buffered(1))


def _ffn_body(x_ref, mod_ref, g_ref, wg_ref, wu_ref, wd_ref, *rest, sub, final):
    o_ref = rest[-1]
    x = x_ref[...]
    h = _prenorm(x, mod_ref, g_ref, sub).astype(BF16)
    acc = None
    for c0 in range(0, D_FF, FF_CHUNK):
        a = _dot(h, wg_ref[:, c0:c0 + FF_CHUNK])
        u = _dot(h, wu_ref[:, c0:c0 + FF_CHUNK])
        t = (_silu(a) * u).astype(BF16)
        part = _dot(t, wd_ref[c0:c0 + FF_CHUNK, :])
        acc = part if acc is None else acc + part
    y = x + 0.5 * mod_ref[0, 3 * sub + 2] * acc
    if final:
        y = _rms(y) * rest[0][...]
    o_ref[...] = y


def _ffn(x, mod, norm_g, wg, wu, wd, layer, which, sub, seq_len, final_g=None):
    n = x.shape[0]
    tm = _tile_rows(n)
    final = final_g is not None
    pick = lambda r, c: pl.BlockSpec((None, None, r, c), lambda i: (layer, which, 0, 0),
                                     pipeline_mode=pl.Buffered(1))
    in_specs = [pl.BlockSpec((tm, D_MODEL), lambda i: (i, 0)),
                _mod_spec(mod, tm, seq_len),
                _const_spec((3, D_MODEL)),
                pick(D_MODEL, D_FF), pick(D_MODEL, D_FF), pick(D_FF, D_MODEL)]
    args = [x, mod, norm_g, wg, wu, wd]
    if final:
        in_specs.append(_const_spec((1, D_MODEL)))
        args.append(final_g.reshape(1, D_MODEL))
    return pl.pallas_call(
        functools.partial(_ffn_body, sub=sub, final=final),
        grid=(n // tm,),
        in_specs=in_specs,
        out_specs=pl.BlockSpec((tm, D_MODEL), lambda i: (i, 0)),
        out_shape=jax.ShapeDtypeStruct((n, D_MODEL), F32),
        compiler_params=_cparams("arbitrary"),
        name="ffn",
    )(*args)


EVEN_COLS = 2688
ODD_COLS = 2432


def _inproj_even_body(x_ref, mod_ref, g_ref, w_ref, cos_ref, sin_ref, gq_ref, gkv_ref, wq_ref, wkb_ref,
                      zh_ref, kfull_ref, rows_ref, qlat_ref, qrope_ref):
    h = _prenorm(x_ref[...], mod_ref, g_ref, 1).astype(BF16)
    zh_ref[...] = _dot(h, w_ref[:, 0:2048])
    z = _dot(h, w_ref[:, 2048:EVEN_COLS])
    cos = cos_ref[...]
    sin = sin_ref[...]
    ckv = _rms(z[:, 256:384]) * gkv_ref[...]
    krope = z[:, 384:512] * cos + z[:, 512:640] * sin
    kfull_ref[:, 0:128] = ckv
    kfull_ref[:, 128:256] = krope
    rows_ref[:, 0:128] = ckv
    rows_ref[:, 128:MLA_ROW] = krope[:, 0:ROPE_D]
    qa = (_rms(z[:, 0:256]) * gq_ref[...]).astype(BF16)
    qz = _dot(qa, wq_ref[...])
    qlat_ref[...] = _dot(qz[:, 0:512].astype(BF16), wkb_ref[...]) * MLA_SCALE
    cos2 = jnp.concatenate([cos, cos], axis=1)
    sin2 = jnp.concatenate([sin, sin], axis=1)
    qrope_ref[...] = (qz[:, 512:768] * cos2 + qz[:, 768:1024] * sin2) * MLA_SCALE


def _inproj_even(x, mod, norm_g, w, cos, sin, gq, gkv, wq, wkb, seq_len):
    n = x.shape[0]
    tm = _tile_rows(n)
    tab_tiles = cos.shape[0] // tm
    tok = lambda c: pl.BlockSpec((tm, c), lambda i: (i, 0))
    tab = pl.BlockSpec((tm, 128), lambda i: (i % tab_tiles, 0))
    return pl.pallas_call(
        _inproj_even_body,
        grid=(n // tm,),
        in_specs=[tok(D_MODEL), _mod_spec(mod, tm, seq_len), _const_spec((3, D_MODEL)),
                  _const_spec((D_MODEL, EVEN_COLS)), tab, tab,
                  _const_spec((1, Q_LORA)), _const_spec((1, KV_LORA)),
                  _const_spec((Q_LORA, 1024)), _const_spec((512, 1024))],
        out_specs=[tok(2048), tok(256), tok(MLA_ROW), tok(1024), tok(256)],
        out_shape=[jax.ShapeDtypeStruct((n, c), F32) for c in (2048, 256, MLA_ROW, 1024, 256)],
        compiler_params=_cparams("arbitrary"),
        name="inproj_even",
    )(x, mod, norm_g, w, cos, sin, gq, gkv, wq, wkb)


def _inproj_odd_body(x_ref, mod_ref, g_ref, w_ref, qn_ref, nsa_ref, win_ref, qs_ref, sb_ref, zg_ref):
    h = _prenorm(x_ref[...], mod_ref, g_ref, 1).astype(BF16)
    qn_ref[...] = _dot(h, w_ref[:, 0:512])
    nsa_ref[...] = _dot(h, w_ref[:, 512:1024])
    win_ref[...] = _dot(h, w_ref[:, 1024:1280])
    qs_ref[...] = _dot(h, w_ref[:, 1280:1792])
    sb_ref[...] = _dot(h, w_ref[:, 1792:2304])
    zg_ref[...] = _dot(h, w_ref[:, 2304:ODD_COLS])


def _inproj_odd(x, mod, norm_g, w, seq_len):
    n = x.shape[0]
    tm = _tile_rows(n)
    tok = lambda c: pl.BlockSpec((tm, c), lambda i: (i, 0))
    cols = (512, NSA_ROW, WIN_ROW, 512, SB_ROW, 128)
    return pl.pallas_call(
        _inproj_odd_body,
        grid=(n // tm,),
        in_specs=[tok(D_MODEL), _mod_spec(mod, tm, seq_len), _const_spec((3, D_MODEL)),
                  _const_spec((D_MODEL, ODD_COLS))],
        out_specs=[tok(c) for c in cols],
        out_shape=[jax.ShapeDtypeStruct((n, c), F32) for c in cols],
        compiler_params=_cparams("arbitrary"),
        name="inproj_odd",
    )(x, mod, norm_g, w)


def _outproj_even_body(x_ref, mod_ref, oa_ref, ob_ref, w_ref, o_ref):
    mix = (_dot(oa_ref[...].astype(BF16), w_ref[0:512, :])
           + _dot(ob_ref[...].astype(BF16), w_ref[512:1024, :]))
    o_ref[...] = x_ref[...] + mod_ref[0, 5] * mix


def _outproj_odd_body(x_ref, mod_ref, ocmp_ref, oslc_ref, owin_ref, zg_ref, osb_ref, e_ref, w_ref, o_ref):
    hi, mid, lo = _split3(jax.nn.sigmoid(zg_ref[...]))
    e = e_ref[...]
    gexp = _dot(hi, e) + _dot(mid, e) + _dot(lo, e)
    nsa = (gexp[:, 0:512] * ocmp_ref[...] + gexp[:, 512:1024] * oslc_ref[...]
           + gexp[:, 1024:1536] * owin_ref[...])
    mix = (_dot(nsa.astype(BF16), w_ref[0:512, :])
           + _dot(osb_ref[...].astype(BF16), w_ref[512:1024, :]))
    o_ref[...] = x_ref[...] + mod_ref[0, 5] * mix


def _outproj(x, mod, parts, consts, w, seq_len, odd):
    n = x.shape[0]
    tm = _tile_rows(n)
    tok = lambda c: pl.BlockSpec((tm, c), lambda i: (i, 0))
    in_specs = ([tok(D_MODEL), _mod_spec(mod, tm, seq_len)] + [tok(p.shape[1]) for p in parts]
                + [_const_spec(c.shape) for c in consts] + [_const_spec((1024, D_MODEL))])
    return pl.pallas_call(
        _outproj_odd_body if odd else _outproj_even_body,
        grid=(n // tm,),
        in_specs=in_specs,
        out_specs=tok(D_MODEL),
        out_shape=jax.ShapeDtypeStruct((n, D_MODEL), F32),
        compiler_params=_cparams("arbitrary"),
        name="outproj_odd" if odd else "outproj_even",
    )(x, mod, *parts, *consts, w)


def _softmax_init(m_ref, l_ref, acc_ref):
    m_ref[...] = jnp.full(m_ref.shape, NEG, F32)
    l_ref[...] = jnp.zeros(l_ref.shape, F32)
    acc_ref[...] = jnp.zeros(acc_ref.shape, F32)


def _softmax_step(s, mask, v, m_ref, l_ref, acc_ref, v_is_transposed=False):
    if mask is not None:
        s = jnp.where(mask, s, NEG)
    m_prev = m_ref[...]
    m_new = jnp.maximum(m_prev, jnp.max(s, axis=1, keepdims=True))
    p = jnp.exp(s - m_new)
    if mask is not None:
        p = jnp.where(mask, p, 0.0)
    alpha = jnp.exp(m_prev - m_new)
    l_ref[...] = alpha * l_ref[...] + jnp.sum(p, axis=1, keepdims=True)
    pv = _dot_nt(p.astype(BF16), v) if v_is_transposed else _dot(p.astype(BF16), v)
    acc_ref[...] = alpha * acc_ref[...] + pv
    m_ref[...] = m_new


def _softmax_out(l_ref, acc_ref):
    l = l_ref[...]
    return jnp.where(l > 0.0, acc_ref[...] / jnp.where(l > 0.0, l, 1.0), 0.0)


def _tsoftmax_step(s_t, mask_t, v_t, m_ref, l_ref, acc_ref):
    if mask_t is not None:
        s_t = jnp.where(mask_t, s_t, NEG)
    m_prev = m_ref[...]
    m_new = jnp.maximum(m_prev, jnp.max(s_t, axis=0, keepdims=True))
    p = jnp.exp(s_t - m_new)
    alpha = jnp.exp(m_prev - m_new)
    l_ref[...] = alpha * l_ref[...] + jnp.sum(p, axis=0, keepdims=True)
    acc_ref[...] = alpha * acc_ref[...] + _dot(v_t, p.astype(BF16))
    m_ref[...] = m_new


def _tsoftmax_out(l_ref, acc_ref):
    return acc_ref[...] * (1.0 / l_ref[...])


def _mla_stack_q_t(qlat, qrope):
    tq = qlat.shape[0]
    ql_t = qlat.T
    qr_t = qrope.T
    sub = _iota((128, tq), 0)
    parts = []
    for h in range(H_B):
        qr = qr_t[128 * (h // 4):128 * (h // 4 + 1)]
        qr = jnp.where((sub // ROPE_D) == (h % 4), qr, 0.0)
        parts.append(jnp.concatenate([ql_t[128 * h:128 * (h + 1)], qr], axis=0))
    return jnp.concatenate(parts, axis=1).astype(BF16)


def _mla_prompt_body(qlat_ref, qrope_ref, kfull_ref, wvbt_ref, o_ref, k_scr, vt_scr, m_ref, l_ref, acc_ref, *, tq):
    qi = pl.program_id(1)

    @pl.when(qi == 0)
    def _():
        kf = kfull_ref[0]
        k_scr[...] = kf.astype(BF16)
        for kb in range(kf.shape[0] // tq):
            vt_scr[kb] = kf[kb * tq:(kb + 1) * tq, 0:KV_LORA].T.astype(BF16)

    q_t = _mla_stack_q_t(qlat_ref[...], qrope_ref[...])
    _softmax_init(m_ref, l_ref, acc_ref)
    shape = (tq, H_B * tq)

    def step(kb, diag):
        k = k_scr[pl.ds(pl.multiple_of(kb * tq, tq), tq), :]
        mask = (_iota(shape, 0) <= (_iota(shape, 1) & (tq - 1))) if diag else None
        _tsoftmax_step(_dot(k, q_t), mask, vt_scr[kb], m_ref, l_ref, acc_ref)

    def far(kb, c):
        step(kb, False)
        return c

    lax.fori_loop(0, qi, far, 0)
    step(qi, True)
    o_t = _tsoftmax_out(l_ref, acc_ref)
    olat_t = jnp.concatenate([o_t[:, h * tq:(h + 1) * tq] for h in range(H_B)], axis=0)
    o_ref[...] = _dot(wvbt_ref[...], olat_t.astype(BF16)).T


def _mla_prompt(qlat, qrope, kfull, wvbt, batch, seq):
    tq = min(TQ_MLA, seq)
    assert tq & (tq - 1) == 0 and seq % tq == 0
    nq = seq // tq
    tok = lambda c: pl.BlockSpec((tq, c), lambda b, i: (b * nq + i, 0))
    return pl.pallas_call(
        functools.partial(_mla_prompt_body, tq=tq),
        grid=(batch, nq),
        in_specs=[tok(1024), tok(256), pl.BlockSpec((1, seq, 256), lambda b, i: (b, 0, 0)),
                  pl.BlockSpec((512, 1024), lambda b, i: (0, 0))],
        out_specs=tok(512),
        out_shape=jax.ShapeDtypeStruct((batch * seq, 512), F32),
        scratch_shapes=[pltpu.VMEM((seq, 256), BF16), pltpu.VMEM((nq, KV_LORA, tq), BF16),
                        pltpu.VMEM((1, H_B * tq), F32), pltpu.VMEM((1, H_B * tq), F32),
                        pltpu.VMEM((KV_LORA, H_B * tq), F32)],
        compiler_params=_cparams("arbitrary", "arbitrary"),
        name="mla_prompt",
    )(qlat, qrope, kfull.reshape(batch, seq, 256), wvbt)


def _hgrn_chunk(zq, zf, zi, zg, par, st, chunk, sub, valid):
    log_lb, log1m_lb, lb_pos, one_m_lb, gn = par[0:1], par[1:2], par[2:3], par[3:4], par[4:5]
    q = _silu(zq)
    ls = _log_sigmoid(zf)
    b = log1m_lb + ls
    lae = jnp.maximum(log_lb, b) + _softplus_neg_abs(log_lb - b)
    logf = jnp.where(lb_pos > 0.5, lae, ls)
    kin = one_m_lb * jax.nn.sigmoid(-zf)
    v = zi
    row = _iota((chunk, 128), 0)
    g = logf
    sh = 1
    while sh < chunk:
        g = g + jnp.where(row >= sh, pltpu.roll(g, sh, axis=0), 0.0)
        sh *= 2
    o = _dot_nt((q * jnp.exp(g)).astype(BF16), st.astype(BF16))
    nsub = chunk // sub
    v_bf = v.astype(BF16)
    if nsub > 1:
        ends = [g[sub * j + sub - 1:sub * j + sub] for j in range(nsub)]
        esub = jnp.concatenate([jnp.broadcast_to(e, (sub, 128)) for e in ends], axis=0)
        kt = (kin * jnp.exp(esub - g)).astype(BF16)
        col_a = _iota((chunk, chunk), 1)
        row_a = _iota((chunk, chunk), 0)
        a = jnp.zeros((chunk, chunk), F32)
        for j in range(nsub - 1):
            qj = (q * jnp.exp(jnp.minimum(g - ends[j], 0.0))).astype(BF16)
            aj = _dot_nt(qj, kt)
            a = jnp.where((col_a >= sub * j) & (col_a < sub * (j + 1)) & (row_a >= sub * (j + 1)), aj, a)
        o = o + _dot(a.astype(BF16), v_bf)
    row_s = _iota((sub, 128), 0)
    diag = []
    for i in range(nsub):
        g_i = g[sub * i:sub * (i + 1)]
        q_i = q[sub * i:sub * (i + 1)]
        o_i = jnp.zeros((sub, 128), F32)
        for s in range(min(sub, valid - sub * i)):
            r = sub * i + s
            e = jnp.exp(jnp.minimum(g_i - g[r:r + 1], 0.0))
            x = jnp.where(row_s >= s, q_i * (kin[r:r + 1] * e), 0.0)
            o_i = o_i + jnp.sum(x, axis=1, keepdims=True) * v[r:r + 1]
        diag.append(o_i)
    o = o + (diag[0] if nsub == 1 else jnp.concatenate(diag, axis=0))
    g_last = g[valid - 1:valid]
    khat = kin * jnp.exp(jnp.minimum(g_last - g, 0.0))
    if valid < chunk:
        khat = jnp.where(row < valid, khat, 0.0)
    st_new = st * jnp.exp(g_last) + _dot_tn(v_bf, khat.astype(BF16))
    return _rms(o) * gn * _silu(zg), st_new


def _hgrn_body(zq_ref, zf_ref, zi_ref, zg_ref, par_ref, *rest, chunk, sub, valid, n_chunks, has_s0):
    if has_s0:
        s0_ref, o_ref, sout_ref, st_ref = rest
    else:
        o_ref, sout_ref, st_ref = rest
    tb = pl.program_id(1)

    @pl.when(tb == 0)
    def _():
        for h in range(H_A):
            st_ref[h] = s0_ref[0, h].T if has_s0 else jnp.zeros((128, 128), F32)

    def chunk_body(c, carry):
        r0 = pl.multiple_of(c * chunk, chunk)
        for h in range(H_A):
            cs = slice(128 * h, 128 * (h + 1))
            rs = pl.ds(r0, chunk)
            o, st_new = _hgrn_chunk(zq_ref[rs, cs], zf_ref[rs, cs], zi_ref[rs, cs], zg_ref[rs, cs],
                                    par_ref[:, cs], st_ref[h], chunk, sub, valid)
            o_ref[rs, cs] = o
            st_ref[h] = st_new
        return carry

    lax.fori_loop(0, n_chunks, chunk_body, 0)

    @pl.when(tb == pl.num_programs(1) - 1)
    def _():
        for h in range(H_A):
            sout_ref[0, h] = st_ref[h].T


def _hgrn(zh, par, batch, rows_per_seq, tb, chunk, sub, valid, s0=None):
    nt = rows_per_seq // tb
    col = lambda j: pl.BlockSpec((tb, 512), lambda b, t: (b * nt + t, j))
    in_specs = [col(0), col(1), col(2), col(3), pl.BlockSpec((8, 512), lambda b, t: (0, 0))]
    args = [zh, zh, zh, zh, par]
    st_spec = pl.BlockSpec((1, H_A, 128, 128), lambda b, t: (b, 0, 0, 0))
    if s0 is not None:
        in_specs.append(st_spec)
        args.append(s0)
    return pl.pallas_call(
        functools.partial(_hgrn_body, chunk=chunk, sub=sub, valid=valid, n_chunks=tb // chunk,
                          has_s0=s0 is not None),
        grid=(batch, nt),
        in_specs=in_specs,
        out_specs=[pl.BlockSpec((tb, 512), lambda b, t: (b * nt + t, 0)), st_spec],
        out_shape=[jax.ShapeDtypeStruct((batch * rows_per_seq, 512), F32),
                   jax.ShapeDtypeStruct((batch, H_A, 128, 128), F32)],
        scratch_shapes=[pltpu.VMEM((H_A, 128, 128), F32)],
        compiler_params=_cparams("arbitrary", "arbitrary"),
        name="hgrn",
    )(*args)


def _block_means(x):
    nb = x.shape[0] // NSA_BLOCK
    return jnp.sum(x.reshape(nb, NSA_BLOCK, x.shape[1]), axis=1) * (1.0 / NSA_BLOCK)


def _means_prompt_body(x_ref, o_ref):
    o_ref[...] = _block_means(x_ref[...])


def _means_prompt(nsa_rows):
    n = nsa_rows.shape[0]
    tm = _tile_rows(n)
    return pl.pallas_call(
        _means_prompt_body,
        grid=(n // tm,),
        in_specs=[pl.BlockSpec((tm, 256), lambda i: (i, 0))],
        out_specs=pl.BlockSpec((tm // NSA_BLOCK, 256), lambda i: (i, 0)),
        out_shape=jax.ShapeDtypeStruct((n // NSA_BLOCK, 256), F32),
        compiler_params=_cparams("arbitrary"),
        name="nsa_means_prompt",
    )(nsa_rows)


def _cmp_head(qh, ck, cv, bias, vis):
    s = _dot_nt(qh.astype(BF16), ck) * ATT_SCALE + bias
    if vis is not None:
        s = jnp.where(vis, s, NEG)
    e = jnp.exp(s - jnp.max(s, axis=1, keepdims=True))
    p = e / jnp.sum(e, axis=1, keepdims=True)
    if vis is not None:
        p = jnp.where(vis, p, 0.0)
    return _dot(p.astype(BF16), cv), p


def _top_blocks(score, blk, n_sel):
    nb = score.shape[1]
    for _ in range(n_sel):
        m = jnp.max(score, axis=1, keepdims=True)
        idx = jnp.min(jnp.where(score == m, blk, nb), axis=1, keepdims=True)
        yield idx, m > 0.5 * NEG
        score = jnp.where(blk == idx, PICKED, score)


def _cmpsel_prompt_body(q_ref, cmp_ref, near_ref, cfar_ref, ocmp_ref, sel_ref, *, tq, nbp, n_sel):
    qi = pl.program_id(1)
    q_t = q_ref[...].T
    shape = (nbp, tq)
    qpos = qi * tq + _iota(shape, 1)
    blk = _iota(shape, 0)
    vis = (blk * NSA_BLOCK + NSA_BLOCK - 1) <= qpos
    cur = jnp.right_shift(qpos, 6)
    allowed = blk <= cur
    forced = (blk == 0) | (blk == cur) | (blk == cur - 1)
    near1 = blk == cur - 1
    near2 = blk == cur - 2
    outs = []
    for g in range(G_C):
        ck = cmp_ref[0, :, DH * g:DH * (g + 1)].astype(BF16)
        cv_t = cmp_ref[0, :, 128 + DH * g:128 + DH * (g + 1)].T.astype(BF16)
        imp = jnp.zeros(shape, F32)
        for j in range(HPG_C):
            h = HPG_C * g + j
            bias = jnp.where(near1, near_ref[0, h:h + 1, :],
                             jnp.where(near2, near_ref[1, h:h + 1, :], cfar_ref[h:h + 1, 0:1]))
            bias = jnp.where(blk == cur, cfar_ref[h:h + 1, 1:2], bias)
            s = _dot(ck, q_t[DH * h:DH * (h + 1)].astype(BF16)) * ATT_SCALE + bias
            s = jnp.where(vis, s, NEG)
            e = jnp.exp(s - jnp.max(s, axis=0, keepdims=True))
            p = jnp.where(vis, e / jnp.sum(e, axis=0, keepdims=True), 0.0)
            outs.append(_dot(cv_t, p.astype(BF16)))
            imp = imp + p
        score = jnp.where(allowed, imp + jnp.where(forced, FORCE_BONUS, 0.0), NEG)
        sel = jnp.zeros(shape, F32)
        for _ in range(n_sel):
            m = jnp.max(score, axis=0, keepdims=True)
            idx = jnp.min(jnp.where(score == m, blk, nbp), axis=0, keepdims=True)
            hit = blk == idx
            sel = jnp.where(hit & (m > 0.5 * NEG), 1.0, sel)
            score = jnp.where(hit, PICKED, score)
        sel_ref[0, g] = sel.astype(BF16)
    ocmp_ref[...] = jnp.concatenate(outs, axis=0).T


def _cmpsel_prompt(q, cmp, near, cfar, batch, seq):
    tq = min(TQ, seq)
    nq = seq // tq
    nbp = cmp.shape[1]
    n_sel = min(TOP_N, -(-seq // NSA_BLOCK))
    return pl.pallas_call(
        functools.partial(_cmpsel_prompt_body, tq=tq, nbp=nbp, n_sel=n_sel),
        grid=(batch, nq),
        in_specs=[pl.BlockSpec((tq, 512), lambda b, i: (b * nq + i, 0)),
                  pl.BlockSpec((1, nbp, 256), lambda b, i: (b, 0, 0)),
                  pl.BlockSpec((2, H_C, tq), lambda b, i: (0, 0, 0)),
                  pl.BlockSpec((H_C, 128), lambda b, i: (0, 0))],
        out_specs=[pl.BlockSpec((tq, 512), lambda b, i: (b * nq + i, 0)),
                   pl.BlockSpec((1, G_C, nbp, tq), lambda b, i: (b, 0, 0, i))],
        out_shape=[jax.ShapeDtypeStruct((batch * seq, 512), F32),
                   jax.ShapeDtypeStruct((batch, G_C, nbp, seq), BF16)],
        compiler_params=_cparams("arbitrary", "arbitrary"),
        name="nsa_cmpsel_prompt",
    )(q, cmp, near, cfar)


def _heads_t(q_t, h0, nh):
    return jnp.concatenate([q_t[DH * h:DH * (h + 1)] for h in range(h0, h0 + nh)], axis=1).astype(BF16)


def _fill_kv(src_ref, k_col, v_col, n_groups, k_scr, vt_scr, tq):
    for g in range(n_groups):
        k_scr[g] = src_ref[0, :, k_col + DH * g:k_col + DH * (g + 1)].astype(BF16)
    for kb in range(vt_scr.shape[0]):
        vt_scr[kb] = src_ref[0, kb * tq:(kb + 1) * tq, v_col:v_col + DH * n_groups].T.astype(BF16)


def _nsa_prompt_body(q_ref, nsa_ref, win_ref, sel_ref, bdt_ref, bst_ref, cfar_ref, oslc_ref, owin_ref,
                     ks_scr, vs_scr, kw_scr, vw_scr, m_ref, l_ref, acc_ref, *, tq, nbp):
    qi = pl.program_id(1)

    @pl.when(qi == 0)
    def _():
        _fill_kv(nsa_ref, 256, 384, G_C, ks_scr, vs_scr, tq)
        _fill_kv(win_ref, 0, 128, G_C, kw_scr, vw_scr, tq)

    q_t = (q_ref[...] * ATT_SCALE).T
    per_tile = tq // NSA_BLOCK
    n_back = WINDOW // tq
    shape = (tq, HPG_C * tq)
    key_i, qry_i = _iota(shape, 0), _iota(shape, 1) & (tq - 1)
    groups = range(G_C)
    heads = [range(HPG_C * g, HPG_C * (g + 1)) for g in groups]
    q_gs = [_heads_t(q_t, HPG_C * g, HPG_C) for g in groups]
    far_rows = [jnp.concatenate([jnp.broadcast_to(cfar_ref[h:h + 1, 0:1], (1, tq)) for h in heads[g]], axis=1)
                for g in groups]

    def tile(g, kb, mode, mask_t, k_scr, vt_scr):
        k = k_scr[g, pl.ds(pl.multiple_of(kb * tq, tq), tq), :]
        if mode == "far":
            bias = far_rows[g]
        else:
            tab = bdt_ref if mode == "diag" else bst_ref
            bias = jnp.concatenate([tab[h] for h in heads[g]], axis=1)
        _tsoftmax_step(_dot(k, q_gs[g]) + bias, mask_t, vt_scr[kb, DH * g:DH * (g + 1), :],
                       m_ref.at[g], l_ref.at[g], acc_ref.at[g])

    def init():
        for g in groups:
            _softmax_init(m_ref.at[g], l_ref.at[g], acc_ref.at[g])

    def outputs():
        return [_tsoftmax_out(l_ref.at[g], acc_ref.at[g])[:, j * tq:(j + 1) * tq]
                for g in groups for j in range(HPG_C)]

    def slc_tile(kb, mode):
        expand = _iota((tq, nbp), 1) == kb * per_tile + jnp.right_shift(_iota((tq, nbp), 0), 6)
        expand = jnp.where(expand, 1.0, 0.0).astype(BF16)
        for g in groups:
            m = _dot(expand, sel_ref[0, g]) > 0.5
            if mode == "diag":
                m = m & (_iota((tq, tq), 0) <= _iota((tq, tq), 1))
            tile(g, kb, mode, jnp.concatenate([m] * HPG_C, axis=1), ks_scr, vs_scr)

    init()

    def far(kb, c):
        slc_tile(kb, "far")
        return c

    lax.fori_loop(0, jnp.maximum(qi - 1, 0), far, 0)

    @pl.when(qi >= 1)
    def _():
        slc_tile(qi - 1, "sub")

    slc_tile(qi, "diag")
    oslc_ref[...] = jnp.concatenate(outputs(), axis=0).T

    init()
    for d in range(n_back, 0, -1):
        mask_t = (key_i >= qry_i) if d == n_back else None

        @pl.when(qi >= d)
        def _(d=d, mask_t=mask_t):
            for g in groups:
                tile(g, qi - d, "sub" if d == 1 else "far", mask_t, kw_scr, vw_scr)

    for g in groups:
        tile(g, qi, "diag", key_i <= qry_i, kw_scr, vw_scr)
    owin_ref[...] = jnp.concatenate(outputs(), axis=0).T


def _nsa_prompt_attn(q, nsa, win, sel, bdt, bst, cfar, batch, seq):
    tq = bdt.shape[1]
    assert tq & (tq - 1) == 0 and seq % tq == 0 and WINDOW % tq == 0 and tq >= MAX_DISTANCE
    nq = seq // tq
    nbp = sel.shape[2]
    tok = pl.BlockSpec((tq, 512), lambda b, i: (b * nq + i, 0))
    seq_spec = lambda w: pl.BlockSpec((1, seq, w), lambda b, i: (b, 0, 0))
    full = lambda a: pl.BlockSpec(a.shape, lambda b, i: (0,) * a.ndim, pipeline_mode=pl.Buffered(1))
    r = HPG_C * tq
    kv_scratch = [pltpu.VMEM((G_C, seq, DH), BF16), pltpu.VMEM((nq, G_C * DH, tq), BF16)]
    return pl.pallas_call(
        functools.partial(_nsa_prompt_body, tq=tq, nbp=nbp),
        grid=(batch, nq),
        in_specs=[tok, seq_spec(NSA_ROW), seq_spec(WIN_ROW),
                  pl.BlockSpec((1, G_C, nbp, tq), lambda b, i: (b, 0, 0, i)), full(bdt), full(bst), full(cfar)],
        out_specs=[tok, tok],
        out_shape=[jax.ShapeDtypeStruct((batch * seq, 512), F32)] * 2,
        scratch_shapes=kv_scratch + kv_scratch + [pltpu.VMEM((G_C, 1, r), F32), pltpu.VMEM((G_C, 1, r), F32),
                                                  pltpu.VMEM((G_C, DH, r), F32)],
        compiler_params=_cparams("arbitrary", "arbitrary"),
        name="nsa_slc_win_prompt",
    )(q, nsa.reshape(batch, seq, NSA_ROW), win.reshape(batch, seq, WIN_ROW), sel, bdt, bst, cfar)


def _sb_prompt_body(q_ref, rows_ref, u_ref, o_ref, k_scr, vt_scr, carry_ref, acc_ref, *, tq):
    qi = pl.program_id(1)

    @pl.when(qi == 0)
    def _():
        _fill_kv(rows_ref, 0, 256, KV_D, k_scr, vt_scr, tq)

    q_t = (q_ref[...] * (ATT_SCALE * LOG2E)).T
    shape = (tq, HPK_D * tq)
    before = _iota(shape, 0) < (_iota(shape, 1) & (tq - 1))
    q_gs = [_heads_t(q_t, HPK_D * g, HPK_D) for g in range(KV_D)]
    carry_ref[...] = jnp.zeros(carry_ref.shape, F32)
    acc_ref[...] = jnp.zeros(acc_ref.shape, F32)

    def step(kb, diag):
        for g in range(KV_D):
            k = k_scr[g, pl.ds(pl.multiple_of(kb * tq, tq), tq), :]
            z = _dot(k, q_gs[g])
            lsz = _log2_sigmoid(z)
            lf = lsz - z
            if diag:
                lf = jnp.where(before, lf, 0.0)
            between = _dot(u_ref[...], lf.astype(BF16))
            a = jnp.exp2(lsz + between + carry_ref[g])
            if diag:
                a = jnp.where(before, a, 0.0)
            acc_ref[g] += _dot(vt_scr[kb, DH * g:DH * (g + 1), :], a.astype(BF16))
            carry_ref[g] += jnp.sum(lf, axis=0, keepdims=True)

    step(qi, True)

    def back(it, c):
        step(qi - 1 - it, False)
        return c

    lax.fori_loop(0, qi, back, 0)
    outs = [acc_ref[g][:, j * tq:(j + 1) * tq] for g in range(KV_D) for j in range(HPK_D)]
    o_ref[...] = jnp.concatenate(outs, axis=0).T


def _sb_prompt(q, rows, batch, seq):
    tq = min(TQ_MLA, seq)
    assert tq & (tq - 1) == 0 and seq % tq == 0
    nq = seq // tq
    later = np.arange(tq)[None, :] > np.arange(tq)[:, None]
    u = jnp.asarray(later, BF16)
    tok = pl.BlockSpec((tq, 512), lambda b, i: (b * nq + i, 0))
    r = HPK_D * tq
    return pl.pallas_call(
        functools.partial(_sb_prompt_body, tq=tq),
        grid=(batch, nq),
        in_specs=[tok, pl.BlockSpec((1, seq, SB_ROW), lambda b, i: (b, 0, 0)),
                  pl.BlockSpec((tq, tq), lambda b, i: (0, 0))],
        out_specs=tok,
        out_shape=jax.ShapeDtypeStruct((batch * seq, 512), F32),
        scratch_shapes=[pltpu.VMEM((KV_D, seq, DH), BF16), pltpu.VMEM((nq, KV_D * DH, tq), BF16),
                        pltpu.VMEM((KV_D, 1, r), F32), pltpu.VMEM((KV_D, DH, r), F32)],
        compiler_params=_cparams("arbitrary", "arbitrary"),
        name="sb_prompt",
    )(q, rows.reshape(batch, seq, SB_ROW), u)


def _page_specs(block, pp, col_block, page_of):
    return [pl.BlockSpec(block, lambda b, s, pt, j=j: (page_of(b, s, j, pt), 0, col_block)) for j in range(pp)]


def _mla_decode_body(pt_ref, q_ref, knew_ref, wvb_ref, *rest, pp, valid):
    pages = rest[:pp]
    o_ref, m_ref, l_ref, acc_ref = rest[pp:]
    st = pl.program_id(1)
    r = SAMPLE_ROWS
    q = q_ref[0].astype(BF16)

    @pl.when(st == 0)
    def _():
        _softmax_init(m_ref, l_ref, acc_ref)

    k_t = jnp.concatenate([pages[j][0] for j in range(pp)], axis=1).astype(BF16)
    _softmax_step(_dot(q, k_t), None, k_t[0:KV_LORA], m_ref, l_ref, acc_ref, v_is_transposed=True)

    @pl.when(st == pl.num_programs(1) - 1)
    def _():
        kn = knew_ref[0].astype(BF16)
        shape = (H_B * r, r)
        col = _iota(shape, 1)
        mask = (col <= (_iota(shape, 0) & (r - 1))) & (col < valid)
        _softmax_step(_dot_nt(q, kn), mask, kn[:, 0:KV_LORA], m_ref, l_ref, acc_ref)
        o = _softmax_out(l_ref, acc_ref)
        olat = jnp.concatenate([o[r * h:r * (h + 1)] for h in range(H_B)], axis=1)
        o_ref[0] = _dot(olat.astype(BF16), wvb_ref[...])


def _mla_decode(page_table, q, knew, wvb, cache_t, valid):
    batch, n_pages = page_table.shape
    pp = min(2 * PAGES_PER_STEP, n_pages)
    assert n_pages % pp == 0
    r = SAMPLE_ROWS
    grid_spec = pltpu.PrefetchScalarGridSpec(
        num_scalar_prefetch=1,
        grid=(batch, n_pages // pp),
        in_specs=[pl.BlockSpec((1, H_B * r, MLA_ROW), lambda b, s, pt: (b, 0, 0)),
                  pl.BlockSpec((1, r, MLA_ROW), lambda b, s, pt: (b, 0, 0)),
                  pl.BlockSpec((1024, 512), lambda b, s, pt: (0, 0))]
        + _page_specs((1, MLA_ROW, PAGE), pp, 0, lambda b, s, j, pt: pt[b, s * pp + j]),
        out_specs=pl.BlockSpec((1, r, 512), lambda b, s, pt: (b, 0, 0)),
        scratch_shapes=[pltpu.VMEM((H_B * r, 1), F32), pltpu.VMEM((H_B * r, 1), F32),
                        pltpu.VMEM((H_B * r, KV_LORA), F32)])
    return pl.pallas_call(
        functools.partial(_mla_decode_body, pp=pp, valid=valid),
        grid_spec=grid_spec,
        out_shape=jax.ShapeDtypeStruct((batch, r, 512), F32),
        compiler_params=_cparams("arbitrary", "arbitrary"),
        name="mla_decode",
    )(page_table, q, knew, wvb, *([cache_t] * pp))


def _means_decode_body(pt_ref, *rest, pp):
    pages = rest[:pp]
    o_ref = rest[pp]
    o_ref[0] = _block_means(jnp.concatenate([pages[j][0] for j in range(pp)], axis=0))


def _means_decode(page_table, cache):
    batch, n_pages = page_table.shape
    pp = min(2 * PAGES_PER_STEP, n_pages)
    assert n_pages % pp == 0
    per_page = PAGE // NSA_BLOCK
    grid_spec = pltpu.PrefetchScalarGridSpec(
        num_scalar_prefetch=1,
        grid=(batch, n_pages // pp),
        in_specs=_page_specs((1, PAGE, 256), pp, 0, lambda b, s, j, pt: pt[b, s * pp + j]),
        out_specs=pl.BlockSpec((1, pp * per_page, 256), lambda b, s, pt: (b, s, 0)))
    return pl.pallas_call(
        functools.partial(_means_decode_body, pp=pp),
        grid_spec=grid_spec,
        out_shape=jax.ShapeDtypeStruct((batch, n_pages * per_page, 256), F32),
        compiler_params=_cparams("arbitrary", "arbitrary"),
        name="nsa_means_decode",
    )(page_table, *([cache] * pp))


def _cmpsel_decode_body(q_ref, cmp_ref, cb_ref, ocmp_ref, idx_ref, *, nb, n_past_sel):
    r = SAMPLE_ROWS
    blk = _iota((1, nb), 1)
    forced = (blk == 0) | (blk == nb - 1)
    lane = _iota((r, 128), 1)
    for g in range(G_C):
        ck = cmp_ref[0, :, DH * g:DH * (g + 1)].astype(BF16)
        cv = cmp_ref[0, :, 128 + DH * g:128 + DH * (g + 1)].astype(BF16)
        imp = jnp.zeros((r, nb), F32)
        for j in range(HPG_C):
            h = HPG_C * g + j
            o, p = _cmp_head(q_ref[0, :, DH * h:DH * (h + 1)], ck, cv, cb_ref[h], None)
            ocmp_ref[0, :, DH * h:DH * (h + 1)] = o
            imp = imp + p
        score = imp + jnp.where(forced, FORCE_BONUS, 0.0)
        picked = jnp.zeros((r, 128), jnp.int32)
        for slot, (idx, _) in enumerate(_top_blocks(score, blk, n_past_sel)):
            picked = jnp.where(lane == slot, idx, picked)
        idx_ref[0, g] = picked


def _cmpsel_decode(q, cmp, cb, n_past_sel):
    batch, nb, _ = cmp.shape
    r = SAMPLE_ROWS
    return pl.pallas_call(
        functools.partial(_cmpsel_decode_body, nb=nb, n_past_sel=n_past_sel),
        grid=(batch,),
        in_specs=[pl.BlockSpec((1, r, 512), lambda b: (b, 0, 0)),
                  pl.BlockSpec((1, nb, 256), lambda b: (b, 0, 0)),
                  pl.BlockSpec((H_C, r, nb), lambda b: (0, 0, 0))],
        out_specs=[pl.BlockSpec((1, r, 512), lambda b: (b, 0, 0)),
                   pl.BlockSpec((1, G_C, r, 128), lambda b: (b, 0, 0, 0))],
        out_shape=[jax.ShapeDtypeStruct((batch, r, 512), F32),
                   jax.ShapeDtypeStruct((batch, G_C, r, 128), jnp.int32)],
        compiler_params=_cparams("arbitrary"),
        name="nsa_cmpsel_decode",
    )(q, cmp, cb)


def _dist_bias(tab3_ref, dist):
    nd = tab3_ref.shape[1]
    width = dist.shape[1]
    onehot = jnp.where(_iota((nd, width), 0) == jnp.clip(dist, 0, nd - 1), 1.0, 0.0).astype(BF16)
    b = _dot(tab3_ref[...], onehot)
    return b[0:8] + b[8:16] + b[16:24]


def _slc_decode_body(idx_ref, pt_ref, q_ref, new_ref, tab3_ref, *rest, n_slots, p_len, n_tok):
    blocks = rest[:n_tok * G_C * n_slots]
    o_ref = rest[n_tok * G_C * n_slots]
    b = pl.program_id(0)
    r = SAMPLE_ROWS
    head_row = _iota((H_C, 1), 0)
    col = _iota((H_C, r), 1)
    n_keys = n_slots * NSA_BLOCK
    lane = _iota((1, n_keys), 1)
    slot_of = jnp.right_shift(lane, 6)
    for t in range(n_tok):
        q = (q_ref[0, t] * ATT_SCALE).astype(BF16)
        out = jnp.zeros((H_C, DH), F32)
        for g in range(G_C):
            kvs = [blocks[(t * G_C + g) * n_slots + s][0] for s in range(n_slots)]
            k = jnp.concatenate([kv[:, DH * g:DH * (g + 1)] for kv in kvs], axis=0).astype(BF16)
            v = jnp.concatenate([kv[:, 128 + DH * g:128 + DH * (g + 1)] for kv in kvs], axis=0).astype(BF16)
            first = jnp.zeros((1, n_keys), jnp.int32)
            for s in range(n_slots):
                n = idx_ref[((b * G_C + g) * r + t) * n_slots + s]
                first = jnp.where(slot_of == s, p_len + t - n * NSA_BLOCK, first)
            sc = _dot_nt(q, k) + _dist_bias(tab3_ref, first - (lane & (NSA_BLOCK - 1)))
            kn = new_ref[0, :, 256 + DH * g:256 + DH * (g + 1)].astype(BF16)
            vn = new_ref[0, :, 384 + DH * g:384 + DH * (g + 1)].astype(BF16)
            cur_ok = col <= t
            sc_new = jnp.where(cur_ok, _dot_nt(q, kn) + _dist_bias(tab3_ref, t - _iota((1, r), 1)), NEG)
            m = jnp.maximum(jnp.max(sc, axis=1, keepdims=True), jnp.max(sc_new, axis=1, keepdims=True))
            p = jnp.exp(sc - m)
            p_new = jnp.where(cur_ok, jnp.exp(sc_new - m), 0.0)
            l = jnp.sum(p, axis=1, keepdims=True) + jnp.sum(p_new, axis=1, keepdims=True)
            acc = _dot(p.astype(BF16), v) + _dot(p_new.astype(BF16), vn)
            in_group = (head_row >= HPG_C * g) & (head_row < HPG_C * (g + 1))
            out = jnp.where(in_group, acc / l, out)
        o_ref[0, t] = out


def _slc_decode(idx, page_table, q, new_rows, tab3, cache, n_tok, p_len):
    batch, n_pages = page_table.shape
    r = SAMPLE_ROWS
    n_slots = idx.shape[0] // (batch * G_C * r)
    per_page = PAGE // NSA_BLOCK

    def slot_spec(t, g, s):
        def index(b, idx_ref, pt):
            n = idx_ref[((b * G_C + g) * r + t) * n_slots + s]
            return (pt[b, n // per_page], n % per_page, 1)
        return pl.BlockSpec((1, NSA_BLOCK, 256), index)

    slots = [slot_spec(t, g, s) for t in range(n_tok) for g in range(G_C) for s in range(n_slots)]
    grid_spec = pltpu.PrefetchScalarGridSpec(
        num_scalar_prefetch=2,
        grid=(batch,),
        in_specs=[pl.BlockSpec((1, n_tok, H_C, DH), lambda b, i, pt: (b, 0, 0, 0)),
                  pl.BlockSpec((1, r, NSA_ROW), lambda b, i, pt: (b, 0, 0)),
                  pl.BlockSpec(tab3.shape, lambda b, i, pt: (0, 0))] + slots,
        out_specs=pl.BlockSpec((1, n_tok, H_C, DH), lambda b, i, pt: (b, 0, 0, 0)))
    return pl.pallas_call(
        functools.partial(_slc_decode_body, n_slots=n_slots, p_len=p_len, n_tok=n_tok),
        grid_spec=grid_spec,
        out_shape=jax.ShapeDtypeStruct((batch, n_tok, H_C, DH), F32),
        compiler_params=_cparams("arbitrary"),
        name="nsa_slc_decode",
    )(idx, page_table, q, new_rows, tab3, *([cache] * len(slots)))


def _win_decode_body(q_ref, state_ref, new_ref, wbs_ref, wbn_ref, o_ref, *, valid):
    r = SAMPLE_ROWS
    wb = state_ref.shape[1]
    for g in range(G_C):
        q4 = (jnp.concatenate([q_ref[0, :, DH * h:DH * (h + 1)] for h in range(HPG_C * g, HPG_C * (g + 1))],
                              axis=0) * ATT_SCALE).astype(BF16)
        ks = state_ref[0, :, DH * g:DH * (g + 1)].astype(BF16)
        vs = state_ref[0, :, 128 + DH * g:128 + DH * (g + 1)].astype(BF16)
        kn = new_ref[0, :, DH * g:DH * (g + 1)].astype(BF16)
        vn = new_ref[0, :, 128 + DH * g:128 + DH * (g + 1)].astype(BF16)
        tok = _iota((HPG_C * r, 1), 0) & (r - 1)
        ok_s = _iota((HPG_C * r, wb), 1) >= tok + (wb - WINDOW)
        col_n = _iota((HPG_C * r, r), 1)
        ok_n = (col_n <= tok) & (col_n < valid)
        bias_s = jnp.concatenate([wbs_ref[h] for h in range(HPG_C * g, HPG_C * (g + 1))], axis=0)
        bias_n = jnp.concatenate([wbn_ref[h] for h in range(HPG_C * g, HPG_C * (g + 1))], axis=0)
        s_s = jnp.where(ok_s, _dot_nt(q4, ks) + bias_s, NEG)
        s_n = jnp.where(ok_n, _dot_nt(q4, kn) + bias_n, NEG)
        m = jnp.maximum(jnp.max(s_s, axis=1, keepdims=True), jnp.max(s_n, axis=1, keepdims=True))
        p_s = jnp.where(ok_s, jnp.exp(s_s - m), 0.0)
        p_n = jnp.where(ok_n, jnp.exp(s_n - m), 0.0)
        l = jnp.sum(p_s, axis=1, keepdims=True) + jnp.sum(p_n, axis=1, keepdims=True)
        o = (_dot(p_s.astype(BF16), vs) + _dot(p_n.astype(BF16), vn)) / l
        for j in range(HPG_C):
            h = HPG_C * g + j
            o_ref[0, :, DH * h:DH * (h + 1)] = o[r * j:r * (j + 1)]


def _win_decode(q, state, new_rows, wbs, wbn, valid):
    batch, wb, _ = state.shape
    r = SAMPLE_ROWS
    return pl.pallas_call(
        functools.partial(_win_decode_body, valid=valid),
        grid=(batch,),
        in_specs=[pl.BlockSpec((1, r, 512), lambda b: (b, 0, 0)),
                  pl.BlockSpec((1, wb, WIN_ROW), lambda b: (b, 0, 0)),
                  pl.BlockSpec((1, r, WIN_ROW), lambda b: (b, 0, 0)),
                  pl.BlockSpec((H_C, r, wb), lambda b: (0, 0, 0)),
                  pl.BlockSpec((H_C, r, r), lambda b: (0, 0, 0))],
        out_specs=pl.BlockSpec((1, r, 512), lambda b: (b, 0, 0)),
        out_shape=jax.ShapeDtypeStruct((batch, r, 512), F32),
        compiler_params=_cparams("arbitrary"),
        name="nsa_win_decode",
    )(q, state, new_rows, wbs, wbn)


def _suffix_sum(x, block=None):
    n = x.shape[0]
    block = block or n
    row = _iota(x.shape, 0) & (block - 1)
    sh = 1
    while sh < block:
        x = x + jnp.where(row < block - sh, pltpu.roll(x, n - sh, axis=0), 0.0)
        sh *= 2
    return x


def _sb_decode_body(pt_ref, wq_ref, new_ref, *rest, pp, valid):
    pages = rest[:pp]
    o_ref, carry_ref, acc_ref, a_ref, v_ref = rest[pp:]
    st = pl.program_id(1)
    r = SAMPLE_ROWS
    half = 64
    wq = wq_ref[0].astype(BF16)
    lane = _iota((1, 128), 1)

    @pl.when(st == 0)
    def _():
        kn = new_ref[0]
        kk = kn[:, 0:256].astype(BF16)
        z = _dot(jnp.concatenate([kk, kk], axis=1), wq)
        lsz = _log2_sigmoid(z)
        row = _iota((r, 128), 0)
        mask = (row < (_iota((r, 128), 1) & (r - 1))) & (row < valid)
        lf = jnp.where(mask, lsz - z, 0.0)
        a = jnp.where(mask, jnp.exp2(lsz + _suffix_sum(lf) - lf), 0.0)
        pad = lambda x: jnp.concatenate([x, jnp.zeros((PAGE - r, x.shape[1]), x.dtype)], axis=0)
        vn = jnp.concatenate([kn[:, 256:512], jnp.zeros((r, 256), F32)], axis=1)
        acc_ref[...] = _dot_tn(pad(a).astype(BF16), pad(vn).astype(BF16))
        carry_ref[...] = jnp.sum(lf, axis=0, keepdims=True)

    carry = carry_ref[...]
    for i in range(pp // 2 - 1, -1, -1):
        early, late = pages[2 * i][0], pages[2 * i + 1][0]
        z = _dot(jnp.concatenate([early[:, 0:256], late[:, 0:256]], axis=1).astype(BF16), wq)
        lsz = _log2_sigmoid(z)
        lf = lsz - z
        suf = _suffix_sum(lf)
        tot = suf[0:1]
        swapped = pltpu.roll(jnp.broadcast_to(tot, (8, 128)), half, axis=1)[0:1]
        between = (suf - lf) + (carry + jnp.where(lane < half, swapped, 0.0))
        a_ref[PAGE * i:PAGE * (i + 1), :] = jnp.exp2(lsz + between).astype(BF16)
        v_ref[PAGE * i:PAGE * (i + 1), :] = jnp.concatenate([early[:, 256:512], late[:, 256:512]],
                                                            axis=1).astype(BF16)
        carry = carry + tot + swapped
    carry_ref[...] = carry
    acc_ref[...] += _dot_tn(a_ref[...], v_ref[...])

    @pl.when(st == pl.num_programs(1) - 1)
    def _():
        o_ref[0] = acc_ref[...]


def _sb_decode(page_table, wq, new_rows, cache, valid):
    batch, n_pages = page_table.shape
    pp = min(PAGES_PER_STEP, n_pages)
    assert n_pages % pp == 0 and pp % 2 == 0
    r = SAMPLE_ROWS
    grid_spec = pltpu.PrefetchScalarGridSpec(
        num_scalar_prefetch=1,
        grid=(batch, n_pages // pp),
        in_specs=[pl.BlockSpec((1, 512, 128), lambda b, s, pt: (b, 0, 0)),
                  pl.BlockSpec((1, r, SB_ROW), lambda b, s, pt: (b, 0, 0))]
        + _page_specs((1, PAGE, SB_ROW), pp, 0, lambda b, s, j, pt: pt[b, n_pages - (s + 1) * pp + j]),
        out_specs=pl.BlockSpec((1, 128, 512), lambda b, s, pt: (b, 0, 0)),
        scratch_shapes=[pltpu.VMEM((1, 128), F32), pltpu.VMEM((128, 512), F32),
                        pltpu.VMEM((pp // 2 * PAGE, 128), BF16), pltpu.VMEM((pp // 2 * PAGE, 512), BF16)])
    return pl.pallas_call(
        functools.partial(_sb_decode_body, pp=pp, valid=valid),
        grid_spec=grid_spec,
        out_shape=jax.ShapeDtypeStruct((batch, 128, 512), F32),
        compiler_params=_cparams("arbitrary", "arbitrary"),
        name="sb_decode",
    )(page_table, wq, new_rows, *([cache] * pp))


def _rope_tables(pos):
    half = ROPE_D // 2
    inv = ROPE_BASE ** (-jnp.arange(half, dtype=F32) / half)
    ang = pos.astype(F32)[:, None] * inv[None, :]
    cos, sin = jnp.cos(ang), jnp.sin(ang)
    return jnp.tile(cos, (1, 8)), jnp.tile(sin, (1, 8))


def _rot_cols(w):
    half = ROPE_D // 2
    return jnp.concatenate([-w[..., half:], w[..., :half]], axis=-1)


def _prep_even(w_in, w_qb, w_kb, w_vb):
    zkr = w_in[:, 2432:2464]
    w_e = jnp.concatenate([w_in[:, :2432], jnp.tile(zkr, (1, 4)), jnp.tile(_rot_cols(zkr), (1, 4))], axis=1)
    wqb = w_qb.reshape(Q_LORA, H_B, NOPE + ROPE_D)
    wr = wqb[:, :, NOPE:]
    w_q = jnp.concatenate([wqb[:, :, :NOPE].reshape(Q_LORA, 512), wr.reshape(Q_LORA, 256),
                           _rot_cols(wr).reshape(Q_LORA, 256)], axis=1)
    eye = jnp.eye(H_B, dtype=F32)
    wkb = jnp.einsum('hcn,hg->hngc', w_kb, eye).reshape(H_B * NOPE, H_B * KV_LORA)
    wvb = jnp.einsum('hcd,hg->hcgd', w_vb, eye).reshape(H_B * KV_LORA, H_B * VD_B)
    return w_e.astype(BF16), w_q.astype(BF16), wkb.astype(BF16), wvb.astype(BF16)


def _prep_odd(w_in):
    w = jnp.concatenate([w_in[:, :1280], w_in[:, 1304:2328], w_in[:, 1280:1304],
                         jnp.zeros((D_MODEL, ODD_COLS - 2328), w_in.dtype)], axis=1)
    return w.astype(BF16)


def _gate_expand():
    e = np.zeros((128, 3 * 512), np.float32)
    for h in range(H_C):
        for j in range(3):
            e[3 * h + j, 512 * j + DH * h:512 * j + DH * (h + 1)] = 1.0
    return jnp.asarray(e, BF16)


def _hgrn_params(hgrn_lb, norm_g, lj):
    lb_all = jax.nn.softmax(hgrn_lb.astype(F32), axis=0)
    lb = (jnp.cumsum(lb_all, axis=0) - lb_all[0])[lj]
    lb_pos = lb > 0
    rows = [jnp.log(jnp.where(lb_pos, lb, 1.0)), jnp.log1p(-lb), lb_pos.astype(F32), 1.0 - lb, norm_g[lj]]
    return jnp.concatenate([jnp.stack(rows), jnp.zeros((3, 512), F32)], axis=0)


def _t5_bucket(dist):
    exact = NUM_BUCKETS // 2
    d = jnp.maximum(dist, 0)
    large = exact + (jnp.log(jnp.maximum(d, 1).astype(F32) / exact)
                     / math.log(MAX_DISTANCE / exact) * (NUM_BUCKETS - exact)).astype(jnp.int32)
    return jnp.where(d < exact, d, jnp.minimum(large, NUM_BUCKETS - 1))


N_DIST = 256


def _toeplitz_t(tab, offset, n):
    i = jnp.arange(n, dtype=jnp.int32)[None, :]
    j = jnp.arange(n, dtype=jnp.int32)[:, None]
    idx = jnp.clip(offset + i - j, 0, N_DIST - 1).reshape(1, n * n)
    onehot = (idx == jnp.arange(N_DIST, dtype=jnp.int32)[:, None]).astype(F32)
    return jnp.dot(tab, onehot, precision=lax.Precision.HIGHEST).reshape(H_C, n, n)


def _decode_bias_rows(tab, base, n_keys, r):
    far = jnp.broadcast_to(tab[:, N_DIST - 1:], (H_C, n_keys + r))
    rev = jnp.concatenate([far, tab[:, ::-1], jnp.zeros((H_C, n_keys + r), F32)], axis=1)
    rows = []
    for t in range(r):
        start = n_keys + r + N_DIST - 1 - (base + t)
        rows.append(rev[:, start:start + n_keys])
    return jnp.stack(rows, axis=1)


def _pad_rows(a, batch, n_tok):
    a = a.reshape(batch, n_tok, -1)
    return jnp.pad(a, ((0, 0), (0, SAMPLE_ROWS - n_tok), (0, 0)))


def _last_rows(rows, n):
    t = rows.shape[1]
    if t < n:
        rows = jnp.pad(rows, ((0, 0), (n - t, 0), (0, 0)))
    return rows[:, rows.shape[1] - n:]


def _run_prompt(x, mod, w, tabs, batch, seq, win_len):
    assert seq % HGRN_CHUNK == 0 and (batch * seq) % _tile_rows(batch * seq) == 0
    tq = min(TQ, seq)
    pos = np.arange(seq)
    cos, sin = _rope_tables(jnp.asarray(pos, jnp.int32))
    tab = tabs['tab']
    bdt, bst = _toeplitz_t(tab, 0, tq), _toeplitz_t(tab, tq, tq)
    nblk = seq // NSA_BLOCK
    nbp = -(-nblk // 16) * 16
    reps = tq // NSA_BLOCK
    near = jnp.stack([jnp.tile(tab[:, 1:NSA_BLOCK + 1], (1, reps)),
                      jnp.tile(tab[:, NSA_BLOCK + 1:2 * NSA_BLOCK + 1], (1, reps))])
    states = []
    for l in range(DEPTH):
        lj = l // 2
        m = mod[l]
        x = _ffn(x, m, w['norm_g'][l], w['wg'], w['wu'], w['wd'], l, 0, 0, seq)
        if l % 2 == 0:
            w_e, w_q, wkb, wvb = w['even'][lj]
            zh, kfull, rows, qlat, qrope = _inproj_even(
                x, m, w['norm_g'][l], w_e, cos, sin, w['gq'][lj], w['gkv'][lj], w_q, wkb, seq)
            tb = min(TM_DENSE, seq)
            o_a, s_new = _hgrn(zh, w['hgrn_par'][lj], batch, seq, tb, HGRN_CHUNK, HGRN_SUB, HGRN_CHUNK)
            o_b = _mla_prompt(qlat, qrope, kfull, wvb.T, batch, seq)
            x = _outproj(x, m, [o_a, o_b], [], w['wout_even'][lj], seq, odd=False)
            states += [rows.reshape(batch, seq, MLA_ROW), s_new]
        else:
            qn, nsa, win, qs, sb, zg = _inproj_odd(x, m, w['norm_g'][l], w['odd'][lj], seq)
            cmp = _means_prompt(nsa).reshape(batch, nblk, 256)
            cmp = jnp.pad(cmp, ((0, 0), (0, nbp - nblk), (0, 0)))
            o_cmp, sel = _cmpsel_prompt(qn, cmp, near, tabs['cfar'], batch, seq)
            o_slc, o_win = _nsa_prompt_attn(qn, nsa, win, sel, bdt, bst, tabs['cfar'], batch, seq)
            o_sb = _sb_prompt(qs, sb, batch, seq)
            x = _outproj(x, m, [o_cmp, o_slc, o_win, zg, o_sb], [w['gate_e']], w['wout_odd'][lj], seq, odd=True)
            states += [nsa.reshape(batch, seq, NSA_ROW), _last_rows(win.reshape(batch, seq, WIN_ROW), win_len),
                       sb.reshape(batch, seq, SB_ROW)]
        x = _ffn(x, m, w['norm_g'][l], w['wg'], w['wu'], w['wd'], l, 1, 2, seq,
                 final_g=w['final_g'] if l == DEPTH - 1 else None)
    return x.reshape(batch, seq, D_MODEL), states


def _run_sample(x, mod, w, tabs, batch, n_tok, page_table, pasts):
    n = batch * n_tok
    n_pages = page_table.shape[1]
    p_len = n_pages * PAGE
    r = SAMPLE_ROWS
    assert n_tok <= r and n_tok <= NSA_BLOCK and n == _tile_rows(n)
    pos = p_len + np.arange(r)
    cos, sin = _rope_tables(jnp.asarray(np.tile(pos[:n_tok], batch), jnp.int32))
    tab = tabs['tab']
    nb = p_len // NSA_BLOCK
    n_past_sel = min(TOP_N, nb + 1) - 1
    assert nb >= 2 and p_len % NSA_BLOCK == 0
    cb = jnp.concatenate([jnp.broadcast_to(tab[:, None, N_DIST - 1:], (H_C, r, nb - 2)),
                          tab[:, NSA_BLOCK + 1:NSA_BLOCK + 1 + r, None], tab[:, 1:1 + r, None]], axis=2)
    states = []
    for l in range(DEPTH):
        lj = l // 2
        m = mod[l]
        x = _ffn(x, m, w['norm_g'][l], w['wg'], w['wu'], w['wd'], l, 0, 0, n)
        if l % 2 == 0:
            cache_mla, state_hgrn = pasts[l]
            w_e, w_q, wkb, wvb = w['even'][lj]
            zh, kfull, rows, qlat, qrope = _inproj_even(
                x, m, w['norm_g'][l], w_e, cos, sin, w['gq'][lj], w['gkv'][lj], w_q, wkb, n)
            zh8 = _pad_rows(zh, batch, n_tok).reshape(batch * r, 2048)
            o_a8, s_new = _hgrn(zh8, w['hgrn_par'][lj], batch, r, r, r, r, n_tok, s0=state_hgrn)
            o_a = o_a8.reshape(batch, r, 512)[:, :n_tok].reshape(n, 512)
            qf = jnp.concatenate([qlat.reshape(batch, n_tok, H_B, KV_LORA),
                                  qrope.reshape(batch, n_tok, H_B, ROPE_D)], axis=-1)
            qf = jnp.pad(qf, ((0, 0), (0, r - n_tok), (0, 0), (0, 0)))
            qf = jnp.swapaxes(qf, 1, 2).reshape(batch, H_B * r, MLA_ROW)
            o_b8 = _mla_decode(page_table, qf, _pad_rows(rows, batch, n_tok), wvb, jnp.swapaxes(cache_mla, 1, 2),
                               n_tok)
            o_b = o_b8[:, :n_tok].reshape(n, 512)
            x = _outproj(x, m, [o_a, o_b], [], w['wout_even'][lj], n, odd=False)
            states += [rows.reshape(batch, n_tok, MLA_ROW), s_new]
        else:
            cache_nsa, state_win, cache_sb = pasts[l]
            wb = state_win.shape[1]
            assert p_len >= wb and wb <= WINDOW
            qn, nsa, win, qs, sb, zg = _inproj_odd(x, m, w['norm_g'][l], w['odd'][lj], n)
            qn8 = _pad_rows(qn, batch, n_tok)
            nsa8 = _pad_rows(nsa, batch, n_tok)
            cmp = _means_decode(page_table, cache_nsa)
            o_cmp8, picked = _cmpsel_decode(qn8, cmp, cb, n_past_sel)
            idx = picked[..., :n_past_sel].reshape(-1)
            o_slc = _slc_decode(idx, page_table, qn.reshape(batch, n_tok, H_C, DH), nsa8, tabs['tab3'],
                                cache_nsa, n_tok, p_len)
            wbs = _decode_bias_rows(tab, wb, wb, r)
            wbn = _decode_bias_rows(tab, 0, r, r)
            o_win8 = _win_decode(qn8, state_win, _pad_rows(win, batch, n_tok), wbs, wbn, n_tok)
            q5 = _pad_rows(qs * (ATT_SCALE * LOG2E), batch, n_tok).reshape(batch, r, KV_D, HPK_D, DH)
            q5 = jnp.transpose(q5, (0, 2, 4, 3, 1)).reshape(batch, KV_D, DH, HPK_D * r)
            wq = jnp.einsum('bgdc,gh->bgdhc', q5, jnp.eye(KV_D, dtype=F32)).reshape(batch, KV_D * DH, KV_D * HPK_D * r)
            nc = KV_D * HPK_D * r
            zc = jnp.zeros_like(wq)
            wq = jnp.concatenate([jnp.concatenate([wq, zc], axis=2), jnp.concatenate([zc, wq], axis=2)], axis=1)
            o_raw = _sb_decode(page_table, wq, _pad_rows(sb, batch, n_tok), cache_sb, n_tok)
            o6 = (o_raw[:, :nc, :256] + o_raw[:, nc:, 256:]).reshape(batch, KV_D, HPK_D, r, KV_D, DH)
            o_sb = jnp.stack([o6[:, g, :, :, g] for g in range(KV_D)], axis=1)
            o_sb = jnp.transpose(o_sb, (0, 3, 1, 2, 4))[:, :n_tok].reshape(n, 512)
            take = lambda a: a[:, :n_tok].reshape(n, 512)
            x = _outproj(x, m, [take(o_cmp8), o_slc.reshape(n, 512), take(o_win8), zg, o_sb], [w['gate_e']],
                         w['wout_odd'][lj], n, odd=True)
            new_win = jnp.concatenate([state_win, win.reshape(batch, n_tok, WIN_ROW)], axis=1)[:, n_tok:]
            states += [nsa.reshape(batch, n_tok, NSA_ROW), new_win, sb.reshape(batch, n_tok, SB_ROW)]
        x = _ffn(x, m, w['norm_g'][l], w['wg'], w['wu'], w['wd'], l, 1, 2, n,
                 final_g=w['final_g'] if l == DEPTH - 1 else None)
    return x.reshape(batch, n_tok, D_MODEL), states


def kernel(x_prompt, x_sample, cache_mla_l0, state_hgrn_l0, cache_nsa_l1, state_win_l1, cache_sb_l1, cache_mla_l2, state_hgrn_l2, cache_nsa_l3, state_win_l3, cache_sb_l3, page_table, c_prompt, c_sample, w_ada, b_ada, norm_g, ffn_w_gate, ffn_w_up, ffn_w_down, w_in_even, w_out_even, hgrn_lb, hgrn_norm_g, mla_q_norm_g, mla_kv_norm_g, mla_w_qb, mla_w_kb, mla_w_vb, w_in_odd, w_out_odd, rel_bias, final_norm_g):
    bp, seq, _ = x_prompt.shape
    bs, n_tok, _ = x_sample.shape
    n_even = w_in_even.shape[0]
    n_odd = w_in_odd.shape[0]
    w = dict(
        norm_g=norm_g, final_g=final_norm_g,
        wg=ffn_w_gate.astype(BF16), wu=ffn_w_up.astype(BF16), wd=ffn_w_down.astype(BF16),
        even=[_prep_even(w_in_even[j], mla_w_qb[j], mla_w_kb[j], mla_w_vb[j]) for j in range(n_even)],
        odd=[_prep_odd(w_in_odd[j]) for j in range(n_odd)],
        wout_even=w_out_even.astype(BF16), wout_odd=w_out_odd.astype(BF16),
        gq=mla_q_norm_g.reshape(n_even, 1, Q_LORA), gkv=mla_kv_norm_g.reshape(n_even, 1, KV_LORA),
        hgrn_par=[_hgrn_params(hgrn_lb, hgrn_norm_g, j) for j in range(n_even)],
        gate_e=_gate_expand())
    tab = rel_bias[_t5_bucket(jnp.arange(N_DIST, dtype=jnp.int32))].T.astype(F32)
    cfar = jnp.concatenate([tab[:, N_DIST - 1:], tab[:, 0:1], jnp.zeros((H_C, 126), F32)], axis=1)
    tabs = dict(tab=tab, cfar=cfar, tab3=jnp.concatenate(_split3(tab), axis=0))
    mod = _ada_mod(jnp.concatenate([c_prompt, c_sample], axis=0), w_ada, b_ada)
    mod = mod.reshape(DEPTH, bp + bs, N_MOD, D_MODEL)
    mod_p = mod[:, :bp, :, None, :]
    mod_s = jnp.transpose(jnp.repeat(mod[:, bp:], n_tok, axis=1), (0, 2, 1, 3))[:, None]
    win_len = state_win_l1.shape[1]
    y_p, st_p = _run_prompt(x_prompt.reshape(bp * seq, D_MODEL), mod_p, w, tabs, bp, seq, win_len)
    pasts = [(cache_mla_l0, state_hgrn_l0), (cache_nsa_l1, state_win_l1, cache_sb_l1),
             (cache_mla_l2, state_hgrn_l2), (cache_nsa_l3, state_win_l3, cache_sb_l3)]
    y_s, st_s = _run_sample(x_sample.reshape(bs * n_tok, D_MODEL), mod_s, w, tabs, bs, n_tok, page_table, pasts)
    out = [y_p, y_s]
    for a, b in zip(st_p, st_s):
        out += [a, b]
    return tuple(out)
```

```python
import functools
import math

import numpy as np
import jax
import jax.numpy as jnp
from jax import lax
from jax.experimental import pallas as pl
from jax.experimental.pallas import tpu as pltpu

F32 = jnp.float32
BF16 = jnp.bfloat16

D_MODEL = 1024
DEPTH = 4
PAGE = 128
NORM_EPS = 1e-6
NEG = -1e30
PICKED = -3e38
N_MOD = 9
D_FF = 2816

H_A = 4
DK_A = 128
HGRN_CHUNK = 64
HGRN_SUB = 16

H_B = 8
Q_LORA = 256
KV_LORA = 128
NOPE = 64
ROPE_D = 32
VD_B = 64
ROPE_BASE = 10000.0
MLA_ROW = KV_LORA + ROPE_D
MLA_SCALE = (NOPE + ROPE_D) ** -0.5

DH = 64
H_C = 8
G_C = 2
HPG_C = 4
NSA_BLOCK = 64
TOP_N = 8
WINDOW = 512
FORCE_BONUS = 100.0
NSA_ROW = 512
WIN_ROW = 256
H_D = 8
KV_D = 4
HPK_D = 2
SB_ROW = 512
ATT_SCALE = DH ** -0.5
NUM_BUCKETS = 32
MAX_DISTANCE = 128

V7X_VMEM_BYTES = 64 * 1024 * 1024
VMEM_LIMIT = V7X_VMEM_BYTES - 8 * 1024 * 1024

TM_DENSE = 512
TM_FFN = 1024
FF_CHUNK = 1408
TQ = 256
TQ_MLA = 512
SAMPLE_ROWS = 8
PAGES_PER_STEP = 16


def _cparams(*sem):
    return pltpu.CompilerParams(dimension_semantics=sem, vmem_limit_bytes=VMEM_LIMIT)


def _iota(shape, dim):
    return lax.broadcasted_iota(jnp.int32, shape, dim)


def _rms(x):
    return x * lax.rsqrt(jnp.mean(x * x, axis=-1, keepdims=True) + NORM_EPS)


def _silu(x):
    return x * jax.nn.sigmoid(x)


def _softplus_neg_abs(x):
    return jnp.log(1.0 + jnp.exp(-jnp.abs(x)))


def _log_sigmoid(x):
    return jnp.minimum(x, 0.0) - _softplus_neg_abs(x)


LOG2E = math.log2(math.e)


def _log2_sigmoid(x2):
    return jnp.minimum(x2, 0.0) - jnp.log2(1.0 + jnp.exp2(-jnp.abs(x2)))


def _dot(a, b):
    return jnp.dot(a, b, preferred_element_type=F32)


def _dot_nt(a, b):
    return lax.dot_general(a, b, (((1,), (1,)), ((), ())), preferred_element_type=F32)


def _dot_tn(a, b):
    return lax.dot_general(a, b, (((0,), (0,)), ((), ())), preferred_element_type=F32)


def _split3(x):
    hi = x.astype(BF16)
    r = x - hi.astype(F32)
    mid = r.astype(BF16)
    lo = (r - mid.astype(F32)).astype(BF16)
    return hi, mid, lo


def _ada_body(c_ref, w_ref, b_ref, o_ref):
    h = _silu(c_ref[...]).astype(BF16)
    o_ref[0] = _dot(h, w_ref[0].astype(BF16)) + b_ref[0]


def _ada_mod(c_all, w_ada, b_ada):
    nc = c_all.shape[0]
    depth, _, ncol = w_ada.shape
    tn = 1024
    return pl.pallas_call(
        _ada_body,
        grid=(depth, ncol // tn),
        in_specs=[pl.BlockSpec((nc, D_MODEL), lambda l, j: (0, 0)),
                  pl.BlockSpec((1, D_MODEL, tn), lambda l, j: (l, 0, j)),
                  pl.BlockSpec((1, 1, tn), lambda l, j: (l, 0, j))],
        out_specs=pl.BlockSpec((1, nc, tn), lambda l, j: (l, 0, j)),
        out_shape=jax.ShapeDtypeStruct((depth, nc, ncol), F32),
        compiler_params=_cparams("arbitrary", "arbitrary"),
        name="ada_mod",
    )(c_all, w_ada, b_ada.reshape(depth, 1, ncol))


def _tile_rows(n):
    return min(TM_DENSE, n)


def _mod_spec(mod, tm, seq_len):
    s, _, r, _ = mod.shape
    if r == 1:
        tiles_per_seq = seq_len // tm
        return pl.BlockSpec((1, N_MOD, 1, D_MODEL), lambda i: (i // tiles_per_seq, 0, 0, 0))
    return pl.BlockSpec((1, N_MOD, tm, D_MODEL), lambda i: (0, 0, i, 0))


def _prenorm(x, mod_ref, g_ref, sub):
    shift = mod_ref[0, 3 * sub]
    scale = mod_ref[0, 3 * sub + 1]
    return _rms(x) * g_ref[sub:sub + 1, :] * (1.0 + scale) + shift


def _const_spec(shape):
    nd = len(shape)
    return pl.BlockSpec(shape, lambda i: (0,) * nd, pipeline_mode=pl.Buffered(1))


def _ffn_body(x_ref, mod_ref, g_ref, wg_ref, wu_ref, wd_ref, *rest, sub, final):
    o_ref = rest[-1]
    x = x_ref[...]
    h = _prenorm(x, mod_ref, g_ref, sub).astype(BF16)
    acc = None
    for c0 in range(0, D_FF, FF_CHUNK):
        a = _dot(h, wg_ref[:, c0:c0 + FF_CHUNK])
        u = _dot(h, wu_ref[:, c0:c0 + FF_CHUNK])
        t = (_silu(a) * u).astype(BF16)
        part = _dot(t, wd_ref[c0:c0 + FF_CHUNK, :])
        acc = part if acc is None else acc + part
    y = x + 0.5 * mod_ref[0, 3 * sub + 2] * acc
    if final:
        y = _rms(y) * rest[0][...]
    o_ref[...] = y


def _ffn(x, mod, norm_g, wg, wu, wd, layer, which, sub, seq_len, final_g=None):
    n = x.shape[0]
    tm = min(TM_FFN, n)
    final = final_g is not None
    pick = lambda r, c: pl.BlockSpec((None, None, r, c), lambda i: (layer, which, 0, 0),
                                     pipeline_mode=pl.Buffered(1))
    in_specs = [pl.BlockSpec((tm, D_MODEL), lambda i: (i, 0)),
                _mod_spec(mod, tm, seq_len),
                _const_spec((3, D_MODEL)),
                pick(D_MODEL, D_FF), pick(D_MODEL, D_FF), pick(D_FF, D_MODEL)]
    args = [x, mod, norm_g, wg, wu, wd]
    if final:
        in_specs.append(_const_spec((1, D_MODEL)))
        args.append(final_g.reshape(1, D_MODEL))
    return pl.pallas_call(
        functools.partial(_ffn_body, sub=sub, final=final),
        grid=(n // tm,),
        in_specs=in_specs,
        out_specs=pl.BlockSpec((tm, D_MODEL), lambda i: (i, 0)),
        out_shape=jax.ShapeDtypeStruct((n, D_MODEL), F32),
        compiler_params=_cparams("arbitrary"),
        name="ffn",
    )(*args)


EVEN_COLS = 2688
ODD_COLS = 2432


def _inproj_even_body(x_ref, mod_ref, g_ref, w_ref, cos_ref, sin_ref, gq_ref, gkv_ref, wq_ref, wkb_ref,
                      zh_ref, kfull_ref, rows_ref, qlat_ref, qrope_ref):
    h = _prenorm(x_ref[...], mod_ref, g_ref, 1).astype(BF16)
    zh_ref[...] = _dot(h, w_ref[:, 0:2048])
    z = _dot(h, w_ref[:, 2048:EVEN_COLS])
    cos = cos_ref[...]
    sin = sin_ref[...]
    ckv = _rms(z[:, 256:384]) * gkv_ref[...]
    krope = z[:, 384:512] * cos + z[:, 512:640] * sin
    kfull_ref[:, 0:128] = ckv
    kfull_ref[:, 128:256] = krope
    rows_ref[:, 0:128] = ckv
    rows_ref[:, 128:MLA_ROW] = krope[:, 0:ROPE_D]
    qa = (_rms(z[:, 0:256]) * gq_ref[...]).astype(BF16)
    qz = _dot(qa, wq_ref[...])
    qlat_ref[...] = _dot(qz[:, 0:512].astype(BF16), wkb_ref[...]) * MLA_SCALE
    cos2 = jnp.concatenate([cos, cos], axis=1)
    sin2 = jnp.concatenate([sin, sin], axis=1)
    qrope_ref[...] = (qz[:, 512:768] * cos2 + qz[:, 768:1024] * sin2) * MLA_SCALE


def _inproj_even(x, mod, norm_g, w, cos, sin, gq, gkv, wq, wkb, seq_len):
    n = x.shape[0]
    tm = _tile_rows(n)
    tab_tiles = cos.shape[0] // tm
    tok = lambda c: pl.BlockSpec((tm, c), lambda i: (i, 0))
    tab = pl.BlockSpec((tm, 128), lambda i: (i % tab_tiles, 0))
    return pl.pallas_call(
        _inproj_even_body,
        grid=(n // tm,),
        in_specs=[tok(D_MODEL), _mod_spec(mod, tm, seq_len), _const_spec((3, D_MODEL)),
                  _const_spec((D_MODEL, EVEN_COLS)), tab, tab,
                  _const_spec((1, Q_LORA)), _const_spec((1, KV_LORA)),
                  _const_spec((Q_LORA, 1024)), _const_spec((512, 1024))],
        out_specs=[tok(2048), tok(256), tok(MLA_ROW), tok(1024), tok(256)],
        out_shape=[jax.ShapeDtypeStruct((n, c), F32) for c in (2048, 256, MLA_ROW, 1024, 256)],
        compiler_params=_cparams("arbitrary"),
        name="inproj_even",
    )(x, mod, norm_g, w, cos, sin, gq, gkv, wq, wkb)


def _inproj_odd_body(x_ref, mod_ref, g_ref, w_ref, qn_ref, nsa_ref, win_ref, qs_ref, sb_ref, zg_ref):
    h = _prenorm(x_ref[...], mod_ref, g_ref, 1).astype(BF16)
    qn_ref[...] = _dot(h, w_ref[:, 0:512])
    nsa_ref[...] = _dot(h, w_ref[:, 512:1024])
    win_ref[...] = _dot(h, w_ref[:, 1024:1280])
    qs_ref[...] = _dot(h, w_ref[:, 1280:1792])
    sb_ref[...] = _dot(h, w_ref[:, 1792:2304])
    zg_ref[...] = _dot(h, w_ref[:, 2304:ODD_COLS])


def _inproj_odd(x, mod, norm_g, w, seq_len):
    n = x.shape[0]
    tm = _tile_rows(n)
    tok = lambda c: pl.BlockSpec((tm, c), lambda i: (i, 0))
    cols = (512, NSA_ROW, WIN_ROW, 512, SB_ROW, 128)
    return pl.pallas_call(
        _inproj_odd_body,
        grid=(n // tm,),
        in_specs=[tok(D_MODEL), _mod_spec(mod, tm, seq_len), _const_spec((3, D_MODEL)),
                  _const_spec((D_MODEL, ODD_COLS))],
        out_specs=[tok(c) for c in cols],
        out_shape=[jax.ShapeDtypeStruct((n, c), F32) for c in cols],
        compiler_params=_cparams("arbitrary"),
        name="inproj_odd",
    )(x, mod, norm_g, w)


def _outproj_even_body(x_ref, mod_ref, oa_ref, ob_ref, w_ref, o_ref):
    mix = (_dot(oa_ref[...].astype(BF16), w_ref[0:512, :])
           + _dot(ob_ref[...].astype(BF16), w_ref[512:1024, :]))
    o_ref[...] = x_ref[...] + mod_ref[0, 5] * mix


def _outproj_odd_body(x_ref, mod_ref, ocmp_ref, oslc_ref, owin_ref, zg_ref, osb_ref, e_ref, w_ref, o_ref):
    hi, mid, lo = _split3(jax.nn.sigmoid(zg_ref[...]))
    e = e_ref[...]
    gexp = _dot(hi, e) + _dot(mid, e) + _dot(lo, e)
    nsa = (gexp[:, 0:512] * ocmp_ref[...] + gexp[:, 512:1024] * oslc_ref[...]
           + gexp[:, 1024:1536] * owin_ref[...])
    mix = (_dot(nsa.astype(BF16), w_ref[0:512, :])
           + _dot(osb_ref[...].astype(BF16), w_ref[512:1024, :]))
    o_ref[...] = x_ref[...] + mod_ref[0, 5] * mix


def _outproj(x, mod, parts, consts, w, seq_len, odd):
    n = x.shape[0]
    tm = _tile_rows(n)
    tok = lambda c: pl.BlockSpec((tm, c), lambda i: (i, 0))
    in_specs = ([tok(D_MODEL), _mod_spec(mod, tm, seq_len)] + [tok(p.shape[1]) for p in parts]
                + [_const_spec(c.shape) for c in consts] + [_const_spec((1024, D_MODEL))])
    return pl.pallas_call(
        _outproj_odd_body if odd else _outproj_even_body,
        grid=(n // tm,),
        in_specs=in_specs,
        out_specs=tok(D_MODEL),
        out_shape=jax.ShapeDtypeStruct((n, D_MODEL), F32),
        compiler_params=_cparams("arbitrary"),
        name="outproj_odd" if odd else "outproj_even",
    )(x, mod, *parts, *consts, w)


def _softmax_init(m_ref, l_ref, acc_ref):
    m_ref[...] = jnp.full(m_ref.shape, NEG, F32)
    l_ref[...] = jnp.zeros(l_ref.shape, F32)
    acc_ref[...] = jnp.zeros(acc_ref.shape, F32)


def _softmax_step(s, mask, v, m_ref, l_ref, acc_ref, v_is_transposed=False):
    if mask is not None:
        s = jnp.where(mask, s, NEG)
    m_prev = m_ref[...]
    m_new = jnp.maximum(m_prev, jnp.max(s, axis=1, keepdims=True))
    p = jnp.exp(s - m_new)
    if mask is not None:
        p = jnp.where(mask, p, 0.0)
    alpha = jnp.exp(m_prev - m_new)
    l_ref[...] = alpha * l_ref[...] + jnp.sum(p, axis=1, keepdims=True)
    pv = _dot_nt(p.astype(BF16), v) if v_is_transposed else _dot(p.astype(BF16), v)
    acc_ref[...] = alpha * acc_ref[...] + pv
    m_ref[...] = m_new


def _softmax_out(l_ref, acc_ref):
    l = l_ref[...]
    return jnp.where(l > 0.0, acc_ref[...] / jnp.where(l > 0.0, l, 1.0), 0.0)


def _tsoftmax_step(s_t, mask_t, v_t, m_ref, l_ref, acc_ref):
    if mask_t is not None:
        s_t = jnp.where(mask_t, s_t, NEG)
    m_prev = m_ref[...]
    m_new = jnp.maximum(m_prev, jnp.max(s_t, axis=0, keepdims=True))
    p = jnp.exp(s_t - m_new)
    alpha = jnp.exp(m_prev - m_new)
    l_ref[...] = alpha * l_ref[...] + jnp.sum(p, axis=0, keepdims=True)
    acc_ref[...] = alpha * acc_ref[...] + _dot(v_t, p.astype(BF16))
    m_ref[...] = m_new


def _tsoftmax_out(l_ref, acc_ref):
    return acc_ref[...] * (1.0 / l_ref[...])


def _mla_stack_q_t(qlat, qrope):
    tq = qlat.shape[0]
    ql_t = qlat.T
    qr_t = qrope.T
    sub = _iota((128, tq), 0)
    parts = []
    for h in range(H_B):
        qr = qr_t[128 * (h // 4):128 * (h // 4 + 1)]
        qr = jnp.where((sub // ROPE_D) == (h % 4), qr, 0.0)
        parts.append(jnp.concatenate([ql_t[128 * h:128 * (h + 1)], qr], axis=0))
    return jnp.concatenate(parts, axis=1).astype(BF16)


def _mla_prompt_body(qlat_ref, qrope_ref, kfull_ref, wvbt_ref, o_ref, k_scr, vt_scr, m_ref, l_ref, acc_ref, *, tq):
    qi = pl.program_id(1)

    @pl.when(qi == 0)
    def _():
        kf = kfull_ref[0]
        k_scr[...] = kf.astype(BF16)
        for kb in range(kf.shape[0] // tq):
            vt_scr[kb] = kf[kb * tq:(kb + 1) * tq, 0:KV_LORA].T.astype(BF16)

    q_t = _mla_stack_q_t(qlat_ref[...], qrope_ref[...])
    _softmax_init(m_ref, l_ref, acc_ref)
    shape = (tq, H_B * tq)

    def step(kb, diag):
        k = k_scr[pl.ds(pl.multiple_of(kb * tq, tq), tq), :]
        mask = (_iota(shape, 0) <= (_iota(shape, 1) & (tq - 1))) if diag else None
        _tsoftmax_step(_dot(k, q_t), mask, vt_scr[kb], m_ref, l_ref, acc_ref)

    def far(kb, c):
        step(kb, False)
        return c

    lax.fori_loop(0, qi, far, 0)
    step(qi, True)
    o_t = _tsoftmax_out(l_ref, acc_ref)
    olat_t = jnp.concatenate([o_t[:, h * tq:(h + 1) * tq] for h in range(H_B)], axis=0)
    o_ref[...] = _dot(wvbt_ref[...], olat_t.astype(BF16)).T


def _mla_prompt(qlat, qrope, kfull, wvbt, batch, seq):
    tq = min(TQ_MLA, seq)
    assert tq & (tq - 1) == 0 and seq % tq == 0
    nq = seq // tq
    tok = lambda c: pl.BlockSpec((tq, c), lambda b, i: (b * nq + i, 0))
    return pl.pallas_call(
        functools.partial(_mla_prompt_body, tq=tq),
        grid=(batch, nq),
        in_specs=[tok(1024), tok(256), pl.BlockSpec((1, seq, 256), lambda b, i: (b, 0, 0)),
                  pl.BlockSpec((512, 1024), lambda b, i: (0, 0))],
        out_specs=tok(512),
        out_shape=jax.ShapeDtypeStruct((batch * seq, 512), F32),
        scratch_shapes=[pltpu.VMEM((seq, 256), BF16), pltpu.VMEM((nq, KV_LORA, tq), BF16),
                        pltpu.VMEM((1, H_B * tq), F32), pltpu.VMEM((1, H_B * tq), F32),
                        pltpu.VMEM((KV_LORA, H_B * tq), F32)],
        compiler_params=_cparams("arbitrary", "arbitrary"),
        name="mla_prompt",
    )(qlat, qrope, kfull.reshape(batch, seq, 256), wvbt)


def _hgrn_chunk(zq, zf, zi, zg, par, st, chunk, sub, valid):
    log_lb, log1m_lb, lb_pos, one_m_lb, gn = par[0:1], par[1:2], par[2:3], par[3:4], par[4:5]
    q = _silu(zq)
    ls = _log_sigmoid(zf)
    b = log1m_lb + ls
    lae = jnp.maximum(log_lb, b) + _softplus_neg_abs(log_lb - b)
    logf = jnp.where(lb_pos > 0.5, lae, ls)
    kin = one_m_lb * jax.nn.sigmoid(-zf)
    v = zi
    row = _iota((chunk, 128), 0)
    g = logf
    sh = 1
    while sh < chunk:
        g = g + jnp.where(row >= sh, pltpu.roll(g, sh, axis=0), 0.0)
        sh *= 2
    o = _dot_nt((q * jnp.exp(g)).astype(BF16), st.astype(BF16))
    nsub = chunk // sub
    v_bf = v.astype(BF16)
    if nsub > 1:
        ends = [g[sub * j + sub - 1:sub * j + sub] for j in range(nsub)]
        esub = jnp.concatenate([jnp.broadcast_to(e, (sub, 128)) for e in ends], axis=0)
        kt = (kin * jnp.exp(esub - g)).astype(BF16)
        col_a = _iota((chunk, chunk), 1)
        row_a = _iota((chunk, chunk), 0)
        a = jnp.zeros((chunk, chunk), F32)
        for j in range(nsub - 1):
            qj = (q * jnp.exp(jnp.minimum(g - ends[j], 0.0))).astype(BF16)
            aj = _dot_nt(qj, kt)
            a = jnp.where((col_a >= sub * j) & (col_a < sub * (j + 1)) & (row_a >= sub * (j + 1)), aj, a)
        o = o + _dot(a.astype(BF16), v_bf)
    row_s = _iota((sub, 128), 0)
    diag = []
    for i in range(nsub):
        g_i = g[sub * i:sub * (i + 1)]
        q_i = q[sub * i:sub * (i + 1)]
        o_i = jnp.zeros((sub, 128), F32)
        for s in range(min(sub, valid - sub * i)):
            r = sub * i + s
            e = jnp.exp(jnp.minimum(g_i - g[r:r + 1], 0.0))
            x = jnp.where(row_s >= s, q_i * (kin[r:r + 1] * e), 0.0)
            o_i = o_i + jnp.sum(x, axis=1, keepdims=True) * v[r:r + 1]
        diag.append(o_i)
    o = o + (diag[0] if nsub == 1 else jnp.concatenate(diag, axis=0))
    g_last = g[valid - 1:valid]
    khat = kin * jnp.exp(jnp.minimum(g_last - g, 0.0))
    if valid < chunk:
        khat = jnp.where(row < valid, khat, 0.0)
    st_new = st * jnp.exp(g_last) + _dot_tn(v_bf, khat.astype(BF16))
    return _rms(o) * gn * _silu(zg), st_new


def _hgrn_body(zq_ref, zf_ref, zi_ref, zg_ref, par_ref, *rest, chunk, sub, valid, n_chunks, has_s0):
    if has_s0:
        s0_ref, o_ref, sout_ref, st_ref = rest
    else:
        o_ref, sout_ref, st_ref = rest
    tb = pl.program_id(1)

    @pl.when(tb == 0)
    def _():
        for h in range(H_A):
            st_ref[h] = s0_ref[0, h].T if has_s0 else jnp.zeros((128, 128), F32)

    def chunk_body(c, carry):
        r0 = pl.multiple_of(c * chunk, chunk)
        for h in range(H_A):
            cs = slice(128 * h, 128 * (h + 1))
            rs = pl.ds(r0, chunk)
            o, st_new = _hgrn_chunk(zq_ref[rs, cs], zf_ref[rs, cs], zi_ref[rs, cs], zg_ref[rs, cs],
                                    par_ref[:, cs], st_ref[h], chunk, sub, valid)
            o_ref[rs, cs] = o
            st_ref[h] = st_new
        return carry

    lax.fori_loop(0, n_chunks, chunk_body, 0)

    @pl.when(tb == pl.num_programs(1) - 1)
    def _():
        for h in range(H_A):
            sout_ref[0, h] = st_ref[h].T


def _hgrn(zh, par, batch, rows_per_seq, tb, chunk, sub, valid, s0=None):
    nt = rows_per_seq // tb
    col = lambda j: pl.BlockSpec((tb, 512), lambda b, t: (b * nt + t, j))
    in_specs = [col(0), col(1), col(2), col(3), pl.BlockSpec((8, 512), lambda b, t: (0, 0))]
    args = [zh, zh, zh, zh, par]
    st_spec = pl.BlockSpec((1, H_A, 128, 128), lambda b, t: (b, 0, 0, 0))
    if s0 is not None:
        in_specs.append(st_spec)
        args.append(s0)
    return pl.pallas_call(
        functools.partial(_hgrn_body, chunk=chunk, sub=sub, valid=valid, n_chunks=tb // chunk,
                          has_s0=s0 is not None),
        grid=(batch, nt),
        in_specs=in_specs,
        out_specs=[pl.BlockSpec((tb, 512), lambda b, t: (b * nt + t, 0)), st_spec],
        out_shape=[jax.ShapeDtypeStruct((batch * rows_per_seq, 512), F32),
                   jax.ShapeDtypeStruct((batch, H_A, 128, 128), F32)],
        scratch_shapes=[pltpu.VMEM((H_A, 128, 128), F32)],
        compiler_params=_cparams("arbitrary", "arbitrary"),
        name="hgrn",
    )(*args)


def _block_means(x):
    nb = x.shape[0] // NSA_BLOCK
    return jnp.sum(x.reshape(nb, NSA_BLOCK, x.shape[1]), axis=1) * (1.0 / NSA_BLOCK)


def _means_prompt_body(x_ref, o_ref):
    o_ref[...] = _block_means(x_ref[...])


def _means_prompt(nsa_rows):
    n = nsa_rows.shape[0]
    tm = _tile_rows(n)
    return pl.pallas_call(
        _means_prompt_body,
        grid=(n // tm,),
        in_specs=[pl.BlockSpec((tm, 256), lambda i: (i, 0))],
        out_specs=pl.BlockSpec((tm // NSA_BLOCK, 256), lambda i: (i, 0)),
        out_shape=jax.ShapeDtypeStruct((n // NSA_BLOCK, 256), F32),
        compiler_params=_cparams("arbitrary"),
        name="nsa_means_prompt",
    )(nsa_rows)


def _cmp_head(qh, ck, cv, bias, vis):
    s = _dot_nt(qh.astype(BF16), ck) * ATT_SCALE + bias
    if vis is not None:
        s = jnp.where(vis, s, NEG)
    e = jnp.exp(s - jnp.max(s, axis=1, keepdims=True))
    p = e / jnp.sum(e, axis=1, keepdims=True)
    if vis is not None:
        p = jnp.where(vis, p, 0.0)
    return _dot(p.astype(BF16), cv), p


def _top_blocks(score, blk, n_sel):
    nb = score.shape[1]
    for _ in range(n_sel):
        m = jnp.max(score, axis=1, keepdims=True)
        idx = jnp.min(jnp.where(score == m, blk, nb), axis=1, keepdims=True)
        yield idx, m > 0.5 * NEG
        score = jnp.where(blk == idx, PICKED, score)


def _cmpsel_prompt_body(q_ref, cmp_ref, near_ref, cfar_ref, ocmp_ref, sel_ref, *, tq, nbp, n_sel):
    qi = pl.program_id(1)
    q_t = q_ref[...].T
    shape = (nbp, tq)
    qpos = qi * tq + _iota(shape, 1)
    blk = _iota(shape, 0)
    vis = (blk * NSA_BLOCK + NSA_BLOCK - 1) <= qpos
    cur = jnp.right_shift(qpos, 6)
    allowed = blk <= cur
    forced = (blk == 0) | (blk == cur) | (blk == cur - 1)
    near1 = blk == cur - 1
    near2 = blk == cur - 2
    outs = []
    for g in range(G_C):
        ck = cmp_ref[0, :, DH * g:DH * (g + 1)].astype(BF16)
        cv_t = cmp_ref[0, :, 128 + DH * g:128 + DH * (g + 1)].T.astype(BF16)
        imp = jnp.zeros(shape, F32)
        for j in range(HPG_C):
            h = HPG_C * g + j
            bias = jnp.where(near1, near_ref[0, h:h + 1, :],
                             jnp.where(near2, near_ref[1, h:h + 1, :], cfar_ref[h:h + 1, 0:1]))
            bias = jnp.where(blk == cur, cfar_ref[h:h + 1, 1:2], bias)
            s = _dot(ck, q_t[DH * h:DH * (h + 1)].astype(BF16)) * ATT_SCALE + bias
            s = jnp.where(vis, s, NEG)
            e = jnp.exp(s - jnp.max(s, axis=0, keepdims=True))
            p = jnp.where(vis, e / jnp.sum(e, axis=0, keepdims=True), 0.0)
            outs.append(_dot(cv_t, p.astype(BF16)))
            imp = imp + p
        score = jnp.where(allowed, imp + jnp.where(forced, FORCE_BONUS, 0.0), NEG)
        sel = jnp.zeros(shape, F32)
        for _ in range(n_sel):
            m = jnp.max(score, axis=0, keepdims=True)
            idx = jnp.min(jnp.where(score == m, blk, nbp), axis=0, keepdims=True)
            hit = blk == idx
            sel = jnp.where(hit & (m > 0.5 * NEG), 1.0, sel)
            score = jnp.where(hit, PICKED, score)
        sel_ref[0, g] = sel.astype(BF16)
    ocmp_ref[...] = jnp.concatenate(outs, axis=0).T


def _cmpsel_prompt(q, cmp, near, cfar, batch, seq):
    tq = min(TQ, seq)
    nq = seq // tq
    nbp = cmp.shape[1]
    n_sel = min(TOP_N, -(-seq // NSA_BLOCK))
    return pl.pallas_call(
        functools.partial(_cmpsel_prompt_body, tq=tq, nbp=nbp, n_sel=n_sel),
        grid=(batch, nq),
        in_specs=[pl.BlockSpec((tq, 512), lambda b, i: (b * nq + i, 0)),
                  pl.BlockSpec((1, nbp, 256), lambda b, i: (b, 0, 0)),
                  pl.BlockSpec((2, H_C, tq), lambda b, i: (0, 0, 0)),
                  pl.BlockSpec((H_C, 128), lambda b, i: (0, 0))],
        out_specs=[pl.BlockSpec((tq, 512), lambda b, i: (b * nq + i, 0)),
                   pl.BlockSpec((1, G_C, nbp, tq), lambda b, i: (b, 0, 0, i))],
        out_shape=[jax.ShapeDtypeStruct((batch * seq, 512), F32),
                   jax.ShapeDtypeStruct((batch, G_C, nbp, seq), BF16)],
        compiler_params=_cparams("arbitrary", "arbitrary"),
        name="nsa_cmpsel_prompt",
    )(q, cmp, near, cfar)


def _heads_t(q_t, h0, nh):
    return jnp.concatenate([q_t[DH * h:DH * (h + 1)] for h in range(h0, h0 + nh)], axis=1).astype(BF16)


def _fill_kv(src_ref, k_col, v_col, n_groups, k_scr, vt_scr, tq):
    for g in range(n_groups):
        k_scr[g] = src_ref[0, :, k_col + DH * g:k_col + DH * (g + 1)].astype(BF16)
    for kb in range(vt_scr.shape[0]):
        vt_scr[kb] = src_ref[0, kb * tq:(kb + 1) * tq, v_col:v_col + DH * n_groups].T.astype(BF16)


def _nsa_prompt_body(q_ref, nsa_ref, win_ref, sel_ref, bdt_ref, bst_ref, cfar_ref, oslc_ref, owin_ref,
                     ks_scr, vs_scr, kw_scr, vw_scr, m_ref, l_ref, acc_ref, *, tq, nbp):
    qi = pl.program_id(1)

    @pl.when(qi == 0)
    def _():
        _fill_kv(nsa_ref, 256, 384, G_C, ks_scr, vs_scr, tq)
        _fill_kv(win_ref, 0, 128, G_C, kw_scr, vw_scr, tq)

    q_t = (q_ref[...] * ATT_SCALE).T
    per_tile = tq // NSA_BLOCK
    n_back = WINDOW // tq
    shape = (tq, HPG_C * tq)
    key_i, qry_i = _iota(shape, 0), _iota(shape, 1) & (tq - 1)
    groups = range(G_C)
    heads = [range(HPG_C * g, HPG_C * (g + 1)) for g in groups]
    q_gs = [_heads_t(q_t, HPG_C * g, HPG_C) for g in groups]
    far_rows = [jnp.concatenate([jnp.broadcast_to(cfar_ref[h:h + 1, 0:1], (1, tq)) for h in heads[g]], axis=1)
                for g in groups]

    def tile(g, kb, mode, mask_t, k_scr, vt_scr):
        k = k_scr[g, pl.ds(pl.multiple_of(kb * tq, tq), tq), :]
        if mode == "far":
            bias = far_rows[g]
        else:
            tab = bdt_ref if mode == "diag" else bst_ref
            bias = jnp.concatenate([tab[h] for h in heads[g]], axis=1)
        _tsoftmax_step(_dot(k, q_gs[g]) + bias, mask_t, vt_scr[kb, DH * g:DH * (g + 1), :],
                       m_ref.at[g], l_ref.at[g], acc_ref.at[g])

    def init():
        for g in groups:
            _softmax_init(m_ref.at[g], l_ref.at[g], acc_ref.at[g])

    def outputs():
        return [_tsoftmax_out(l_ref.at[g], acc_ref.at[g])[:, j * tq:(j + 1) * tq]
                for g in groups for j in range(HPG_C)]

    def slc_tile(kb, mode):
        expand = _iota((tq, nbp), 1) == kb * per_tile + jnp.right_shift(_iota((tq, nbp), 0), 6)
        expand = jnp.where(expand, 1.0, 0.0).astype(BF16)
        for g in groups:
            m = _dot(expand, sel_ref[0, g]) > 0.5
            if mode == "diag":
                m = m & (_iota((tq, tq), 0) <= _iota((tq, tq), 1))
            tile(g, kb, mode, jnp.concatenate([m] * HPG_C, axis=1), ks_scr, vs_scr)

    init()

    def far(kb, c):
        slc_tile(kb, "far")
        return c

    lax.fori_loop(0, jnp.maximum(qi - 1, 0), far, 0)

    @pl.when(qi >= 1)
    def _():
        slc_tile(qi - 1, "sub")

    slc_tile(qi, "diag")
    oslc_ref[...] = jnp.concatenate(outputs(), axis=0).T

    init()
    for d in range(n_back, 0, -1):
        mask_t = (key_i >= qry_i) if d == n_back else None

        @pl.when(qi >= d)
        def _(d=d, mask_t=mask_t):
            for g in groups:
                tile(g, qi - d, "sub" if d == 1 else "far", mask_t, kw_scr, vw_scr)

    for g in groups:
        tile(g, qi, "diag", key_i <= qry_i, kw_scr, vw_scr)
    owin_ref[...] = jnp.concatenate(outputs(), axis=0).T


def _nsa_prompt_attn(q, nsa, win, sel, bdt, bst, cfar, batch, seq):
    tq = bdt.shape[1]
    assert tq & (tq - 1) == 0 and seq % tq == 0 and WINDOW % tq == 0 and tq >= MAX_DISTANCE
    nq = seq // tq
    nbp = sel.shape[2]
    tok = pl.BlockSpec((tq, 512), lambda b, i: (b * nq + i, 0))
    seq_spec = lambda w: pl.BlockSpec((1, seq, w), lambda b, i: (b, 0, 0))
    full = lambda a: pl.BlockSpec(a.shape, lambda b, i: (0,) * a.ndim, pipeline_mode=pl.Buffered(1))
    r = HPG_C * tq
    kv_scratch = [pltpu.VMEM((G_C, seq, DH), BF16), pltpu.VMEM((nq, G_C * DH, tq), BF16)]
    return pl.pallas_call(
        functools.partial(_nsa_prompt_body, tq=tq, nbp=nbp),
        grid=(batch, nq),
        in_specs=[tok, seq_spec(NSA_ROW), seq_spec(WIN_ROW),
                  pl.BlockSpec((1, G_C, nbp, tq), lambda b, i: (b, 0, 0, i)), full(bdt), full(bst), full(cfar)],
        out_specs=[tok, tok],
        out_shape=[jax.ShapeDtypeStruct((batch * seq, 512), F32)] * 2,
        scratch_shapes=kv_scratch + kv_scratch + [pltpu.VMEM((G_C, 1, r), F32), pltpu.VMEM((G_C, 1, r), F32),
                                                  pltpu.VMEM((G_C, DH, r), F32)],
        compiler_params=_cparams("arbitrary", "arbitrary"),
        name="nsa_slc_win_prompt",
    )(q, nsa.reshape(batch, seq, NSA_ROW), win.reshape(batch, seq, WIN_ROW), sel, bdt, bst, cfar)


def _sb_prompt_body(q_ref, rows_ref, u_ref, o_ref, k_scr, vt_scr, carry_ref, acc_ref, *, tq):
    qi = pl.program_id(1)

    @pl.when(qi == 0)
    def _():
        _fill_kv(rows_ref, 0, 256, KV_D, k_scr, vt_scr, tq)

    q_t = (q_ref[...] * (ATT_SCALE * LOG2E)).T
    shape = (tq, HPK_D * tq)
    before = _iota(shape, 0) < (_iota(shape, 1) & (tq - 1))
    q_gs = [_heads_t(q_t, HPK_D * g, HPK_D) for g in range(KV_D)]
    carry_ref[...] = jnp.zeros(carry_ref.shape, F32)
    acc_ref[...] = jnp.zeros(acc_ref.shape, F32)

    def step(kb, diag):
        for g in range(KV_D):
            k = k_scr[g, pl.ds(pl.multiple_of(kb * tq, tq), tq), :]
            z = _dot(k, q_gs[g])
            lsz = _log2_sigmoid(z)
            lf = lsz - z
            if diag:
                lf = jnp.where(before, lf, 0.0)
            between = _dot(u_ref[...], lf.astype(BF16))
            a = jnp.exp2(lsz + between + carry_ref[g])
            if diag:
                a = jnp.where(before, a, 0.0)
            acc_ref[g] += _dot(vt_scr[kb, DH * g:DH * (g + 1), :], a.astype(BF16))
            carry_ref[g] += jnp.sum(lf, axis=0, keepdims=True)

    step(qi, True)

    def back(it, c):
        step(qi - 1 - it, False)
        return c

    lax.fori_loop(0, qi, back, 0)
    outs = [acc_ref[g][:, j * tq:(j + 1) * tq] for g in range(KV_D) for j in range(HPK_D)]
    o_ref[...] = jnp.concatenate(outs, axis=0).T


def _sb_prompt(q, rows, batch, seq):
    tq = min(TQ_MLA, seq)
    assert tq & (tq - 1) == 0 and seq % tq == 0
    nq = seq // tq
    later = np.arange(tq)[None, :] > np.arange(tq)[:, None]
    u = jnp.asarray(later, BF16)
    tok = pl.BlockSpec((tq, 512), lambda b, i: (b * nq + i, 0))
    r = HPK_D * tq
    return pl.pallas_call(
        functools.partial(_sb_prompt_body, tq=tq),
        grid=(batch, nq),
        in_specs=[tok, pl.BlockSpec((1, seq, SB_ROW), lambda b, i: (b, 0, 0)),
                  pl.BlockSpec((tq, tq), lambda b, i: (0, 0))],
        out_specs=tok,
        out_shape=jax.ShapeDtypeStruct((batch * seq, 512), F32),
        scratch_shapes=[pltpu.VMEM((KV_D, seq, DH), BF16), pltpu.VMEM((nq, KV_D * DH, tq), BF16),
                        pltpu.VMEM((KV_D, 1, r), F32), pltpu.VMEM((KV_D, DH, r), F32)],
        compiler_params=_cparams("arbitrary", "arbitrary"),
        name="sb_prompt",
    )(q, rows.reshape(batch, seq, SB_ROW), u)


def _page_specs(block, pp, col_block, page_of):
    return [pl.BlockSpec(block, lambda b, s, pt, j=j: (page_of(b, s, j, pt), 0, col_block)) for j in range(pp)]


def _mla_decode_body(pt_ref, q_ref, knew_ref, wvb_ref, *rest, pp, valid):
    pages = rest[:pp]
    o_ref, m_ref, l_ref, acc_ref = rest[pp:]
    st = pl.program_id(1)
    r = SAMPLE_ROWS
    q = q_ref[0].astype(BF16)

    @pl.when(st == 0)
    def _():
        _softmax_init(m_ref, l_ref, acc_ref)

    k_t = jnp.concatenate([pages[j][0] for j in range(pp)], axis=1).astype(BF16)
    _softmax_step(_dot(q, k_t), None, k_t[0:KV_LORA], m_ref, l_ref, acc_ref, v_is_transposed=True)

    @pl.when(st == pl.num_programs(1) - 1)
    def _():
        kn = knew_ref[0].astype(BF16)
        shape = (H_B * r, r)
        col = _iota(shape, 1)
        mask = (col <= (_iota(shape, 0) & (r - 1))) & (col < valid)
        _softmax_step(_dot_nt(q, kn), mask, kn[:, 0:KV_LORA], m_ref, l_ref, acc_ref)
        o = _softmax_out(l_ref, acc_ref)
        olat = jnp.concatenate([o[r * h:r * (h + 1)] for h in range(H_B)], axis=1)
        o_ref[0] = _dot(olat.astype(BF16), wvb_ref[...])


def _mla_decode(page_table, q, knew, wvb, cache_t, valid):
    batch, n_pages = page_table.shape
    pp = min(2 * PAGES_PER_STEP, n_pages)
    assert n_pages % pp == 0
    r = SAMPLE_ROWS
    grid_spec = pltpu.PrefetchScalarGridSpec(
        num_scalar_prefetch=1,
        grid=(batch, n_pages // pp),
        in_specs=[pl.BlockSpec((1, H_B * r, MLA_ROW), lambda b, s, pt: (b, 0, 0)),
                  pl.BlockSpec((1, r, MLA_ROW), lambda b, s, pt: (b, 0, 0)),
                  pl.BlockSpec((1024, 512), lambda b, s, pt: (0, 0))]
        + _page_specs((1, MLA_ROW, PAGE), pp, 0, lambda b, s, j, pt: pt[b, s * pp + j]),
        out_specs=pl.BlockSpec((1, r, 512), lambda b, s, pt: (b, 0, 0)),
        scratch_shapes=[pltpu.VMEM((H_B * r, 1), F32), pltpu.VMEM((H_B * r, 1), F32),
                        pltpu.VMEM((H_B * r, KV_LORA), F32)])
    return pl.pallas_call(
        functools.partial(_mla_decode_body, pp=pp, valid=valid),
        grid_spec=grid_spec,
        out_shape=jax.ShapeDtypeStruct((batch, r, 512), F32),
        compiler_params=_cparams("arbitrary", "arbitrary"),
        name="mla_decode",
    )(page_table, q, knew, wvb, *([cache_t] * pp))


def _means_decode_body(pt_ref, *rest, pp):
    pages = rest[:pp]
    o_ref = rest[pp]
    o_ref[0] = _block_means(jnp.concatenate([pages[j][0] for j in range(pp)], axis=0))


def _means_decode(page_table, cache):
    batch, n_pages = page_table.shape
    pp = min(2 * PAGES_PER_STEP, n_pages)
    assert n_pages % pp == 0
    per_page = PAGE // NSA_BLOCK
    grid_spec = pltpu.PrefetchScalarGridSpec(
        num_scalar_prefetch=1,
        grid=(batch, n_pages // pp),
        in_specs=_page_specs((1, PAGE, 256), pp, 0, lambda b, s, j, pt: pt[b, s * pp + j]),
        out_specs=pl.BlockSpec((1, pp * per_page, 256), lambda b, s, pt: (b, s, 0)))
    return pl.pallas_call(
        functools.partial(_means_decode_body, pp=pp),
        grid_spec=grid_spec,
        out_shape=jax.ShapeDtypeStruct((batch, n_pages * per_page, 256), F32),
        compiler_params=_cparams("arbitrary", "arbitrary"),
        name="nsa_means_decode",
    )(page_table, *([cache] * pp))


def _cmpsel_decode_body(q_ref, cmp_ref, cb_ref, ocmp_ref, idx_ref, *, nb, n_past_sel):
    r = SAMPLE_ROWS
    blk = _iota((1, nb), 1)
    forced = (blk == 0) | (blk == nb - 1)
    lane = _iota((r, 128), 1)
    for g in range(G_C):
        ck = cmp_ref[0, :, DH * g:DH * (g + 1)].astype(BF16)
        cv = cmp_ref[0, :, 128 + DH * g:128 + DH * (g + 1)].astype(BF16)
        imp = jnp.zeros((r, nb), F32)
        for j in range(HPG_C):
            h = HPG_C * g + j
            o, p = _cmp_head(q_ref[0, :, DH * h:DH * (h + 1)], ck, cv, cb_ref[h], None)
            ocmp_ref[0, :, DH * h:DH * (h + 1)] = o
            imp = imp + p
        score = imp + jnp.where(forced, FORCE_BONUS, 0.0)
        picked = jnp.zeros((r, 128), jnp.int32)
        for slot, (idx, _) in enumerate(_top_blocks(score, blk, n_past_sel)):
            picked = jnp.where(lane == slot, idx, picked)
        idx_ref[0, g] = picked


def _cmpsel_decode(q, cmp, cb, n_past_sel):
    batch, nb, _ = cmp.shape
    r = SAMPLE_ROWS
    return pl.pallas_call(
        functools.partial(_cmpsel_decode_body, nb=nb, n_past_sel=n_past_sel),
        grid=(batch,),
        in_specs=[pl.BlockSpec((1, r, 512), lambda b: (b, 0, 0)),
                  pl.BlockSpec((1, nb, 256), lambda b: (b, 0, 0)),
                  pl.BlockSpec((H_C, r, nb), lambda b: (0, 0, 0))],
        out_specs=[pl.BlockSpec((1, r, 512), lambda b: (b, 0, 0)),
                   pl.BlockSpec((1, G_C, r, 128), lambda b: (b, 0, 0, 0))],
        out_shape=[jax.ShapeDtypeStruct((batch, r, 512), F32),
                   jax.ShapeDtypeStruct((batch, G_C, r, 128), jnp.int32)],
        compiler_params=_cparams("arbitrary"),
        name="nsa_cmpsel_decode",
    )(q, cmp, cb)


def _dist_bias(tab3_ref, dist):
    nd = tab3_ref.shape[1]
    width = dist.shape[1]
    onehot = jnp.where(_iota((nd, width), 0) == jnp.clip(dist, 0, nd - 1), 1.0, 0.0).astype(BF16)
    b = _dot(tab3_ref[...], onehot)
    return b[0:8] + b[8:16] + b[16:24]


def _slc_decode_body(idx_ref, pt_ref, q_ref, new_ref, tab3_ref, *rest, n_slots, p_len, n_tok):
    blocks = rest[:n_tok * G_C * n_slots]
    o_ref = rest[n_tok * G_C * n_slots]
    b = pl.program_id(0)
    r = SAMPLE_ROWS
    head_row = _iota((H_C, 1), 0)
    col = _iota((H_C, r), 1)
    n_keys = n_slots * NSA_BLOCK
    lane = _iota((1, n_keys), 1)
    slot_of = jnp.right_shift(lane, 6)
    for t in range(n_tok):
        q = (q_ref[0, t] * ATT_SCALE).astype(BF16)
        out = jnp.zeros((H_C, DH), F32)
        for g in range(G_C):
            kvs = [blocks[(t * G_C + g) * n_slots + s][0] for s in range(n_slots)]
            k = jnp.concatenate([kv[:, DH * g:DH * (g + 1)] for kv in kvs], axis=0).astype(BF16)
            v = jnp.concatenate([kv[:, 128 + DH * g:128 + DH * (g + 1)] for kv in kvs], axis=0).astype(BF16)
            first = jnp.zeros((1, n_keys), jnp.int32)
            for s in range(n_slots):
                n = idx_ref[((b * G_C + g) * r + t) * n_slots + s]
                first = jnp.where(slot_of == s, p_len + t - n * NSA_BLOCK, first)
            sc = _dot_nt(q, k) + _dist_bias(tab3_ref, first - (lane & (NSA_BLOCK - 1)))
            kn = new_ref[0, :, 256 + DH * g:256 + DH * (g + 1)].astype(BF16)
            vn = new_ref[0, :, 384 + DH * g:384 + DH * (g + 1)].astype(BF16)
            cur_ok = col <= t
            sc_new = jnp.where(cur_ok, _dot_nt(q, kn) + _dist_bias(tab3_ref, t - _iota((1, r), 1)), NEG)
            m = jnp.maximum(jnp.max(sc, axis=1, keepdims=True), jnp.max(sc_new, axis=1, keepdims=True))
            p = jnp.exp(sc - m)
            p_new = jnp.where(cur_ok, jnp.exp(sc_new - m), 0.0)
            l = jnp.sum(p, axis=1, keepdims=True) + jnp.sum(p_new, axis=1, keepdims=True)
            acc = _dot(p.astype(BF16), v) + _dot(p_new.astype(BF16), vn)
            in_group = (head_row >= HPG_C * g) & (head_row < HPG_C * (g + 1))
            out = jnp.where(in_group, acc / l, out)
        o_ref[0, t] = out


def _slc_decode(idx, page_table, q, new_rows, tab3, cache, n_tok, p_len):
    batch, n_pages = page_table.shape
    r = SAMPLE_ROWS
    n_slots = idx.shape[0] // (batch * G_C * r)
    per_page = PAGE // NSA_BLOCK

    def slot_spec(t, g, s):
        def index(b, idx_ref, pt):
            n = idx_ref[((b * G_C + g) * r + t) * n_slots + s]
            return (pt[b, n // per_page], n % per_page, 1)
        return pl.BlockSpec((1, NSA_BLOCK, 256), index)

    slots = [slot_spec(t, g, s) for t in range(n_tok) for g in range(G_C) for s in range(n_slots)]
    grid_spec = pltpu.PrefetchScalarGridSpec(
        num_scalar_prefetch=2,
        grid=(batch,),
        in_specs=[pl.BlockSpec((1, n_tok, H_C, DH), lambda b, i, pt: (b, 0, 0, 0)),
                  pl.BlockSpec((1, r, NSA_ROW), lambda b, i, pt: (b, 0, 0)),
                  pl.BlockSpec(tab3.shape, lambda b, i, pt: (0, 0))] + slots,
        out_specs=pl.BlockSpec((1, n_tok, H_C, DH), lambda b, i, pt: (b, 0, 0, 0)))
    return pl.pallas_call(
        functools.partial(_slc_decode_body, n_slots=n_slots, p_len=p_len, n_tok=n_tok),
        grid_spec=grid_spec,
        out_shape=jax.ShapeDtypeStruct((batch, n_tok, H_C, DH), F32),
        compiler_params=_cparams("arbitrary"),
        name="nsa_slc_decode",
    )(idx, page_table, q, new_rows, tab3, *([cache] * len(slots)))


def _win_decode_body(q_ref, state_ref, new_ref, wbs_ref, wbn_ref, o_ref, *, valid):
    r = SAMPLE_ROWS
    wb = state_ref.shape[1]
    for g in range(G_C):
        q4 = (jnp.concatenate([q_ref[0, :, DH * h:DH * (h + 1)] for h in range(HPG_C * g, HPG_C * (g + 1))],
                              axis=0) * ATT_SCALE).astype(BF16)
        ks = state_ref[0, :, DH * g:DH * (g + 1)].astype(BF16)
        vs = state_ref[0, :, 128 + DH * g:128 + DH * (g + 1)].astype(BF16)
        kn = new_ref[0, :, DH * g:DH * (g + 1)].astype(BF16)
        vn = new_ref[0, :, 128 + DH * g:128 + DH * (g + 1)].astype(BF16)
        tok = _iota((HPG_C * r, 1), 0) & (r - 1)
        ok_s = _iota((HPG_C * r, wb), 1) >= tok + (wb - WINDOW)
        col_n = _iota((HPG_C * r, r), 1)
        ok_n = (col_n <= tok) & (col_n < valid)
        bias_s = jnp.concatenate([wbs_ref[h] for h in range(HPG_C * g, HPG_C * (g + 1))], axis=0)
        bias_n = jnp.concatenate([wbn_ref[h] for h in range(HPG_C * g, HPG_C * (g + 1))], axis=0)
        s_s = jnp.where(ok_s, _dot_nt(q4, ks) + bias_s, NEG)
        s_n = jnp.where(ok_n, _dot_nt(q4, kn) + bias_n, NEG)
        m = jnp.maximum(jnp.max(s_s, axis=1, keepdims=True), jnp.max(s_n, axis=1, keepdims=True))
        p_s = jnp.where(ok_s, jnp.exp(s_s - m), 0.0)
        p_n = jnp.where(ok_n, jnp.exp(s_n - m), 0.0)
        l = jnp.sum(p_s, axis=1, keepdims=True) + jnp.sum(p_n, axis=1, keepdims=True)
        o = (_dot(p_s.astype(BF16), vs) + _dot(p_n.astype(BF16), vn)) / l
        for j in range(HPG_C):
            h = HPG_C * g + j
            o_ref[0, :, DH * h:DH * (h + 1)] = o[r * j:r * (j + 1)]


def _win_decode(q, state, new_rows, wbs, wbn, valid):
    batch, wb, _ = state.shape
    r = SAMPLE_ROWS
    return pl.pallas_call(
        functools.partial(_win_decode_body, valid=valid),
        grid=(batch,),
        in_specs=[pl.BlockSpec((1, r, 512), lambda b: (b, 0, 0)),
                  pl.BlockSpec((1, wb, WIN_ROW), lambda b: (b, 0, 0)),
                  pl.BlockSpec((1, r, WIN_ROW), lambda b: (b, 0, 0)),
                  pl.BlockSpec((H_C, r, wb), lambda b: (0, 0, 0)),
                  pl.BlockSpec((H_C, r, r), lambda b: (0, 0, 0))],
        out_specs=pl.BlockSpec((1, r, 512), lambda b: (b, 0, 0)),
        out_shape=jax.ShapeDtypeStruct((batch, r, 512), F32),
        compiler_params=_cparams("arbitrary"),
        name="nsa_win_decode",
    )(q, state, new_rows, wbs, wbn)


def _suffix_sum(x, block=None):
    n = x.shape[0]
    block = block or n
    row = _iota(x.shape, 0) & (block - 1)
    sh = 1
    while sh < block:
        x = x + jnp.where(row < block - sh, pltpu.roll(x, n - sh, axis=0), 0.0)
        sh *= 2
    return x


def _sb_decode_body(pt_ref, wq_ref, new_ref, *rest, pp, valid):
    pages = rest[:pp]
    o_ref, carry_ref, acc_ref, a_ref, v_ref = rest[pp:]
    st = pl.program_id(1)
    r = SAMPLE_ROWS
    half = 64
    wq = wq_ref[0].astype(BF16)
    lane = _iota((1, 128), 1)

    @pl.when(st == 0)
    def _():
        kn = new_ref[0]
        kk = kn[:, 0:256].astype(BF16)
        z = _dot(jnp.concatenate([kk, kk], axis=1), wq)
        lsz = _log2_sigmoid(z)
        row = _iota((r, 128), 0)
        mask = (row < (_iota((r, 128), 1) & (r - 1))) & (row < valid)
        lf = jnp.where(mask, lsz - z, 0.0)
        a = jnp.where(mask, jnp.exp2(lsz + _suffix_sum(lf) - lf), 0.0)
        pad = lambda x: jnp.concatenate([x, jnp.zeros((PAGE - r, x.shape[1]), x.dtype)], axis=0)
        vn = jnp.concatenate([kn[:, 256:512], jnp.zeros((r, 256), F32)], axis=1)
        acc_ref[...] = _dot_tn(pad(a).astype(BF16), pad(vn).astype(BF16))
        carry_ref[...] = jnp.sum(lf, axis=0, keepdims=True)

    carry = carry_ref[...]
    for i in range(pp // 2 - 1, -1, -1):
        early, late = pages[2 * i][0], pages[2 * i + 1][0]
        z = _dot(jnp.concatenate([early[:, 0:256], late[:, 0:256]], axis=1).astype(BF16), wq)
        lsz = _log2_sigmoid(z)
        lf = lsz - z
        suf = _suffix_sum(lf)
        tot = suf[0:1]
        swapped = pltpu.roll(jnp.broadcast_to(tot, (8, 128)), half, axis=1)[0:1]
        between = (suf - lf) + (carry + jnp.where(lane < half, swapped, 0.0))
        a_ref[PAGE * i:PAGE * (i + 1), :] = jnp.exp2(lsz + between).astype(BF16)
        v_ref[PAGE * i:PAGE * (i + 1), :] = jnp.concatenate([early[:, 256:512], late[:, 256:512]],
                                                            axis=1).astype(BF16)
        carry = carry + tot + swapped
    carry_ref[...] = carry
    acc_ref[...] += _dot_tn(a_ref[...], v_ref[...])

    @pl.when(st == pl.num_programs(1) - 1)
    def _():
        o_ref[0] = acc_ref[...]


def _sb_decode(page_table, wq, new_rows, cache, valid):
    batch, n_pages = page_table.shape
    pp = min(2 * PAGES_PER_STEP, n_pages)
    assert n_pages % pp == 0 and pp % 2 == 0
    r = SAMPLE_ROWS
    grid_spec = pltpu.PrefetchScalarGridSpec(
        num_scalar_prefetch=1,
        grid=(batch, n_pages // pp),
        in_specs=[pl.BlockSpec((1, 512, 128), lambda b, s, pt: (b, 0, 0)),
                  pl.BlockSpec((1, r, SB_ROW), lambda b, s, pt: (b, 0, 0))]
        + _page_specs((1, PAGE, SB_ROW), pp, 0, lambda b, s, j, pt: pt[b, n_pages - (s + 1) * pp + j]),
        out_specs=pl.BlockSpec((1, 128, 512), lambda b, s, pt: (b, 0, 0)),
        scratch_shapes=[pltpu.VMEM((1, 128), F32), pltpu.VMEM((128, 512), F32),
                        pltpu.VMEM((pp // 2 * PAGE, 128), BF16), pltpu.VMEM((pp // 2 * PAGE, 512), BF16)])
    return pl.pallas_call(
        functools.partial(_sb_decode_body, pp=pp, valid=valid),
        grid_spec=grid_spec,
        out_shape=jax.ShapeDtypeStruct((batch, 128, 512), F32),
        compiler_params=_cparams("arbitrary", "arbitrary"),
        name="sb_decode",
    )(page_table, wq, new_rows, *([cache] * pp))


def _rope_tables(pos):
    half = ROPE_D // 2
    inv = ROPE_BASE ** (-jnp.arange(half, dtype=F32) / half)
    ang = pos.astype(F32)[:, None] * inv[None, :]
    cos, sin = jnp.cos(ang), jnp.sin(ang)
    return jnp.tile(cos, (1, 8)), jnp.tile(sin, (1, 8))


def _rot_cols(w):
    half = ROPE_D // 2
    return jnp.concatenate([-w[..., half:], w[..., :half]], axis=-1)


def _prep_even(w_in, w_qb, w_kb, w_vb):
    zkr = w_in[:, 2432:2464]
    w_e = jnp.concatenate([w_in[:, :2432], jnp.tile(zkr, (1, 4)), jnp.tile(_rot_cols(zkr), (1, 4))], axis=1)
    wqb = w_qb.reshape(Q_LORA, H_B, NOPE + ROPE_D)
    wr = wqb[:, :, NOPE:]
    w_q = jnp.concatenate([wqb[:, :, :NOPE].reshape(Q_LORA, 512), wr.reshape(Q_LORA, 256),
                           _rot_cols(wr).reshape(Q_LORA, 256)], axis=1)
    eye = jnp.eye(H_B, dtype=F32)
    wkb = jnp.einsum('hcn,hg->hngc', w_kb, eye).reshape(H_B * NOPE, H_B * KV_LORA)
    wvb = jnp.einsum('hcd,hg->hcgd', w_vb, eye).reshape(H_B * KV_LORA, H_B * VD_B)
    return w_e.astype(BF16), w_q.astype(BF16), wkb.astype(BF16), wvb.astype(BF16)


def _prep_odd(w_in):
    w = jnp.concatenate([w_in[:, :1280], w_in[:, 1304:2328], w_in[:, 1280:1304],
                         jnp.zeros((D_MODEL, ODD_COLS - 2328), w_in.dtype)], axis=1)
    return w.astype(BF16)


def _gate_expand():
    e = np.zeros((128, 3 * 512), np.float32)
    for h in range(H_C):
        for j in range(3):
            e[3 * h + j, 512 * j + DH * h:512 * j + DH * (h + 1)] = 1.0
    return jnp.asarray(e, BF16)


def _hgrn_params(hgrn_lb, norm_g, lj):
    lb_all = jax.nn.softmax(hgrn_lb.astype(F32), axis=0)
    lb = (jnp.cumsum(lb_all, axis=0) - lb_all[0])[lj]
    lb_pos = lb > 0
    rows = [jnp.log(jnp.where(lb_pos, lb, 1.0)), jnp.log1p(-lb), lb_pos.astype(F32), 1.0 - lb, norm_g[lj]]
    return jnp.concatenate([jnp.stack(rows), jnp.zeros((3, 512), F32)], axis=0)


def _t5_bucket(dist):
    exact = NUM_BUCKETS // 2
    d = jnp.maximum(dist, 0)
    large = exact + (jnp.log(jnp.maximum(d, 1).astype(F32) / exact)
                     / math.log(MAX_DISTANCE / exact) * (NUM_BUCKETS - exact)).astype(jnp.int32)
    return jnp.where(d < exact, d, jnp.minimum(large, NUM_BUCKETS - 1))


N_DIST = 256


def _toeplitz_t(tab, offset, n):
    i = jnp.arange(n, dtype=jnp.int32)[None, :]
    j = jnp.arange(n, dtype=jnp.int32)[:, None]
    idx = jnp.clip(offset + i - j, 0, N_DIST - 1).reshape(1, n * n)
    onehot = (idx == jnp.arange(N_DIST, dtype=jnp.int32)[:, None]).astype(F32)
    return jnp.dot(tab, onehot, precision=lax.Precision.HIGHEST).reshape(H_C, n, n)


def _decode_bias_rows(tab, base, n_keys, r):
    far = jnp.broadcast_to(tab[:, N_DIST - 1:], (H_C, n_keys + r))
    rev = jnp.concatenate([far, tab[:, ::-1], jnp.zeros((H_C, n_keys + r), F32)], axis=1)
    rows = []
    for t in range(r):
        start = n_keys + r + N_DIST - 1 - (base + t)
        rows.append(rev[:, start:start + n_keys])
    return jnp.stack(rows, axis=1)


def _pad_rows(a, batch, n_tok):
    a = a.reshape(batch, n_tok, -1)
    return jnp.pad(a, ((0, 0), (0, SAMPLE_ROWS - n_tok), (0, 0)))


def _last_rows(rows, n):
    t = rows.shape[1]
    if t < n:
        rows = jnp.pad(rows, ((0, 0), (n - t, 0), (0, 0)))
    return rows[:, rows.shape[1] - n:]


def _run_prompt(x, mod, w, tabs, batch, seq, win_len):
    assert seq % HGRN_CHUNK == 0 and (batch * seq) % _tile_rows(batch * seq) == 0
    tq = min(TQ, seq)
    pos = np.arange(seq)
    cos, sin = _rope_tables(jnp.asarray(pos, jnp.int32))
    tab = tabs['tab']
    tq_att = min(TQ_MLA, seq)
    bdt, bst = _toeplitz_t(tab, 0, tq_att), _toeplitz_t(tab, tq_att, tq_att)
    nblk = seq // NSA_BLOCK
    nbp = -(-nblk // 16) * 16
    reps = tq // NSA_BLOCK
    near = jnp.stack([jnp.tile(tab[:, 1:NSA_BLOCK + 1], (1, reps)),
                      jnp.tile(tab[:, NSA_BLOCK + 1:2 * NSA_BLOCK + 1], (1, reps))])
    states = []
    for l in range(DEPTH):
        lj = l // 2
        m = mod[l]
        x = _ffn(x, m, w['norm_g'][l], w['wg'], w['wu'], w['wd'], l, 0, 0, seq)
        if l % 2 == 0:
            w_e, w_q, wkb, wvb = w['even'][lj]
            zh, kfull, rows, qlat, qrope = _inproj_even(
                x, m, w['norm_g'][l], w_e, cos, sin, w['gq'][lj], w['gkv'][lj], w_q, wkb, seq)
            tb = min(TM_DENSE, seq)
            o_a, s_new = _hgrn(zh, w['hgrn_par'][lj], batch, seq, tb, HGRN_CHUNK, HGRN_SUB, HGRN_CHUNK)
            o_b = _mla_prompt(qlat, qrope, kfull, wvb.T, batch, seq)
            x = _outproj(x, m, [o_a, o_b], [], w['wout_even'][lj], seq, odd=False)
            states += [rows.reshape(batch, seq, MLA_ROW), s_new]
        else:
            qn, nsa, win, qs, sb, zg = _inproj_odd(x, m, w['norm_g'][l], w['odd'][lj], seq)
            cmp = _means_prompt(nsa).reshape(batch, nblk, 256)
            cmp = jnp.pad(cmp, ((0, 0), (0, nbp - nblk), (0, 0)))
            o_cmp, sel = _cmpsel_prompt(qn, cmp, near, tabs['cfar'], batch, seq)
            o_slc, o_win = _nsa_prompt_attn(qn, nsa, win, sel, bdt, bst, tabs['cfar'], batch, seq)
            o_sb = _sb_prompt(qs, sb, batch, seq)
            x = _outproj(x, m, [o_cmp, o_slc, o_win, zg, o_sb], [w['gate_e']], w['wout_odd'][lj], seq, odd=True)
            states += [nsa.reshape(batch, seq, NSA_ROW), _last_rows(win.reshape(batch, seq, WIN_ROW), win_len),
                       sb.reshape(batch, seq, SB_ROW)]
        x = _ffn(x, m, w['norm_g'][l], w['wg'], w['wu'], w['wd'], l, 1, 2, seq,
                 final_g=w['final_g'] if l == DEPTH - 1 else None)
    return x.reshape(batch, seq, D_MODEL), states


def _run_sample(x, mod, w, tabs, batch, n_tok, page_table, pasts):
    n = batch * n_tok
    n_pages = page_table.shape[1]
    p_len = n_pages * PAGE
    r = SAMPLE_ROWS
    assert n_tok <= r and n_tok <= NSA_BLOCK and n == _tile_rows(n)
    pos = p_len + np.arange(r)
    cos, sin = _rope_tables(jnp.asarray(np.tile(pos[:n_tok], batch), jnp.int32))
    tab = tabs['tab']
    nb = p_len // NSA_BLOCK
    n_past_sel = min(TOP_N, nb + 1) - 1
    assert nb >= 2 and p_len % NSA_BLOCK == 0
    cb = jnp.concatenate([jnp.broadcast_to(tab[:, None, N_DIST - 1:], (H_C, r, nb - 2)),
                          tab[:, NSA_BLOCK + 1:NSA_BLOCK + 1 + r, None], tab[:, 1:1 + r, None]], axis=2)
    states = []
    for l in range(DEPTH):
        lj = l // 2
        m = mod[l]
        x = _ffn(x, m, w['norm_g'][l], w['wg'], w['wu'], w['wd'], l, 0, 0, n)
        if l % 2 == 0:
            cache_mla, state_hgrn = pasts[l]
            w_e, w_q, wkb, wvb = w['even'][lj]
            zh, kfull, rows, qlat, qrope = _inproj_even(
                x, m, w['norm_g'][l], w_e, cos, sin, w['gq'][lj], w['gkv'][lj], w_q, wkb, n)
            zh8 = _pad_rows(zh, batch, n_tok).reshape(batch * r, 2048)
            o_a8, s_new = _hgrn(zh8, w['hgrn_par'][lj], batch, r, r, r, r, n_tok, s0=state_hgrn)
            o_a = o_a8.reshape(batch, r, 512)[:, :n_tok].reshape(n, 512)
            qf = jnp.concatenate([qlat.reshape(batch, n_tok, H_B, KV_LORA),
                                  qrope.reshape(batch, n_tok, H_B, ROPE_D)], axis=-1)
            qf = jnp.pad(qf, ((0, 0), (0, r - n_tok), (0, 0), (0, 0)))
            qf = jnp.swapaxes(qf, 1, 2).reshape(batch, H_B * r, MLA_ROW)
            o_b8 = _mla_decode(page_table, qf, _pad_rows(rows, batch, n_tok), wvb, jnp.swapaxes(cache_mla, 1, 2),
                               n_tok)
            o_b = o_b8[:, :n_tok].reshape(n, 512)
            x = _outproj(x, m, [o_a, o_b], [], w['wout_even'][lj], n, odd=False)
            states += [rows.reshape(batch, n_tok, MLA_ROW), s_new]
        else:
            cache_nsa, state_win, cache_sb = pasts[l]
            wb = state_win.shape[1]
            assert p_len >= wb and wb <= WINDOW
            qn, nsa, win, qs, sb, zg = _inproj_odd(x, m, w['norm_g'][l], w['odd'][lj], n)
            qn8 = _pad_rows(qn, batch, n_tok)
            nsa8 = _pad_rows(nsa, batch, n_tok)
            cmp = _means_decode(page_table, cache_nsa)
            o_cmp8, picked = _cmpsel_decode(qn8, cmp, cb, n_past_sel)
            idx = picked[..., :n_past_sel].reshape(-1)
            o_slc = _slc_decode(idx, page_table, qn.reshape(batch, n_tok, H_C, DH), nsa8, tabs['tab3'],
                                cache_nsa, n_tok, p_len)
            wbs = _decode_bias_rows(tab, wb, wb, r)
            wbn = _decode_bias_rows(tab, 0, r, r)
            o_win8 = _win_decode(qn8, state_win, _pad_rows(win, batch, n_tok), wbs, wbn, n_tok)
            q5 = _pad_rows(qs * (ATT_SCALE * LOG2E), batch, n_tok).reshape(batch, r, KV_D, HPK_D, DH)
            q5 = jnp.transpose(q5, (0, 2, 4, 3, 1)).reshape(batch, KV_D, DH, HPK_D * r)
            wq = jnp.einsum('bgdc,gh->bgdhc', q5, jnp.eye(KV_D, dtype=F32)).reshape(batch, KV_D * DH, KV_D * HPK_D * r)
            nc = KV_D * HPK_D * r
            zc = jnp.zeros_like(wq)
            wq = jnp.concatenate([jnp.concatenate([wq, zc], axis=2), jnp.concatenate([zc, wq], axis=2)], axis=1)
            o_raw = _sb_decode(page_table, wq, _pad_rows(sb, batch, n_tok), cache_sb, n_tok)
            o6 = (o_raw[:, :nc, :256] + o_raw[:, nc:, 256:]).reshape(batch, KV_D, HPK_D, r, KV_D, DH)
            o_sb = jnp.stack([o6[:, g, :, :, g] for g in range(KV_D)], axis=1)
            o_sb = jnp.transpose(o_sb, (0, 3, 1, 2, 4))[:, :n_tok].reshape(n, 512)
            take = lambda a: a[:, :n_tok].reshape(n, 512)
            x = _outproj(x, m, [take(o_cmp8), o_slc.reshape(n, 512), take(o_win8), zg, o_sb], [w['gate_e']],
                         w['wout_odd'][lj], n, odd=True)
            new_win = jnp.concatenate([state_win, win.reshape(batch, n_tok, WIN_ROW)], axis=1)[:, n_tok:]
            states += [nsa.reshape(batch, n_tok, NSA_ROW), new_win, sb.reshape(batch, n_tok, SB_ROW)]
        x = _ffn(x, m, w['norm_g'][l], w['wg'], w['wu'], w['wd'], l, 1, 2, n,
                 final_g=w['final_g'] if l == DEPTH - 1 else None)
    return x.reshape(batch, n_tok, D_MODEL), states


def kernel(x_prompt, x_sample, cache_mla_l0, state_hgrn_l0, cache_nsa_l1, state_win_l1, cache_sb_l1, cache_mla_l2, state_hgrn_l2, cache_nsa_l3, state_win_l3, cache_sb_l3, page_table, c_prompt, c_sample, w_ada, b_ada, norm_g, ffn_w_gate, ffn_w_up, ffn_w_down, w_in_even, w_out_even, hgrn_lb, hgrn_norm_g, mla_q_norm_g, mla_kv_norm_g, mla_w_qb, mla_w_kb, mla_w_vb, w_in_odd, w_out_odd, rel_bias, final_norm_g):
    bp, seq, _ = x_prompt.shape
    bs, n_tok, _ = x_sample.shape
    n_even = w_in_even.shape[0]
    n_odd = w_in_odd.shape[0]
    w = dict(
        norm_g=norm_g, final_g=final_norm_g,
        wg=ffn_w_gate.astype(BF16), wu=ffn_w_up.astype(BF16), wd=ffn_w_down.astype(BF16),
        even=[_prep_even(w_in_even[j], mla_w_qb[j], mla_w_kb[j], mla_w_vb[j]) for j in range(n_even)],
        odd=[_prep_odd(w_in_odd[j]) for j in range(n_odd)],
        wout_even=w_out_even.astype(BF16), wout_odd=w_out_odd.astype(BF16),
        gq=mla_q_norm_g.reshape(n_even, 1, Q_LORA), gkv=mla_kv_norm_g.reshape(n_even, 1, KV_LORA),
        hgrn_par=[_hgrn_params(hgrn_lb, hgrn_norm_g, j) for j in range(n_even)],
        gate_e=_gate_expand())
    tab = rel_bias[_t5_bucket(jnp.arange(N_DIST, dtype=jnp.int32))].T.astype(F32)
    cfar = jnp.concatenate([tab[:, N_DIST - 1:], tab[:, 0:1], jnp.zeros((H_C, 126), F32)], axis=1)
    tabs = dict(tab=tab, cfar=cfar, tab3=jnp.concatenate(_split3(tab), axis=0))
    mod = _ada_mod(jnp.concatenate([c_prompt, c_sample], axis=0), w_ada, b_ada)
    mod = mod.reshape(DEPTH, bp + bs, N_MOD, D_MODEL)
    mod_p = mod[:, :bp, :, None, :]
    mod_s = jnp.transpose(jnp.repeat(mod[:, bp:], n_tok, axis=1), (0, 2, 1, 3))[:, None]
    win_len = state_win_l1.shape[1]
    y_p, st_p = _run_prompt(x_prompt.reshape(bp * seq, D_MODEL), mod_p, w, tabs, bp, seq, win_len)
    pasts = [(cache_mla_l0, state_hgrn_l0), (cache_nsa_l1, state_win_l1, cache_sb_l1),
             (cache_mla_l2, state_hgrn_l2), (cache_nsa_l3, state_win_l3, cache_sb_l3)]
    y_s, st_s = _run_sample(x_sample.reshape(bs * n_tok, D_MODEL), mod_s, w, tabs, bs, n_tok, page_table, pasts)
    out = [y_p, y_s]
    for a, b in zip(st_p, st_s):
        out += [a, b]
    return tuple(out)
```

```python
import functools
import math

import numpy as np
import jax
import jax.numpy as jnp
from jax import lax
from jax.experimental import pallas as pl
from jax.experimental.pallas import tpu as pltpu

F32 = jnp.float32
BF16 = jnp.bfloat16

D_MODEL = 1024
DEPTH = 4
PAGE = 128
NORM_EPS = 1e-6
NEG = -1e30
PICKED = -3e38
N_MOD = 9
D_FF = 2816

H_A = 4
DK_A = 128
HGRN_CHUNK = 64
HGRN_SUB = 16

H_B = 8
Q_LORA = 256
KV_LORA = 128
NOPE = 64
ROPE_D = 32
VD_B = 64
ROPE_BASE = 10000.0
MLA_ROW = KV_LORA + ROPE_D
MLA_SCALE = (NOPE + ROPE_D) ** -0.5

DH = 64
H_C = 8
G_C = 2
HPG_C = 4
NSA_BLOCK = 64
TOP_N = 8
WINDOW = 512
FORCE_BONUS = 100.0
NSA_ROW = 512
WIN_ROW = 256
H_D = 8
KV_D = 4
HPK_D = 2
SB_ROW = 512
ATT_SCALE = DH ** -0.5
NUM_BUCKETS = 32
MAX_DISTANCE = 128

V7X_VMEM_BYTES = 64 * 1024 * 1024
VMEM_LIMIT = V7X_VMEM_BYTES - 8 * 1024 * 1024

TM_DENSE = 512
TM_FFN = 1024
FF_CHUNK = 1408
TQ = 256
TQ_MLA = 512
SAMPLE_ROWS = 8
PAGES_PER_STEP = 16


def _cparams(*sem):
    return pltpu.CompilerParams(dimension_semantics=sem, vmem_limit_bytes=VMEM_LIMIT)


def _iota(shape, dim):
    return lax.broadcasted_iota(jnp.int32, shape, dim)


def _rms(x):
    return x * lax.rsqrt(jnp.mean(x * x, axis=-1, keepdims=True) + NORM_EPS)


def _silu(x):
    return x * jax.nn.sigmoid(x)


def _softplus_neg_abs(x):
    return jnp.log(1.0 + jnp.exp(-jnp.abs(x)))


def _log_sigmoid(x):
    return jnp.minimum(x, 0.0) - _softplus_neg_abs(x)


LOG2E = math.log2(math.e)


def _log2_sigmoid(x2):
    return jnp.minimum(x2, 0.0) - jnp.log2(1.0 + jnp.exp2(-jnp.abs(x2)))


def _dot(a, b):
    return jnp.dot(a, b, preferred_element_type=F32)


def _dot_nt(a, b):
    return lax.dot_general(a, b, (((1,), (1,)), ((), ())), preferred_element_type=F32)


def _dot_tn(a, b):
    return lax.dot_general(a, b, (((0,), (0,)), ((), ())), preferred_element_type=F32)


def _split3(x):
    hi = x.astype(BF16)
    r = x - hi.astype(F32)
    mid = r.astype(BF16)
    lo = (r - mid.astype(F32)).astype(BF16)
    return hi, mid, lo


def _ada_body(c_ref, w_ref, b_ref, o_ref):
    h = _silu(c_ref[...]).astype(BF16)
    o_ref[0] = _dot(h, w_ref[0].astype(BF16)) + b_ref[0]


def _ada_mod(c_all, w_ada, b_ada):
    nc = c_all.shape[0]
    depth, _, ncol = w_ada.shape
    tn = 1024
    return pl.pallas_call(
        _ada_body,
        grid=(depth, ncol // tn),
        in_specs=[pl.BlockSpec((nc, D_MODEL), lambda l, j: (0, 0)),
                  pl.BlockSpec((1, D_MODEL, tn), lambda l, j: (l, 0, j)),
                  pl.BlockSpec((1, 1, tn), lambda l, j: (l, 0, j))],
        out_specs=pl.BlockSpec((1, nc, tn), lambda l, j: (l, 0, j)),
        out_shape=jax.ShapeDtypeStruct((depth, nc, ncol), F32),
        compiler_params=_cparams("arbitrary", "arbitrary"),
        name="ada_mod",
    )(c_all, w_ada, b_ada.reshape(depth, 1, ncol))


def _tile_rows(n):
    return min(TM_DENSE, n)


def _mod_spec(mod, tm, seq_len):
    s, _, r, _ = mod.shape
    if r == 1:
        tiles_per_seq = seq_len // tm
        return pl.BlockSpec((1, N_MOD, 1, D_MODEL), lambda i: (i // tiles_per_seq, 0, 0, 0))
    return pl.BlockSpec((1, N_MOD, tm, D_MODEL), lambda i: (0, 0, i, 0))


def _prenorm(x, mod_ref, g_ref, sub):
    shift = mod_ref[0, 3 * sub]
    scale = mod_ref[0, 3 * sub + 1]
    return _rms(x) * g_ref[sub:sub + 1, :] * (1.0 + scale) + shift


def _const_spec(shape):
    nd = len(shape)
    return pl.BlockSpec(shape, lambda i: (0,) * nd, pipeline_mode=pl.Buffered(1))


def _ffn_body(x_ref, mod_ref, g_ref, wg_ref, wu_ref, wd_ref, *rest, sub, final):
    o_ref = rest[-1]
    x = x_ref[...]
    h = _prenorm(x, mod_ref, g_ref, sub).astype(BF16)
    acc = None
    for c0 in range(0, D_FF, FF_CHUNK):
        a = _dot(h, wg_ref[:, c0:c0 + FF_CHUNK])
        u = _dot(h, wu_ref[:, c0:c0 + FF_CHUNK])
        t = (_silu(a) * u).astype(BF16)
        part = _dot(t, wd_ref[c0:c0 + FF_CHUNK, :])
        acc = part if acc is None else acc + part
    y = x + 0.5 * mod_ref[0, 3 * sub + 2] * acc
    if final:
        y = _rms(y) * rest[0][...]
    o_ref[...] = y


def _ffn(x, mod, norm_g, wg, wu, wd, layer, which, sub, seq_len, final_g=None):
    n = x.shape[0]
    tm = min(TM_FFN, n)
    final = final_g is not None
    pick = lambda r, c: pl.BlockSpec((None, None, r, c), lambda i: (layer, which, 0, 0),
                                     pipeline_mode=pl.Buffered(1))
    in_specs = [pl.BlockSpec((tm, D_MODEL), lambda i: (i, 0)),
                _mod_spec(mod, tm, seq_len),
                _const_spec((3, D_MODEL)),
                pick(D_MODEL, D_FF), pick(D_MODEL, D_FF), pick(D_FF, D_MODEL)]
    args = [x, mod, norm_g, wg, wu, wd]
    if final:
        in_specs.append(_const_spec((1, D_MODEL)))
        args.append(final_g.reshape(1, D_MODEL))
    return pl.pallas_call(
        functools.partial(_ffn_body, sub=sub, final=final),
        grid=(n // tm,),
        in_specs=in_specs,
        out_specs=pl.BlockSpec((tm, D_MODEL), lambda i: (i, 0)),
        out_shape=jax.ShapeDtypeStruct((n, D_MODEL), F32),
        compiler_params=_cparams("arbitrary"),
        name="ffn",
    )(*args)


EVEN_COLS = 2688
ODD_COLS = 2432


def _inproj_even_body(x_ref, mod_ref, g_ref, w_ref, cos_ref, sin_ref, gq_ref, gkv_ref, wq_ref, wkb_ref,
                      zh_ref, kfull_ref, rows_ref, qlat_ref, qrope_ref):
    h = _prenorm(x_ref[...], mod_ref, g_ref, 1).astype(BF16)
    zh_ref[...] = _dot(h, w_ref[:, 0:2048])
    z = _dot(h, w_ref[:, 2048:EVEN_COLS])
    cos = cos_ref[...]
    sin = sin_ref[...]
    ckv = _rms(z[:, 256:384]) * gkv_ref[...]
    krope = z[:, 384:512] * cos + z[:, 512:640] * sin
    kfull_ref[:, 0:128] = ckv
    kfull_ref[:, 128:256] = krope
    rows_ref[:, 0:128] = ckv
    rows_ref[:, 128:MLA_ROW] = krope[:, 0:ROPE_D]
    qa = (_rms(z[:, 0:256]) * gq_ref[...]).astype(BF16)
    qz = _dot(qa, wq_ref[...])
    qlat_ref[...] = _dot(qz[:, 0:512].astype(BF16), wkb_ref[...]) * MLA_SCALE
    cos2 = jnp.concatenate([cos, cos], axis=1)
    sin2 = jnp.concatenate([sin, sin], axis=1)
    qrope_ref[...] = (qz[:, 512:768] * cos2 + qz[:, 768:1024] * sin2) * MLA_SCALE


def _inproj_even(x, mod, norm_g, w, cos, sin, gq, gkv, wq, wkb, seq_len):
    n = x.shape[0]
    tm = _tile_rows(n)
    tab_tiles = cos.shape[0] // tm
    tok = lambda c: pl.BlockSpec((tm, c), lambda i: (i, 0))
    tab = pl.BlockSpec((tm, 128), lambda i: (i % tab_tiles, 0))
    return pl.pallas_call(
        _inproj_even_body,
        grid=(n // tm,),
        in_specs=[tok(D_MODEL), _mod_spec(mod, tm, seq_len), _const_spec((3, D_MODEL)),
                  _const_spec((D_MODEL, EVEN_COLS)), tab, tab,
                  _const_spec((1, Q_LORA)), _const_spec((1, KV_LORA)),
                  _const_spec((Q_LORA, 1024)), _const_spec((512, 1024))],
        out_specs=[tok(2048), tok(256), tok(MLA_ROW), tok(1024), tok(256)],
        out_shape=[jax.ShapeDtypeStruct((n, c), F32) for c in (2048, 256, MLA_ROW, 1024, 256)],
        compiler_params=_cparams("arbitrary"),
        name="inproj_even",
    )(x, mod, norm_g, w, cos, sin, gq, gkv, wq, wkb)


def _inproj_odd_body(x_ref, mod_ref, g_ref, w_ref, qn_ref, nsa_ref, win_ref, qs_ref, sb_ref, zg_ref):
    h = _prenorm(x_ref[...], mod_ref, g_ref, 1).astype(BF16)
    qn_ref[...] = _dot(h, w_ref[:, 0:512])
    nsa_ref[...] = _dot(h, w_ref[:, 512:1024])
    win_ref[...] = _dot(h, w_ref[:, 1024:1280])
    qs_ref[...] = _dot(h, w_ref[:, 1280:1792])
    sb_ref[...] = _dot(h, w_ref[:, 1792:2304])
    zg_ref[...] = _dot(h, w_ref[:, 2304:ODD_COLS])


def _inproj_odd(x, mod, norm_g, w, seq_len):
    n = x.shape[0]
    tm = _tile_rows(n)
    tok = lambda c: pl.BlockSpec((tm, c), lambda i: (i, 0))
    cols = (512, NSA_ROW, WIN_ROW, 512, SB_ROW, 128)
    return pl.pallas_call(
        _inproj_odd_body,
        grid=(n // tm,),
        in_specs=[tok(D_MODEL), _mod_spec(mod, tm, seq_len), _const_spec((3, D_MODEL)),
                  _const_spec((D_MODEL, ODD_COLS))],
        out_specs=[tok(c) for c in cols],
        out_shape=[jax.ShapeDtypeStruct((n, c), F32) for c in cols],
        compiler_params=_cparams("arbitrary"),
        name="inproj_odd",
    )(x, mod, norm_g, w)


def _outproj_even_body(x_ref, mod_ref, oa_ref, ob_ref, w_ref, o_ref):
    mix = (_dot(oa_ref[...].astype(BF16), w_ref[0:512, :])
           + _dot(ob_ref[...].astype(BF16), w_ref[512:1024, :]))
    o_ref[...] = x_ref[...] + mod_ref[0, 5] * mix


def _outproj_odd_body(x_ref, mod_ref, ocmp_ref, oslc_ref, owin_ref, zg_ref, osb_ref, e_ref, w_ref, o_ref):
    hi, mid, lo = _split3(jax.nn.sigmoid(zg_ref[...]))
    e = e_ref[...]
    gexp = _dot(hi, e) + _dot(mid, e) + _dot(lo, e)
    nsa = (gexp[:, 0:512] * ocmp_ref[...] + gexp[:, 512:1024] * oslc_ref[...]
           + gexp[:, 1024:1536] * owin_ref[...])
    mix = (_dot(nsa.astype(BF16), w_ref[0:512, :])
           + _dot(osb_ref[...].astype(BF16), w_ref[512:1024, :]))
    o_ref[...] = x_ref[...] + mod_ref[0, 5] * mix


def _outproj(x, mod, parts, consts, w, seq_len, odd):
    n = x.shape[0]
    tm = _tile_rows(n)
    tok = lambda c: pl.BlockSpec((tm, c), lambda i: (i, 0))
    in_specs = ([tok(D_MODEL), _mod_spec(mod, tm, seq_len)] + [tok(p.shape[1]) for p in parts]
                + [_const_spec(c.shape) for c in consts] + [_const_spec((1024, D_MODEL))])
    return pl.pallas_call(
        _outproj_odd_body if odd else _outproj_even_body,
        grid=(n // tm,),
        in_specs=in_specs,
        out_specs=tok(D_MODEL),
        out_shape=jax.ShapeDtypeStruct((n, D_MODEL), F32),
        compiler_params=_cparams("arbitrary"),
        name="outproj_odd" if odd else "outproj_even",
    )(x, mod, *parts, *consts, w)


def _softmax_init(m_ref, l_ref, acc_ref):
    m_ref[...] = jnp.full(m_ref.shape, NEG, F32)
    l_ref[...] = jnp.zeros(l_ref.shape, F32)
    acc_ref[...] = jnp.zeros(acc_ref.shape, F32)


def _softmax_step(s, mask, v, m_ref, l_ref, acc_ref, v_is_transposed=False):
    if mask is not None:
        s = jnp.where(mask, s, NEG)
    m_prev = m_ref[...]
    m_new = jnp.maximum(m_prev, jnp.max(s, axis=1, keepdims=True))
    p = jnp.exp(s - m_new)
    if mask is not None:
        p = jnp.where(mask, p, 0.0)
    alpha = jnp.exp(m_prev - m_new)
    l_ref[...] = alpha * l_ref[...] + jnp.sum(p, axis=1, keepdims=True)
    pv = _dot_nt(p.astype(BF16), v) if v_is_transposed else _dot(p.astype(BF16), v)
    acc_ref[...] = alpha * acc_ref[...] + pv
    m_ref[...] = m_new


def _softmax_out(l_ref, acc_ref):
    l = l_ref[...]
    return jnp.where(l > 0.0, acc_ref[...] / jnp.where(l > 0.0, l, 1.0), 0.0)


def _tsoftmax_step(s_t, mask_t, v_t, m_ref, l_ref, acc_ref):
    if mask_t is not None:
        s_t = jnp.where(mask_t, s_t, NEG)
    m_prev = m_ref[...]
    m_new = jnp.maximum(m_prev, jnp.max(s_t, axis=0, keepdims=True))
    p = jnp.exp(s_t - m_new)
    alpha = jnp.exp(m_prev - m_new)
    l_ref[...] = alpha * l_ref[...] + jnp.sum(p, axis=0, keepdims=True)
    acc_ref[...] = alpha * acc_ref[...] + _dot(v_t, p.astype(BF16))
    m_ref[...] = m_new


def _tsoftmax_out(l_ref, acc_ref):
    return acc_ref[...] * (1.0 / l_ref[...])


def _mla_stack_q_t(qlat, qrope):
    tq = qlat.shape[0]
    ql_t = qlat.T
    qr_t = qrope.T
    sub = _iota((128, tq), 0)
    parts = []
    for h in range(H_B):
        qr = qr_t[128 * (h // 4):128 * (h // 4 + 1)]
        qr = jnp.where((sub // ROPE_D) == (h % 4), qr, 0.0)
        parts.append(jnp.concatenate([ql_t[128 * h:128 * (h + 1)], qr], axis=0))
    return jnp.concatenate(parts, axis=1).astype(BF16)


def _mla_prompt_body(qlat_ref, qrope_ref, kfull_ref, wvbt_ref, o_ref, k_scr, vt_scr, m_ref, l_ref, acc_ref, *, tq):
    qi = pl.program_id(1)

    @pl.when(qi == 0)
    def _():
        kf = kfull_ref[0]
        k_scr[...] = kf.astype(BF16)
        for kb in range(kf.shape[0] // tq):
            vt_scr[kb] = kf[kb * tq:(kb + 1) * tq, 0:KV_LORA].T.astype(BF16)

    q_t = _mla_stack_q_t(qlat_ref[...], qrope_ref[...])
    _softmax_init(m_ref, l_ref, acc_ref)
    shape = (tq, H_B * tq)

    def step(kb, diag):
        k = k_scr[pl.ds(pl.multiple_of(kb * tq, tq), tq), :]
        mask = (_iota(shape, 0) <= (_iota(shape, 1) & (tq - 1))) if diag else None
        _tsoftmax_step(_dot(k, q_t), mask, vt_scr[kb], m_ref, l_ref, acc_ref)

    def far(kb, c):
        step(kb, False)
        return c

    lax.fori_loop(0, qi, far, 0)
    step(qi, True)
    o_t = _tsoftmax_out(l_ref, acc_ref)
    olat_t = jnp.concatenate([o_t[:, h * tq:(h + 1) * tq] for h in range(H_B)], axis=0)
    o_ref[...] = _dot(wvbt_ref[...], olat_t.astype(BF16)).T


def _mla_prompt(qlat, qrope, kfull, wvbt, batch, seq):
    tq = min(TQ_MLA, seq)
    assert tq & (tq - 1) == 0 and seq % tq == 0
    nq = seq // tq
    tok = lambda c: pl.BlockSpec((tq, c), lambda b, i: (b * nq + i, 0))
    return pl.pallas_call(
        functools.partial(_mla_prompt_body, tq=tq),
        grid=(batch, nq),
        in_specs=[tok(1024), tok(256), pl.BlockSpec((1, seq, 256), lambda b, i: (b, 0, 0)),
                  pl.BlockSpec((512, 1024), lambda b, i: (0, 0))],
        out_specs=tok(512),
        out_shape=jax.ShapeDtypeStruct((batch * seq, 512), F32),
        scratch_shapes=[pltpu.VMEM((seq, 256), BF16), pltpu.VMEM((nq, KV_LORA, tq), BF16),
                        pltpu.VMEM((1, H_B * tq), F32), pltpu.VMEM((1, H_B * tq), F32),
                        pltpu.VMEM((KV_LORA, H_B * tq), F32)],
        compiler_params=_cparams("arbitrary", "arbitrary"),
        name="mla_prompt",
    )(qlat, qrope, kfull.reshape(batch, seq, 256), wvbt)


def _hgrn_chunk(zq, zf, zi, zg, par, st, chunk, sub, valid):
    log_lb, log1m_lb, lb_pos, one_m_lb, gn = par[0:1], par[1:2], par[2:3], par[3:4], par[4:5]
    q = _silu(zq)
    ls = _log_sigmoid(zf)
    b = log1m_lb + ls
    lae = jnp.maximum(log_lb, b) + _softplus_neg_abs(log_lb - b)
    logf = jnp.where(lb_pos > 0.5, lae, ls)
    kin = one_m_lb * jax.nn.sigmoid(-zf)
    v = zi
    row = _iota((chunk, 128), 0)
    g = logf
    sh = 1
    while sh < chunk:
        g = g + jnp.where(row >= sh, pltpu.roll(g, sh, axis=0), 0.0)
        sh *= 2
    o = _dot_nt((q * jnp.exp(g)).astype(BF16), st.astype(BF16))
    nsub = chunk // sub
    v_bf = v.astype(BF16)
    if nsub > 1:
        ends = [g[sub * j + sub - 1:sub * j + sub] for j in range(nsub)]
        esub = jnp.concatenate([jnp.broadcast_to(e, (sub, 128)) for e in ends], axis=0)
        kt = (kin * jnp.exp(esub - g)).astype(BF16)
        col_a = _iota((chunk, chunk), 1)
        row_a = _iota((chunk, chunk), 0)
        a = jnp.zeros((chunk, chunk), F32)
        for j in range(nsub - 1):
            qj = (q * jnp.exp(jnp.minimum(g - ends[j], 0.0))).astype(BF16)
            aj = _dot_nt(qj, kt)
            a = jnp.where((col_a >= sub * j) & (col_a < sub * (j + 1)) & (row_a >= sub * (j + 1)), aj, a)
        o = o + _dot(a.astype(BF16), v_bf)
    row_s = _iota((sub, 128), 0)
    diag = []
    for i in range(nsub):
        g_i = g[sub * i:sub * (i + 1)]
        q_i = q[sub * i:sub * (i + 1)]
        o_i = jnp.zeros((sub, 128), F32)
        for s in range(min(sub, valid - sub * i)):
            r = sub * i + s
            e = jnp.exp(jnp.minimum(g_i - g[r:r + 1], 0.0))
            x = jnp.where(row_s >= s, q_i * (kin[r:r + 1] * e), 0.0)
            o_i = o_i + jnp.sum(x, axis=1, keepdims=True) * v[r:r + 1]
        diag.append(o_i)
    o = o + (diag[0] if nsub == 1 else jnp.concatenate(diag, axis=0))
    g_last = g[valid - 1:valid]
    khat = kin * jnp.exp(jnp.minimum(g_last - g, 0.0))
    if valid < chunk:
        khat = jnp.where(row < valid, khat, 0.0)
    st_new = st * jnp.exp(g_last) + _dot_tn(v_bf, khat.astype(BF16))
    return _rms(o) * gn * _silu(zg), st_new


def _hgrn_body(zq_ref, zf_ref, zi_ref, zg_ref, par_ref, *rest, chunk, sub, valid, n_chunks, has_s0):
    if has_s0:
        s0_ref, o_ref, sout_ref, st_ref = rest
    else:
        o_ref, sout_ref, st_ref = rest
    tb = pl.program_id(1)

    @pl.when(tb == 0)
    def _():
        for h in range(H_A):
            st_ref[h] = s0_ref[0, h].T if has_s0 else jnp.zeros((128, 128), F32)

    def chunk_body(c, carry):
        r0 = pl.multiple_of(c * chunk, chunk)
        for h in range(H_A):
            cs = slice(128 * h, 128 * (h + 1))
            rs = pl.ds(r0, chunk)
            o, st_new = _hgrn_chunk(zq_ref[rs, cs], zf_ref[rs, cs], zi_ref[rs, cs], zg_ref[rs, cs],
                                    par_ref[:, cs], st_ref[h], chunk, sub, valid)
            o_ref[rs, cs] = o
            st_ref[h] = st_new
        return carry

    lax.fori_loop(0, n_chunks, chunk_body, 0)

    @pl.when(tb == pl.num_programs(1) - 1)
    def _():
        for h in range(H_A):
            sout_ref[0, h] = st_ref[h].T


def _hgrn(zh, par, batch, rows_per_seq, tb, chunk, sub, valid, s0=None):
    nt = rows_per_seq // tb
    col = lambda j: pl.BlockSpec((tb, 512), lambda b, t: (b * nt + t, j))
    in_specs = [col(0), col(1), col(2), col(3), pl.BlockSpec((8, 512), lambda b, t: (0, 0))]
    args = [zh, zh, zh, zh, par]
    st_spec = pl.BlockSpec((1, H_A, 128, 128), lambda b, t: (b, 0, 0, 0))
    if s0 is not None:
        in_specs.append(st_spec)
        args.append(s0)
    return pl.pallas_call(
        functools.partial(_hgrn_body, chunk=chunk, sub=sub, valid=valid, n_chunks=tb // chunk,
                          has_s0=s0 is not None),
        grid=(batch, nt),
        in_specs=in_specs,
        out_specs=[pl.BlockSpec((tb, 512), lambda b, t: (b * nt + t, 0)), st_spec],
        out_shape=[jax.ShapeDtypeStruct((batch * rows_per_seq, 512), F32),
                   jax.ShapeDtypeStruct((batch, H_A, 128, 128), F32)],
        scratch_shapes=[pltpu.VMEM((H_A, 128, 128), F32)],
        compiler_params=_cparams("arbitrary", "arbitrary"),
        name="hgrn",
    )(*args)


def _block_means(x):
    nb = x.shape[0] // NSA_BLOCK
    return jnp.sum(x.reshape(nb, NSA_BLOCK, x.shape[1]), axis=1) * (1.0 / NSA_BLOCK)


def _means_prompt_body(x_ref, o_ref):
    o_ref[...] = _block_means(x_ref[...])


def _means_prompt(nsa_rows):
    n = nsa_rows.shape[0]
    tm = _tile_rows(n)
    return pl.pallas_call(
        _means_prompt_body,
        grid=(n // tm,),
        in_specs=[pl.BlockSpec((tm, 256), lambda i: (i, 0))],
        out_specs=pl.BlockSpec((tm // NSA_BLOCK, 256), lambda i: (i, 0)),
        out_shape=jax.ShapeDtypeStruct((n // NSA_BLOCK, 256), F32),
        compiler_params=_cparams("arbitrary"),
        name="nsa_means_prompt",
    )(nsa_rows)


def _cmp_head(qh, ck, cv, bias, vis):
    s = _dot_nt(qh.astype(BF16), ck) * ATT_SCALE + bias
    if vis is not None:
        s = jnp.where(vis, s, NEG)
    e = jnp.exp(s - jnp.max(s, axis=1, keepdims=True))
    p = e / jnp.sum(e, axis=1, keepdims=True)
    if vis is not None:
        p = jnp.where(vis, p, 0.0)
    return _dot(p.astype(BF16), cv), p


def _top_blocks(score, blk, n_sel):
    nb = score.shape[1]
    for _ in range(n_sel):
        m = jnp.max(score, axis=1, keepdims=True)
        idx = jnp.min(jnp.where(score == m, blk, nb), axis=1, keepdims=True)
        yield idx, m > 0.5 * NEG
        score = jnp.where(blk == idx, PICKED, score)


def _cmpsel_prompt_body(q_ref, cmp_ref, near_ref, cfar_ref, ocmp_ref, sel_ref, *, tq, nbp, n_sel):
    qi = pl.program_id(1)
    q_t = q_ref[...].T
    shape = (nbp, tq)
    qpos = qi * tq + _iota(shape, 1)
    blk = _iota(shape, 0)
    vis = (blk * NSA_BLOCK + NSA_BLOCK - 1) <= qpos
    cur = jnp.right_shift(qpos, 6)
    allowed = blk <= cur
    forced = (blk == 0) | (blk == cur) | (blk == cur - 1)
    near1 = blk == cur - 1
    near2 = blk == cur - 2
    outs = []
    for g in range(G_C):
        ck = cmp_ref[0, :, DH * g:DH * (g + 1)].astype(BF16)
        cv_t = cmp_ref[0, :, 128 + DH * g:128 + DH * (g + 1)].T.astype(BF16)
        imp = jnp.zeros(shape, F32)
        for j in range(HPG_C):
            h = HPG_C * g + j
            bias = jnp.where(near1, near_ref[0, h:h + 1, :],
                             jnp.where(near2, near_ref[1, h:h + 1, :], cfar_ref[h:h + 1, 0:1]))
            bias = jnp.where(blk == cur, cfar_ref[h:h + 1, 1:2], bias)
            s = _dot(ck, q_t[DH * h:DH * (h + 1)].astype(BF16)) * ATT_SCALE + bias
            s = jnp.where(vis, s, NEG)
            e = jnp.exp(s - jnp.max(s, axis=0, keepdims=True))
            p = jnp.where(vis, e / jnp.sum(e, axis=0, keepdims=True), 0.0)
            outs.append(_dot(cv_t, p.astype(BF16)))
            imp = imp + p
        score = jnp.where(allowed, imp + jnp.where(forced, FORCE_BONUS, 0.0), NEG)
        sel = jnp.zeros(shape, F32)
        for _ in range(n_sel):
            m = jnp.max(score, axis=0, keepdims=True)
            idx = jnp.min(jnp.where(score == m, blk, nbp), axis=0, keepdims=True)
            hit = blk == idx
            sel = jnp.where(hit & (m > 0.5 * NEG), 1.0, sel)
            score = jnp.where(hit, PICKED, score)
        sel_ref[0, g] = sel.astype(BF16)
    ocmp_ref[...] = jnp.concatenate(outs, axis=0).T


def _cmpsel_prompt(q, cmp, near, cfar, batch, seq):
    tq = min(TQ, seq)
    nq = seq // tq
    nbp = cmp.shape[1]
    n_sel = min(TOP_N, -(-seq // NSA_BLOCK))
    return pl.pallas_call(
        functools.partial(_cmpsel_prompt_body, tq=tq, nbp=nbp, n_sel=n_sel),
        grid=(batch, nq),
        in_specs=[pl.BlockSpec((tq, 512), lambda b, i: (b * nq + i, 0)),
                  pl.BlockSpec((1, nbp, 256), lambda b, i: (b, 0, 0)),
                  pl.BlockSpec((2, H_C, tq), lambda b, i: (0, 0, 0)),
                  pl.BlockSpec((H_C, 128), lambda b, i: (0, 0))],
        out_specs=[pl.BlockSpec((tq, 512), lambda b, i: (b * nq + i, 0)),
                   pl.BlockSpec((1, G_C, nbp, tq), lambda b, i: (b, 0, 0, i))],
        out_shape=[jax.ShapeDtypeStruct((batch * seq, 512), F32),
                   jax.ShapeDtypeStruct((batch, G_C, nbp, seq), BF16)],
        compiler_params=_cparams("arbitrary", "arbitrary"),
        name="nsa_cmpsel_prompt",
    )(q, cmp, near, cfar)


def _heads_t(q_t, h0, nh):
    return jnp.concatenate([q_t[DH * h:DH * (h + 1)] for h in range(h0, h0 + nh)], axis=1).astype(BF16)


def _fill_kv(src_ref, k_col, v_col, n_groups, k_scr, vt_scr, tq):
    for g in range(n_groups):
        k_scr[g] = src_ref[0, :, k_col + DH * g:k_col + DH * (g + 1)].astype(BF16)
    for kb in range(vt_scr.shape[0]):
        vt_scr[kb] = src_ref[0, kb * tq:(kb + 1) * tq, v_col:v_col + DH * n_groups].T.astype(BF16)


def _nsa_prompt_body(q_ref, nsa_ref, win_ref, sel_ref, bdt_ref, bst_ref, cfar_ref, oslc_ref, owin_ref,
                     ks_scr, vs_scr, kw_scr, vw_scr, m_ref, l_ref, acc_ref, *, tq, nbp):
    qi = pl.program_id(1)

    @pl.when(qi == 0)
    def _():
        _fill_kv(nsa_ref, 256, 384, G_C, ks_scr, vs_scr, tq)
        _fill_kv(win_ref, 0, 128, G_C, kw_scr, vw_scr, tq)

    q_t = (q_ref[...] * ATT_SCALE).T
    per_tile = tq // NSA_BLOCK
    n_back = WINDOW // tq
    shape = (tq, HPG_C * tq)
    key_i, qry_i = _iota(shape, 0), _iota(shape, 1) & (tq - 1)
    groups = range(G_C)
    heads = [range(HPG_C * g, HPG_C * (g + 1)) for g in groups]
    q_gs = [_heads_t(q_t, HPG_C * g, HPG_C) for g in groups]
    far_rows = [jnp.concatenate([jnp.broadcast_to(cfar_ref[h:h + 1, 0:1], (1, tq)) for h in heads[g]], axis=1)
                for g in groups]

    def tile(g, kb, mode, mask_t, k_scr, vt_scr):
        k = k_scr[g, pl.ds(pl.multiple_of(kb * tq, tq), tq), :]
        if mode == "far":
            bias = far_rows[g]
        else:
            tab = bdt_ref if mode == "diag" else bst_ref
            bias = jnp.concatenate([tab[h] for h in heads[g]], axis=1)
        _tsoftmax_step(_dot(k, q_gs[g]) + bias, mask_t, vt_scr[kb, DH * g:DH * (g + 1), :],
                       m_ref.at[g], l_ref.at[g], acc_ref.at[g])

    def init():
        for g in groups:
            _softmax_init(m_ref.at[g], l_ref.at[g], acc_ref.at[g])

    def outputs():
        return [_tsoftmax_out(l_ref.at[g], acc_ref.at[g])[:, j * tq:(j + 1) * tq]
                for g in groups for j in range(HPG_C)]

    def slc_tile(kb, mode):
        expand = _iota((tq, nbp), 1) == kb * per_tile + jnp.right_shift(_iota((tq, nbp), 0), 6)
        expand = jnp.where(expand, 1.0, 0.0).astype(BF16)
        for g in groups:
            m = _dot(expand, sel_ref[0, g]) > 0.5
            if mode == "diag":
                m = m & (_iota((tq, tq), 0) <= _iota((tq, tq), 1))
            tile(g, kb, mode, jnp.concatenate([m] * HPG_C, axis=1), ks_scr, vs_scr)

    init()

    def far(kb, c):
        slc_tile(kb, "far")
        return c

    lax.fori_loop(0, jnp.maximum(qi - 1, 0), far, 0)

    @pl.when(qi >= 1)
    def _():
        slc_tile(qi - 1, "sub")

    slc_tile(qi, "diag")
    oslc_ref[...] = jnp.concatenate(outputs(), axis=0).T

    init()
    for d in range(n_back, 0, -1):
        mask_t = (key_i >= qry_i) if d == n_back else None

        @pl.when(qi >= d)
        def _(d=d, mask_t=mask_t):
            for g in groups:
                tile(g, qi - d, "sub" if d == 1 else "far", mask_t, kw_scr, vw_scr)

    for g in groups:
        tile(g, qi, "diag", key_i <= qry_i, kw_scr, vw_scr)
    owin_ref[...] = jnp.concatenate(outputs(), axis=0).T


def _nsa_prompt_attn(q, nsa, win, sel, bdt, bst, cfar, batch, seq):
    tq = bdt.shape[1]
    assert tq & (tq - 1) == 0 and seq % tq == 0 and WINDOW % tq == 0 and tq >= MAX_DISTANCE
    nq = seq // tq
    nbp = sel.shape[2]
    tok = pl.BlockSpec((tq, 512), lambda b, i: (b * nq + i, 0))
    seq_spec = lambda w: pl.BlockSpec((1, seq, w), lambda b, i: (b, 0, 0))
    full = lambda a: pl.BlockSpec(a.shape, lambda b, i: (0,) * a.ndim, pipeline_mode=pl.Buffered(1))
    r = HPG_C * tq
    kv_scratch = [pltpu.VMEM((G_C, seq, DH), BF16), pltpu.VMEM((nq, G_C * DH, tq), BF16)]
    return pl.pallas_call(
        functools.partial(_nsa_prompt_body, tq=tq, nbp=nbp),
        grid=(batch, nq),
        in_specs=[tok, seq_spec(NSA_ROW), seq_spec(WIN_ROW),
                  pl.BlockSpec((1, G_C, nbp, tq), lambda b, i: (b, 0, 0, i)), full(bdt), full(bst), full(cfar)],
        out_specs=[tok, tok],
        out_shape=[jax.ShapeDtypeStruct((batch * seq, 512), F32)] * 2,
        scratch_shapes=kv_scratch + kv_scratch + [pltpu.VMEM((G_C, 1, r), F32), pltpu.VMEM((G_C, 1, r), F32),
                                                  pltpu.VMEM((G_C, DH, r), F32)],
        compiler_params=_cparams("arbitrary", "arbitrary"),
        name="nsa_slc_win_prompt",
    )(q, nsa.reshape(batch, seq, NSA_ROW), win.reshape(batch, seq, WIN_ROW), sel, bdt, bst, cfar)


def _sb_prompt_body(q_ref, rows_ref, u_ref, o_ref, k_scr, vt_scr, carry_ref, acc_ref, *, tq):
    qi = pl.program_id(1)

    @pl.when(qi == 0)
    def _():
        _fill_kv(rows_ref, 0, 256, KV_D, k_scr, vt_scr, tq)

    q_t = (q_ref[...] * (ATT_SCALE * LOG2E)).T
    shape = (tq, HPK_D * tq)
    before = _iota(shape, 0) < (_iota(shape, 1) & (tq - 1))
    q_gs = [_heads_t(q_t, HPK_D * g, HPK_D) for g in range(KV_D)]
    carry_ref[...] = jnp.zeros(carry_ref.shape, F32)
    acc_ref[...] = jnp.zeros(acc_ref.shape, F32)

    def step(kb, diag):
        for g in range(KV_D):
            k = k_scr[g, pl.ds(pl.multiple_of(kb * tq, tq), tq), :]
            z = _dot(k, q_gs[g])
            lsz = _log2_sigmoid(z)
            lf = lsz - z
            if diag:
                lf = jnp.where(before, lf, 0.0)
            between = _dot(u_ref[...], lf.astype(BF16))
            a = jnp.exp2(lsz + between + carry_ref[g])
            if diag:
                a = jnp.where(before, a, 0.0)
            acc_ref[g] += _dot(vt_scr[kb, DH * g:DH * (g + 1), :], a.astype(BF16))
            carry_ref[g] += jnp.sum(lf, axis=0, keepdims=True)

    step(qi, True)

    def back(it, c):
        step(qi - 1 - it, False)
        return c

    lax.fori_loop(0, qi, back, 0)
    outs = [acc_ref[g][:, j * tq:(j + 1) * tq] for g in range(KV_D) for j in range(HPK_D)]
    o_ref[...] = jnp.concatenate(outs, axis=0).T


def _sb_prompt(q, rows, batch, seq):
    tq = min(TQ_MLA, seq)
    assert tq & (tq - 1) == 0 and seq % tq == 0
    nq = seq // tq
    later = np.arange(tq)[None, :] > np.arange(tq)[:, None]
    u = jnp.asarray(later, BF16)
    tok = pl.BlockSpec((tq, 512), lambda b, i: (b * nq + i, 0))
    r = HPK_D * tq
    return pl.pallas_call(
        functools.partial(_sb_prompt_body, tq=tq),
        grid=(batch, nq),
        in_specs=[tok, pl.BlockSpec((1, seq, SB_ROW), lambda b, i: (b, 0, 0)),
                  pl.BlockSpec((tq, tq), lambda b, i: (0, 0))],
        out_specs=tok,
        out_shape=jax.ShapeDtypeStruct((batch * seq, 512), F32),
        scratch_shapes=[pltpu.VMEM((KV_D, seq, DH), BF16), pltpu.VMEM((nq, KV_D * DH, tq), BF16),
                        pltpu.VMEM((KV_D, 1, r), F32), pltpu.VMEM((KV_D, DH, r), F32)],
        compiler_params=_cparams("arbitrary", "arbitrary"),
        name="sb_prompt",
    )(q, rows.reshape(batch, seq, SB_ROW), u)


def _page_specs(block, pp, col_block, page_of):
    return [pl.BlockSpec(block, lambda b, s, pt, j=j: (page_of(b, s, j, pt), 0, col_block)) for j in range(pp)]


def _mla_decode_body(pt_ref, q_ref, knew_ref, wvb_ref, *rest, pp, valid):
    pages = rest[:pp]
    o_ref, m_ref, l_ref, acc_ref = rest[pp:]
    st = pl.program_id(1)
    r = SAMPLE_ROWS
    q = q_ref[0].astype(BF16)

    @pl.when(st == 0)
    def _():
        _softmax_init(m_ref, l_ref, acc_ref)

    k_t = jnp.concatenate([pages[j][0] for j in range(pp)], axis=1).astype(BF16)
    _softmax_step(_dot(q, k_t), None, k_t[0:KV_LORA], m_ref, l_ref, acc_ref, v_is_transposed=True)

    @pl.when(st == pl.num_programs(1) - 1)
    def _():
        kn = knew_ref[0].astype(BF16)
        shape = (H_B * r, r)
        col = _iota(shape, 1)
        mask = (col <= (_iota(shape, 0) & (r - 1))) & (col < valid)
        _softmax_step(_dot_nt(q, kn), mask, kn[:, 0:KV_LORA], m_ref, l_ref, acc_ref)
        o = _softmax_out(l_ref, acc_ref)
        olat = jnp.concatenate([o[r * h:r * (h + 1)] for h in range(H_B)], axis=1)
        o_ref[0] = _dot(olat.astype(BF16), wvb_ref[...])


def _mla_decode(page_table, q, knew, wvb, cache_t, valid):
    batch, n_pages = page_table.shape
    pp = min(4 * PAGES_PER_STEP, n_pages)
    assert n_pages % pp == 0
    r = SAMPLE_ROWS
    grid_spec = pltpu.PrefetchScalarGridSpec(
        num_scalar_prefetch=1,
        grid=(batch, n_pages // pp),
        in_specs=[pl.BlockSpec((1, H_B * r, MLA_ROW), lambda b, s, pt: (b, 0, 0)),
                  pl.BlockSpec((1, r, MLA_ROW), lambda b, s, pt: (b, 0, 0)),
                  pl.BlockSpec((1024, 512), lambda b, s, pt: (0, 0))]
        + _page_specs((1, MLA_ROW, PAGE), pp, 0, lambda b, s, j, pt: pt[b, s * pp + j]),
        out_specs=pl.BlockSpec((1, r, 512), lambda b, s, pt: (b, 0, 0)),
        scratch_shapes=[pltpu.VMEM((H_B * r, 1), F32), pltpu.VMEM((H_B * r, 1), F32),
                        pltpu.VMEM((H_B * r, KV_LORA), F32)])
    return pl.pallas_call(
        functools.partial(_mla_decode_body, pp=pp, valid=valid),
        grid_spec=grid_spec,
        out_shape=jax.ShapeDtypeStruct((batch, r, 512), F32),
        compiler_params=_cparams("arbitrary", "arbitrary"),
        name="mla_decode",
    )(page_table, q, knew, wvb, *([cache_t] * pp))


def _means_decode_body(pt_ref, *rest, pp):
    pages = rest[:pp]
    o_ref = rest[pp]
    o_ref[0] = _block_means(jnp.concatenate([pages[j][0] for j in range(pp)], axis=0))


def _means_decode(page_table, cache):
    batch, n_pages = page_table.shape
    pp = min(4 * PAGES_PER_STEP, n_pages)
    assert n_pages % pp == 0
    per_page = PAGE // NSA_BLOCK
    grid_spec = pltpu.PrefetchScalarGridSpec(
        num_scalar_prefetch=1,
        grid=(batch, n_pages // pp),
        in_specs=_page_specs((1, PAGE, 256), pp, 0, lambda b, s, j, pt: pt[b, s * pp + j]),
        out_specs=pl.BlockSpec((1, pp * per_page, 256), lambda b, s, pt: (b, s, 0)))
    return pl.pallas_call(
        functools.partial(_means_decode_body, pp=pp),
        grid_spec=grid_spec,
        out_shape=jax.ShapeDtypeStruct((batch, n_pages * per_page, 256), F32),
        compiler_params=_cparams("arbitrary", "arbitrary"),
        name="nsa_means_decode",
    )(page_table, *([cache] * pp))


def _cmpsel_decode_body(q_ref, cmp_ref, cb_ref, ocmp_ref, idx_ref, *, nb, n_past_sel):
    r = SAMPLE_ROWS
    blk = _iota((1, nb), 1)
    forced = (blk == 0) | (blk == nb - 1)
    lane = _iota((r, 128), 1)
    for g in range(G_C):
        ck = cmp_ref[0, :, DH * g:DH * (g + 1)].astype(BF16)
        cv = cmp_ref[0, :, 128 + DH * g:128 + DH * (g + 1)].astype(BF16)
        imp = jnp.zeros((r, nb), F32)
        for j in range(HPG_C):
            h = HPG_C * g + j
            o, p = _cmp_head(q_ref[0, :, DH * h:DH * (h + 1)], ck, cv, cb_ref[h], None)
            ocmp_ref[0, :, DH * h:DH * (h + 1)] = o
            imp = imp + p
        score = imp + jnp.where(forced, FORCE_BONUS, 0.0)
        picked = jnp.zeros((r, 128), jnp.int32)
        for slot, (idx, _) in enumerate(_top_blocks(score, blk, n_past_sel)):
            picked = jnp.where(lane == slot, idx, picked)
        idx_ref[0, g] = picked


def _cmpsel_decode(q, cmp, cb, n_past_sel):
    batch, nb, _ = cmp.shape
    r = SAMPLE_ROWS
    return pl.pallas_call(
        functools.partial(_cmpsel_decode_body, nb=nb, n_past_sel=n_past_sel),
        grid=(batch,),
        in_specs=[pl.BlockSpec((1, r, 512), lambda b: (b, 0, 0)),
                  pl.BlockSpec((1, nb, 256), lambda b: (b, 0, 0)),
                  pl.BlockSpec((H_C, r, nb), lambda b: (0, 0, 0))],
        out_specs=[pl.BlockSpec((1, r, 512), lambda b: (b, 0, 0)),
                   pl.BlockSpec((1, G_C, r, 128), lambda b: (b, 0, 0, 0))],
        out_shape=[jax.ShapeDtypeStruct((batch, r, 512), F32),
                   jax.ShapeDtypeStruct((batch, G_C, r, 128), jnp.int32)],
        compiler_params=_cparams("arbitrary"),
        name="nsa_cmpsel_decode",
    )(q, cmp, cb)


def _dist_bias(tab3_ref, dist):
    nd = tab3_ref.shape[1]
    width = dist.shape[1]
    onehot = jnp.where(_iota((nd, width), 0) == jnp.clip(dist, 0, nd - 1), 1.0, 0.0).astype(BF16)
    b = _dot(tab3_ref[...], onehot)
    return b[0:8] + b[8:16] + b[16:24]


def _slc_decode_body(idx_ref, pt_ref, q_ref, new_ref, tab3_ref, *rest, n_slots, p_len, n_tok):
    blocks = rest[:n_tok * G_C * n_slots]
    o_ref = rest[n_tok * G_C * n_slots]
    b = pl.program_id(0)
    r = SAMPLE_ROWS
    head_row = _iota((H_C, 1), 0)
    col = _iota((H_C, r), 1)
    n_keys = n_slots * NSA_BLOCK
    lane = _iota((1, n_keys), 1)
    slot_of = jnp.right_shift(lane, 6)
    for t in range(n_tok):
        q = (q_ref[0, t] * ATT_SCALE).astype(BF16)
        out = jnp.zeros((H_C, DH), F32)
        for g in range(G_C):
            kvs = [blocks[(t * G_C + g) * n_slots + s][0] for s in range(n_slots)]
            k = jnp.concatenate([kv[:, DH * g:DH * (g + 1)] for kv in kvs], axis=0).astype(BF16)
            v = jnp.concatenate([kv[:, 128 + DH * g:128 + DH * (g + 1)] for kv in kvs], axis=0).astype(BF16)
            first = jnp.zeros((1, n_keys), jnp.int32)
            for s in range(n_slots):
                n = idx_ref[((b * G_C + g) * r + t) * n_slots + s]
                first = jnp.where(slot_of == s, p_len + t - n * NSA_BLOCK, first)
            sc = _dot_nt(q, k) + _dist_bias(tab3_ref, first - (lane & (NSA_BLOCK - 1)))
            kn = new_ref[0, :, 256 + DH * g:256 + DH * (g + 1)].astype(BF16)
            vn = new_ref[0, :, 384 + DH * g:384 + DH * (g + 1)].astype(BF16)
            cur_ok = col <= t
            sc_new = jnp.where(cur_ok, _dot_nt(q, kn) + _dist_bias(tab3_ref, t - _iota((1, r), 1)), NEG)
            m = jnp.maximum(jnp.max(sc, axis=1, keepdims=True), jnp.max(sc_new, axis=1, keepdims=True))
            p = jnp.exp(sc - m)
            p_new = jnp.where(cur_ok, jnp.exp(sc_new - m), 0.0)
            l = jnp.sum(p, axis=1, keepdims=True) + jnp.sum(p_new, axis=1, keepdims=True)
            acc = _dot(p.astype(BF16), v) + _dot(p_new.astype(BF16), vn)
            in_group = (head_row >= HPG_C * g) & (head_row < HPG_C * (g + 1))
            out = jnp.where(in_group, acc / l, out)
        o_ref[0, t] = out


def _slc_decode(idx, page_table, q, new_rows, tab3, cache, n_tok, p_len):
    batch, n_pages = page_table.shape
    r = SAMPLE_ROWS
    n_slots = idx.shape[0] // (batch * G_C * r)
    per_page = PAGE // NSA_BLOCK

    def slot_spec(t, g, s):
        def index(b, idx_ref, pt):
            n = idx_ref[((b * G_C + g) * r + t) * n_slots + s]
            return (pt[b, n // per_page], n % per_page, 1)
        return pl.BlockSpec((1, NSA_BLOCK, 256), index)

    slots = [slot_spec(t, g, s) for t in range(n_tok) for g in range(G_C) for s in range(n_slots)]
    grid_spec = pltpu.PrefetchScalarGridSpec(
        num_scalar_prefetch=2,
        grid=(batch,),
        in_specs=[pl.BlockSpec((1, n_tok, H_C, DH), lambda b, i, pt: (b, 0, 0, 0)),
                  pl.BlockSpec((1, r, NSA_ROW), lambda b, i, pt: (b, 0, 0)),
                  pl.BlockSpec(tab3.shape, lambda b, i, pt: (0, 0))] + slots,
        out_specs=pl.BlockSpec((1, n_tok, H_C, DH), lambda b, i, pt: (b, 0, 0, 0)))
    return pl.pallas_call(
        functools.partial(_slc_decode_body, n_slots=n_slots, p_len=p_len, n_tok=n_tok),
        grid_spec=grid_spec,
        out_shape=jax.ShapeDtypeStruct((batch, n_tok, H_C, DH), F32),
        compiler_params=_cparams("arbitrary"),
        name="nsa_slc_decode",
    )(idx, page_table, q, new_rows, tab3, *([cache] * len(slots)))


def _win_decode_body(q_ref, state_ref, new_ref, wbs_ref, wbn_ref, o_ref, *, valid):
    r = SAMPLE_ROWS
    wb = state_ref.shape[1]
    for g in range(G_C):
        q4 = (jnp.concatenate([q_ref[0, :, DH * h:DH * (h + 1)] for h in range(HPG_C * g, HPG_C * (g + 1))],
                              axis=0) * ATT_SCALE).astype(BF16)
        ks = state_ref[0, :, DH * g:DH * (g + 1)].astype(BF16)
        vs = state_ref[0, :, 128 + DH * g:128 + DH * (g + 1)].astype(BF16)
        kn = new_ref[0, :, DH * g:DH * (g + 1)].astype(BF16)
        vn = new_ref[0, :, 128 + DH * g:128 + DH * (g + 1)].astype(BF16)
        tok = _iota((HPG_C * r, 1), 0) & (r - 1)
        ok_s = _iota((HPG_C * r, wb), 1) >= tok + (wb - WINDOW)
        col_n = _iota((HPG_C * r, r), 1)
        ok_n = (col_n <= tok) & (col_n < valid)
        bias_s = jnp.concatenate([wbs_ref[h] for h in range(HPG_C * g, HPG_C * (g + 1))], axis=0)
        bias_n = jnp.concatenate([wbn_ref[h] for h in range(HPG_C * g, HPG_C * (g + 1))], axis=0)
        s_s = jnp.where(ok_s, _dot_nt(q4, ks) + bias_s, NEG)
        s_n = jnp.where(ok_n, _dot_nt(q4, kn) + bias_n, NEG)
        m = jnp.maximum(jnp.max(s_s, axis=1, keepdims=True), jnp.max(s_n, axis=1, keepdims=True))
        p_s = jnp.where(ok_s, jnp.exp(s_s - m), 0.0)
        p_n = jnp.where(ok_n, jnp.exp(s_n - m), 0.0)
        l = jnp.sum(p_s, axis=1, keepdims=True) + jnp.sum(p_n, axis=1, keepdims=True)
        o = (_dot(p_s.astype(BF16), vs) + _dot(p_n.astype(BF16), vn)) / l
        for j in range(HPG_C):
            h = HPG_C * g + j
            o_ref[0, :, DH * h:DH * (h + 1)] = o[r * j:r * (j + 1)]


def _win_decode(q, state, new_rows, wbs, wbn, valid):
    batch, wb, _ = state.shape
    r = SAMPLE_ROWS
    return pl.pallas_call(
        functools.partial(_win_decode_body, valid=valid),
        grid=(batch,),
        in_specs=[pl.BlockSpec((1, r, 512), lambda b: (b, 0, 0)),
                  pl.BlockSpec((1, wb, WIN_ROW), lambda b: (b, 0, 0)),
                  pl.BlockSpec((1, r, WIN_ROW), lambda b: (b, 0, 0)),
                  pl.BlockSpec((H_C, r, wb), lambda b: (0, 0, 0)),
                  pl.BlockSpec((H_C, r, r), lambda b: (0, 0, 0))],
        out_specs=pl.BlockSpec((1, r, 512), lambda b: (b, 0, 0)),
        out_shape=jax.ShapeDtypeStruct((batch, r, 512), F32),
        compiler_params=_cparams("arbitrary"),
        name="nsa_win_decode",
    )(q, state, new_rows, wbs, wbn)


def _suffix_sum(x, block=None):
    n = x.shape[0]
    block = block or n
    row = _iota(x.shape, 0) & (block - 1)
    sh = 1
    while sh < block:
        x = x + jnp.where(row < block - sh, pltpu.roll(x, n - sh, axis=0), 0.0)
        sh *= 2
    return x


def _sb_decode_body(pt_ref, wq_ref, new_ref, *rest, pp, valid):
    pages = rest[:pp]
    o_ref, carry_ref, acc_ref, a_ref, v_ref = rest[pp:]
    st = pl.program_id(1)
    r = SAMPLE_ROWS
    half = 64
    wq = wq_ref[0].astype(BF16)
    lane = _iota((1, 128), 1)

    @pl.when(st == 0)
    def _():
        kn = new_ref[0]
        kk = kn[:, 0:256].astype(BF16)
        z = _dot(jnp.concatenate([kk, kk], axis=1), wq)
        lsz = _log2_sigmoid(z)
        row = _iota((r, 128), 0)
        mask = (row < (_iota((r, 128), 1) & (r - 1))) & (row < valid)
        lf = jnp.where(mask, lsz - z, 0.0)
        a = jnp.where(mask, jnp.exp2(lsz + _suffix_sum(lf) - lf), 0.0)
        pad = lambda x: jnp.concatenate([x, jnp.zeros((PAGE - r, x.shape[1]), x.dtype)], axis=0)
        vn = jnp.concatenate([kn[:, 256:512], jnp.zeros((r, 256), F32)], axis=1)
        acc_ref[...] = _dot_tn(pad(a).astype(BF16), pad(vn).astype(BF16))
        carry_ref[...] = jnp.sum(lf, axis=0, keepdims=True)

    carry = carry_ref[...]
    for i in range(pp // 2 - 1, -1, -1):
        early, late = pages[2 * i][0], pages[2 * i + 1][0]
        z = _dot(jnp.concatenate([early[:, 0:256], late[:, 0:256]], axis=1).astype(BF16), wq)
        lsz = _log2_sigmoid(z)
        lf = lsz - z
        suf = _suffix_sum(lf)
        tot = suf[0:1]
        swapped = pltpu.roll(jnp.broadcast_to(tot, (8, 128)), half, axis=1)[0:1]
        between = (suf - lf) + (carry + jnp.where(lane < half, swapped, 0.0))
        a_ref[PAGE * i:PAGE * (i + 1), :] = jnp.exp2(lsz + between).astype(BF16)
        v_ref[PAGE * i:PAGE * (i + 1), :] = jnp.concatenate([early[:, 256:512], late[:, 256:512]],
                                                            axis=1).astype(BF16)
        carry = carry + tot + swapped
    carry_ref[...] = carry
    acc_ref[...] += _dot_tn(a_ref[...], v_ref[...])

    @pl.when(st == pl.num_programs(1) - 1)
    def _():
        o_ref[0] = acc_ref[...]


def _sb_decode(page_table, wq, new_rows, cache, valid):
    batch, n_pages = page_table.shape
    pp = min(2 * PAGES_PER_STEP, n_pages)
    assert n_pages % pp == 0 and pp % 2 == 0
    r = SAMPLE_ROWS
    grid_spec = pltpu.PrefetchScalarGridSpec(
        num_scalar_prefetch=1,
        grid=(batch, n_pages // pp),
        in_specs=[pl.BlockSpec((1, 512, 128), lambda b, s, pt: (b, 0, 0)),
                  pl.BlockSpec((1, r, SB_ROW), lambda b, s, pt: (b, 0, 0))]
        + _page_specs((1, PAGE, SB_ROW), pp, 0, lambda b, s, j, pt: pt[b, n_pages - (s + 1) * pp + j]),
        out_specs=pl.BlockSpec((1, 128, 512), lambda b, s, pt: (b, 0, 0)),
        scratch_shapes=[pltpu.VMEM((1, 128), F32), pltpu.VMEM((128, 512), F32),
                        pltpu.VMEM((pp // 2 * PAGE, 128), BF16), pltpu.VMEM((pp // 2 * PAGE, 512), BF16)])
    return pl.pallas_call(
        functools.partial(_sb_decode_body, pp=pp, valid=valid),
        grid_spec=grid_spec,
        out_shape=jax.ShapeDtypeStruct((batch, 128, 512), F32),
        compiler_params=_cparams("arbitrary", "arbitrary"),
        name="sb_decode",
    )(page_table, wq, new_rows, *([cache] * pp))


def _rope_tables(pos):
    half = ROPE_D // 2
    inv = ROPE_BASE ** (-jnp.arange(half, dtype=F32) / half)
    ang = pos.astype(F32)[:, None] * inv[None, :]
    cos, sin = jnp.cos(ang), jnp.sin(ang)
    return jnp.tile(cos, (1, 8)), jnp.tile(sin, (1, 8))


def _rot_cols(w):
    half = ROPE_D // 2
    return jnp.concatenate([-w[..., half:], w[..., :half]], axis=-1)


def _prep_even(w_in, w_qb, w_kb, w_vb):
    zkr = w_in[:, 2432:2464]
    w_e = jnp.concatenate([w_in[:, :2432], jnp.tile(zkr, (1, 4)), jnp.tile(_rot_cols(zkr), (1, 4))], axis=1)
    wqb = w_qb.reshape(Q_LORA, H_B, NOPE + ROPE_D)
    wr = wqb[:, :, NOPE:]
    w_q = jnp.concatenate([wqb[:, :, :NOPE].reshape(Q_LORA, 512), wr.reshape(Q_LORA, 256),
                           _rot_cols(wr).reshape(Q_LORA, 256)], axis=1)
    eye = jnp.eye(H_B, dtype=F32)
    wkb = jnp.einsum('hcn,hg->hngc', w_kb, eye).reshape(H_B * NOPE, H_B * KV_LORA)
    wvb = jnp.einsum('hcd,hg->hcgd', w_vb, eye).reshape(H_B * KV_LORA, H_B * VD_B)
    return w_e.astype(BF16), w_q.astype(BF16), wkb.astype(BF16), wvb.astype(BF16)


def _prep_odd(w_in):
    w = jnp.concatenate([w_in[:, :1280], w_in[:, 1304:2328], w_in[:, 1280:1304],
                         jnp.zeros((D_MODEL, ODD_COLS - 2328), w_in.dtype)], axis=1)
    return w.astype(BF16)


def _gate_expand():
    e = np.zeros((128, 3 * 512), np.float32)
    for h in range(H_C):
        for j in range(3):
            e[3 * h + j, 512 * j + DH * h:512 * j + DH * (h + 1)] = 1.0
    return jnp.asarray(e, BF16)


def _hgrn_params(hgrn_lb, norm_g, lj):
    lb_all = jax.nn.softmax(hgrn_lb.astype(F32), axis=0)
    lb = (jnp.cumsum(lb_all, axis=0) - lb_all[0])[lj]
    lb_pos = lb > 0
    rows = [jnp.log(jnp.where(lb_pos, lb, 1.0)), jnp.log1p(-lb), lb_pos.astype(F32), 1.0 - lb, norm_g[lj]]
    return jnp.concatenate([jnp.stack(rows), jnp.zeros((3, 512), F32)], axis=0)


def _t5_bucket(dist):
    exact = NUM_BUCKETS // 2
    d = jnp.maximum(dist, 0)
    large = exact + (jnp.log(jnp.maximum(d, 1).astype(F32) / exact)
                     / math.log(MAX_DISTANCE / exact) * (NUM_BUCKETS - exact)).astype(jnp.int32)
    return jnp.where(d < exact, d, jnp.minimum(large, NUM_BUCKETS - 1))


N_DIST = 256


def _toeplitz_t(tab, offset, n):
    i = jnp.arange(n, dtype=jnp.int32)[None, :]
    j = jnp.arange(n, dtype=jnp.int32)[:, None]
    idx = jnp.clip(offset + i - j, 0, N_DIST - 1).reshape(1, n * n)
    onehot = (idx == jnp.arange(N_DIST, dtype=jnp.int32)[:, None]).astype(F32)
    return jnp.dot(tab, onehot, precision=lax.Precision.HIGHEST).reshape(H_C, n, n)


def _decode_bias_rows(tab, base, n_keys, r):
    far = jnp.broadcast_to(tab[:, N_DIST - 1:], (H_C, n_keys + r))
    rev = jnp.concatenate([far, tab[:, ::-1], jnp.zeros((H_C, n_keys + r), F32)], axis=1)
    rows = []
    for t in range(r):
        start = n_keys + r + N_DIST - 1 - (base + t)
        rows.append(rev[:, start:start + n_keys])
    return jnp.stack(rows, axis=1)


def _pad_rows(a, batch, n_tok):
    a = a.reshape(batch, n_tok, -1)
    return jnp.pad(a, ((0, 0), (0, SAMPLE_ROWS - n_tok), (0, 0)))


def _last_rows(rows, n):
    t = rows.shape[1]
    if t < n:
        rows = jnp.pad(rows, ((0, 0), (n - t, 0), (0, 0)))
    return rows[:, rows.shape[1] - n:]


def _run_prompt(x, mod, w, tabs, batch, seq, win_len):
    assert seq % HGRN_CHUNK == 0 and (batch * seq) % _tile_rows(batch * seq) == 0
    tq = min(TQ, seq)
    pos = np.arange(seq)
    cos, sin = _rope_tables(jnp.asarray(pos, jnp.int32))
    tab = tabs['tab']
    tq_att = min(TQ_MLA, seq)
    bdt, bst = _toeplitz_t(tab, 0, tq_att), _toeplitz_t(tab, tq_att, tq_att)
    nblk = seq // NSA_BLOCK
    nbp = -(-nblk // 16) * 16
    reps = tq // NSA_BLOCK
    near = jnp.stack([jnp.tile(tab[:, 1:NSA_BLOCK + 1], (1, reps)),
                      jnp.tile(tab[:, NSA_BLOCK + 1:2 * NSA_BLOCK + 1], (1, reps))])
    states = []
    for l in range(DEPTH):
        lj = l // 2
        m = mod[l]
        x = _ffn(x, m, w['norm_g'][l], w['wg'], w['wu'], w['wd'], l, 0, 0, seq)
        if l % 2 == 0:
            w_e, w_q, wkb, wvb = w['even'][lj]
            zh, kfull, rows, qlat, qrope = _inproj_even(
                x, m, w['norm_g'][l], w_e, cos, sin, w['gq'][lj], w['gkv'][lj], w_q, wkb, seq)
            tb = min(TM_DENSE, seq)
            o_a, s_new = _hgrn(zh, w['hgrn_par'][lj], batch, seq, tb, HGRN_CHUNK, HGRN_SUB, HGRN_CHUNK)
            o_b = _mla_prompt(qlat, qrope, kfull, wvb.T, batch, seq)
            x = _outproj(x, m, [o_a, o_b], [], w['wout_even'][lj], seq, odd=False)
            states += [rows.reshape(batch, seq, MLA_ROW), s_new]
        else:
            qn, nsa, win, qs, sb, zg = _inproj_odd(x, m, w['norm_g'][l], w['odd'][lj], seq)
            cmp = _means_prompt(nsa).reshape(batch, nblk, 256)
            cmp = jnp.pad(cmp, ((0, 0), (0, nbp - nblk), (0, 0)))
            o_cmp, sel = _cmpsel_prompt(qn, cmp, near, tabs['cfar'], batch, seq)
            o_slc, o_win = _nsa_prompt_attn(qn, nsa, win, sel, bdt, bst, tabs['cfar'], batch, seq)
            o_sb = _sb_prompt(qs, sb, batch, seq)
            x = _outproj(x, m, [o_cmp, o_slc, o_win, zg, o_sb], [w['gate_e']], w['wout_odd'][lj], seq, odd=True)
            states += [nsa.reshape(batch, seq, NSA_ROW), _last_rows(win.reshape(batch, seq, WIN_ROW), win_len),
                       sb.reshape(batch, seq, SB_ROW)]
        x = _ffn(x, m, w['norm_g'][l], w['wg'], w['wu'], w['wd'], l, 1, 2, seq,
                 final_g=w['final_g'] if l == DEPTH - 1 else None)
    return x.reshape(batch, seq, D_MODEL), states


def _run_sample(x, mod, w, tabs, batch, n_tok, page_table, pasts):
    n = batch * n_tok
    n_pages = page_table.shape[1]
    p_len = n_pages * PAGE
    r = SAMPLE_ROWS
    assert n_tok <= r and n_tok <= NSA_BLOCK and n == _tile_rows(n)
    pos = p_len + np.arange(r)
    cos, sin = _rope_tables(jnp.asarray(np.tile(pos[:n_tok], batch), jnp.int32))
    tab = tabs['tab']
    nb = p_len // NSA_BLOCK
    n_past_sel = min(TOP_N, nb + 1) - 1
    assert nb >= 2 and p_len % NSA_BLOCK == 0
    cb = jnp.concatenate([jnp.broadcast_to(tab[:, None, N_DIST - 1:], (H_C, r, nb - 2)),
                          tab[:, NSA_BLOCK + 1:NSA_BLOCK + 1 + r, None], tab[:, 1:1 + r, None]], axis=2)
    states = []
    for l in range(DEPTH):
        lj = l // 2
        m = mod[l]
        x = _ffn(x, m, w['norm_g'][l], w['wg'], w['wu'], w['wd'], l, 0, 0, n)
        if l % 2 == 0:
            cache_mla, state_hgrn = pasts[l]
            w_e, w_q, wkb, wvb = w['even'][lj]
            zh, kfull, rows, qlat, qrope = _inproj_even(
                x, m, w['norm_g'][l], w_e, cos, sin, w['gq'][lj], w['gkv'][lj], w_q, wkb, n)
            zh8 = _pad_rows(zh, batch, n_tok).reshape(batch * r, 2048)
            o_a8, s_new = _hgrn(zh8, w['hgrn_par'][lj], batch, r, r, r, r, n_tok, s0=state_hgrn)
            o_a = o_a8.reshape(batch, r, 512)[:, :n_tok].reshape(n, 512)
            qf = jnp.concatenate([qlat.reshape(batch, n_tok, H_B, KV_LORA),
                                  qrope.reshape(batch, n_tok, H_B, ROPE_D)], axis=-1)
            qf = jnp.pad(qf, ((0, 0), (0, r - n_tok), (0, 0), (0, 0)))
            qf = jnp.swapaxes(qf, 1, 2).reshape(batch, H_B * r, MLA_ROW)
            o_b8 = _mla_decode(page_table, qf, _pad_rows(rows, batch, n_tok), wvb, jnp.swapaxes(cache_mla, 1, 2),
                               n_tok)
            o_b = o_b8[:, :n_tok].reshape(n, 512)
            x = _outproj(x, m, [o_a, o_b], [], w['wout_even'][lj], n, odd=False)
            states += [rows.reshape(batch, n_tok, MLA_ROW), s_new]
        else:
            cache_nsa, state_win, cache_sb = pasts[l]
            wb = state_win.shape[1]
            assert p_len >= wb and wb <= WINDOW
            qn, nsa, win, qs, sb, zg = _inproj_odd(x, m, w['norm_g'][l], w['odd'][lj], n)
            qn8 = _pad_rows(qn, batch, n_tok)
            nsa8 = _pad_rows(nsa, batch, n_tok)
            cmp = _means_decode(page_table, cache_nsa)
            o_cmp8, picked = _cmpsel_decode(qn8, cmp, cb, n_past_sel)
            idx = picked[..., :n_past_sel].reshape(-1)
            o_slc = _slc_decode(idx, page_table, qn.reshape(batch, n_tok, H_C, DH), nsa8, tabs['tab3'],
                                cache_nsa, n_tok, p_len)
            wbs = _decode_bias_rows(tab, wb, wb, r)
            wbn = _decode_bias_rows(tab, 0, r, r)
            o_win8 = _win_decode(qn8, state_win, _pad_rows(win, batch, n_tok), wbs, wbn, n_tok)
            q5 = _pad_rows(qs * (ATT_SCALE * LOG2E), batch, n_tok).reshape(batch, r, KV_D, HPK_D, DH)
            q5 = jnp.transpose(q5, (0, 2, 4, 3, 1)).reshape(batch, KV_D, DH, HPK_D * r)
            wq = jnp.einsum('bgdc,gh->bgdhc', q5, jnp.eye(KV_D, dtype=F32)).reshape(batch, KV_D * DH, KV_D * HPK_D * r)
            nc = KV_D * HPK_D * r
            zc = jnp.zeros_like(wq)
            wq = jnp.concatenate([jnp.concatenate([wq, zc], axis=2), jnp.concatenate([zc, wq], axis=2)], axis=1)
            o_raw = _sb_decode(page_table, wq, _pad_rows(sb, batch, n_tok), cache_sb, n_tok)
            o6 = (o_raw[:, :nc, :256] + o_raw[:, nc:, 256:]).reshape(batch, KV_D, HPK_D, r, KV_D, DH)
            o_sb = jnp.stack([o6[:, g, :, :, g] for g in range(KV_D)], axis=1)
            o_sb = jnp.transpose(o_sb, (0, 3, 1, 2, 4))[:, :n_tok].reshape(n, 512)
            take = lambda a: a[:, :n_tok].reshape(n, 512)
            x = _outproj(x, m, [take(o_cmp8), o_slc.reshape(n, 512), take(o_win8), zg, o_sb], [w['gate_e']],
                         w['wout_odd'][lj], n, odd=True)
            new_win = jnp.concatenate([state_win, win.reshape(batch, n_tok, WIN_ROW)], axis=1)[:, n_tok:]
            states += [nsa.reshape(batch, n_tok, NSA_ROW), new_win, sb.reshape(batch, n_tok, SB_ROW)]
        x = _ffn(x, m, w['norm_g'][l], w['wg'], w['wu'], w['wd'], l, 1, 2, n,
                 final_g=w['final_g'] if l == DEPTH - 1 else None)
    return x.reshape(batch, n_tok, D_MODEL), states


def kernel(x_prompt, x_sample, cache_mla_l0, state_hgrn_l0, cache_nsa_l1, state_win_l1, cache_sb_l1, cache_mla_l2, state_hgrn_l2, cache_nsa_l3, state_win_l3, cache_sb_l3, page_table, c_prompt, c_sample, w_ada, b_ada, norm_g, ffn_w_gate, ffn_w_up, ffn_w_down, w_in_even, w_out_even, hgrn_lb, hgrn_norm_g, mla_q_norm_g, mla_kv_norm_g, mla_w_qb, mla_w_kb, mla_w_vb, w_in_odd, w_out_odd, rel_bias, final_norm_g):
    bp, seq, _ = x_prompt.shape
    bs, n_tok, _ = x_sample.shape
    n_even = w_in_even.shape[0]
    n_odd = w_in_odd.shape[0]
    w = dict(
        norm_g=norm_g, final_g=final_norm_g,
        wg=ffn_w_gate.astype(BF16), wu=ffn_w_up.astype(BF16), wd=ffn_w_down.astype(BF16),
        even=[_prep_even(w_in_even[j], mla_w_qb[j], mla_w_kb[j], mla_w_vb[j]) for j in range(n_even)],
        odd=[_prep_odd(w_in_odd[j]) for j in range(n_odd)],
        wout_even=w_out_even.astype(BF16), wout_odd=w_out_odd.astype(BF16),
        gq=mla_q_norm_g.reshape(n_even, 1, Q_LORA), gkv=mla_kv_norm_g.reshape(n_even, 1, KV_LORA),
        hgrn_par=[_hgrn_params(hgrn_lb, hgrn_norm_g, j) for j in range(n_even)],
        gate_e=_gate_expand())
    tab = rel_bias[_t5_bucket(jnp.arange(N_DIST, dtype=jnp.int32))].T.astype(F32)
    cfar = jnp.concatenate([tab[:, N_DIST - 1:], tab[:, 0:1], jnp.zeros((H_C, 126), F32)], axis=1)
    tabs = dict(tab=tab, cfar=cfar, tab3=jnp.concatenate(_split3(tab), axis=0))
    mod = _ada_mod(jnp.concatenate([c_prompt, c_sample], axis=0), w_ada, b_ada)
    mod = mod.reshape(DEPTH, bp + bs, N_MOD, D_MODEL)
    mod_p = mod[:, :bp, :, None, :]
    mod_s = jnp.transpose(jnp.repeat(mod[:, bp:], n_tok, axis=1), (0, 2, 1, 3))[:, None]
    win_len = state_win_l1.shape[1]
    y_p, st_p = _run_prompt(x_prompt.reshape(bp * seq, D_MODEL), mod_p, w, tabs, bp, seq, win_len)
    pasts = [(cache_mla_l0, state_hgrn_l0), (cache_nsa_l1, state_win_l1, cache_sb_l1),
             (cache_mla_l2, state_hgrn_l2), (cache_nsa_l3, state_win_l3, cache_sb_l3)]
    y_s, st_s = _run_sample(x_sample.reshape(bs * n_tok, D_MODEL), mod_s, w, tabs, bs, n_tok, page_table, pasts)
    out = [y_p, y_s]
    for a, b in zip(st_p, st_s):
        out += [a, b]
    return tuple(out)
```
